```python
import jax
import jax.numpy as jnp
from jax import lax
import numpy as np

D_MODEL = 1024
BATCH = 2
SEQ = 8192
DEPTH = 4
DEC_BATCH = 8
DEC_SEQ = 16
PAST_LEN = 2048

CHUNK = 64
QBLOCK = 128
RMS_EPS = 1e-6
NEG_BIG = -1e30

RWKV_HEADS = 8
RWKV_HEAD_DIM = 64
RWKV_WIDTH = RWKV_HEADS * RWKV_HEAD_DIM
DECAY_LORA = 64
AAA_LORA = 64
RWKV_SHIFT_WIDTH = 3 * RWKV_WIDTH + DECAY_LORA + AAA_LORA
GN_EPS = 64e-5

MLA_HEADS = 8
MLA_NOPE = 64
MLA_ROPE = 32
MLA_V = 64
MLA_Q_LORA = 384
MLA_KV_LORA = 256
MLA_WIDTH = MLA_HEADS * MLA_V
ROPE_THETA = 10000.0

SB_HEADS = 8
SB_HEAD_DIM = 64
SB_WIDTH = SB_HEADS * SB_HEAD_DIM

N_BRANCH = 3
IN_SIZES = (RWKV_SHIFT_WIDTH, RWKV_WIDTH, MLA_Q_LORA, MLA_KV_LORA, MLA_ROPE, MLA_WIDTH,
            SB_WIDTH, SB_WIDTH, SB_WIDTH, SB_WIDTH, N_BRANCH * D_MODEL)
D_IN = (RWKV_SHIFT_WIDTH + RWKV_WIDTH + MLA_Q_LORA + MLA_KV_LORA + MLA_ROPE + MLA_WIDTH
        + 4 * SB_WIDTH + N_BRANCH * D_MODEL)
RWKV_SIZES = (RWKV_WIDTH, RWKV_WIDTH, RWKV_WIDTH, DECAY_LORA, AAA_LORA)

kernel_name = "hybrid_rwkv7_mla_stickbreaking_stream_step"


def _split(x, sizes):
    return jnp.split(x, np.cumsum(sizes)[:-1].tolist(), axis=-1)


def rms_norm(x, g, eps=RMS_EPS):
    xf = x.astype(jnp.float32)
    return xf * lax.rsqrt(jnp.mean(xf * xf, axis=-1, keepdims=True) + eps) * g.astype(jnp.float32)


def rope(x, pos):
    half = MLA_ROPE // 2
    inv = ROPE_THETA ** (-jnp.arange(half, dtype=jnp.float32) / half)
    ang = pos.astype(jnp.float32)[..., None] * inv
    cos, sin = jnp.cos(ang), jnp.sin(ang)
    x1, x2 = x[..., :half], x[..., half:]
    return jnp.concatenate([x1 * cos - x2 * sin, x1 * sin + x2 * cos], axis=-1)


def sweep_queries(fn, q, q_pos):
    T = q.shape[1]
    if T <= QBLOCK or T % QBLOCK:
        return fn(q, q_pos)
    nb = T // QBLOCK
    qb = jnp.moveaxis(q.reshape(q.shape[0], nb, QBLOCK, *q.shape[2:]), 1, 0)
    pb = q_pos.reshape(nb, QBLOCK)
    ob = lax.map(lambda args: fn(args[0], args[1]), (qb, pb))
    return jnp.moveaxis(ob, 0, 1).reshape(q.shape[0], T, *ob.shape[3:])


def chunk_softmax_attn(q, k, v, q_pos, k_pos):
    s = jnp.einsum('bqhd,bshd->bhqs', q, k).astype(jnp.float32) * (q.shape[-1] ** -0.5)
    mask = (k_pos // CHUNK)[None, :] <= (q_pos // CHUNK)[:, None]
    p = jax.nn.softmax(jnp.where(mask, s, NEG_BIG), axis=-1)
    return jnp.einsum('bhqs,bshd->bqhd', p, v.astype(jnp.float32))


def stick_breaking_attn(q, k, v, q_pos, k_pos):
    z = jnp.einsum('bqhd,bshd->bhqs', q, k).astype(jnp.float32) * (q.shape[-1] ** -0.5)
    mask = k_pos[None, :] < q_pos[:, None]
    log_1mb = jnp.where(mask, jax.nn.log_sigmoid(-z), 0.0)
    suffix = lax.cumsum(log_1mb, axis=3, reverse=True) - log_1mb
    a = jnp.where(mask, jnp.exp(jax.nn.log_sigmoid(z) + suffix), 0.0)
    return jnp.einsum('bhqs,bshd->bqhd', a, v.astype(jnp.float32))


def rwkv7_branch(p_shift, past_shift, S0, P, l):
    B, T, _ = p_shift.shape
    H, N = RWKV_HEADS, RWKV_HEAD_DIM
    prev = jnp.concatenate([past_shift.astype(p_shift.dtype), p_shift[:, :-1]], axis=1)
    xs = p_shift + (prev - p_shift) * P['rwkv_mu'][l]
    r, k, v, wd, ad = _split(xs, RWKV_SIZES)
    w_log = -jax.nn.softplus(-(P['rwkv_w0'][l] + jnp.tanh(wd) @ P['rwkv_w_up'][l])) - 0.5
    a = jax.nn.sigmoid(P['rwkv_a0'][l] + ad @ P['rwkv_a_up'][l])
    hd = lambda t: t.astype(jnp.float32).reshape(B, T, H, N)
    r, k, v, a = hd(r), hd(k), hd(v), hd(a)
    decay = jnp.exp(-jnp.exp(hd(w_log)))
    kk = k * P['rwkv_k_k'][l].reshape(H, N)
    kk = kk / jnp.maximum(jnp.sqrt(jnp.sum(kk * kk, axis=-1, keepdims=True)), 1e-12)
    k = k * (1.0 + (a - 1.0) * P['rwkv_k_a'][l].reshape(H, N))

    def step(S, inp):
        r_t, w_t, k_t, v_t, kk_t, a_t = inp
        sa = jnp.einsum('bhij,bhj->bhi', S, -kk_t)
        S = (S * w_t[:, :, None, :] + sa[..., :, None] * (kk_t * a_t)[..., None, :]
             + v_t[..., :, None] * k_t[..., None, :])
        return S, jnp.einsum('bhij,bhj->bhi', S, r_t)

    S_fin, ys = lax.scan(step, S0.astype(jnp.float32),
                         tuple(jnp.moveaxis(t, 1, 0) for t in (r, decay, k, v, kk, a)))
    y = jnp.moveaxis(ys, 0, 1)
    mu = jnp.mean(y, axis=-1, keepdims=True)
    var = jnp.mean(jnp.square(y - mu), axis=-1, keepdims=True)
    y = ((y - mu) * lax.rsqrt(var + GN_EPS)).reshape(B, T, RWKV_WIDTH) * P['rwkv_gn_g'][l] + P['rwkv_gn_b'][l]
    bonus = jnp.sum(r * k * P['rwkv_r_k'][l].reshape(H, N), axis=-1, keepdims=True) * v
    return y + bonus.reshape(B, T, RWKV_WIDTH), S_fin, p_shift[:, -1:]


def mla_branch(cq_raw, ckv_raw, kr_raw, past_ckv, past_kr, q_pos, k_pos, P, l):
    B, T, _ = cq_raw.shape
    q = (rms_norm(cq_raw, P['mla_q_norm'][l]) @ P['mla_w_uq'][l]).reshape(B, T, MLA_HEADS, MLA_NOPE + MLA_ROPE)
    q_nope = rms_norm(q[..., :MLA_NOPE], P['mla_qn_nope'][l])
    q_rope = rope(rms_norm(q[..., MLA_NOPE:], P['mla_qn_rope'][l]), q_pos[:, None])
    c_kv = rms_norm(ckv_raw, P['mla_kv_norm'][l])
    k_rope = rope(rms_norm(kr_raw, P['mla_kn_rope'][l]), q_pos)
    ckv_all = jnp.concatenate([past_ckv, c_kv], axis=1)
    kr_all = jnp.concatenate([past_kr, k_rope], axis=1)
    S = ckv_all.shape[1]
    kv = (ckv_all @ P['mla_w_ukv'][l]).reshape(B, S, MLA_HEADS, MLA_NOPE + MLA_V)
    k_nope = rms_norm(kv[..., :MLA_NOPE], P['mla_kn_nope'][l])
    v = kv[..., MLA_NOPE:]
    k = jnp.concatenate([k_nope, jnp.broadcast_to(kr_all[:, :, None, :].astype(jnp.float32),
                                                  (B, S, MLA_HEADS, MLA_ROPE))], axis=-1)
    qf = jnp.concatenate([q_nope, q_rope], axis=-1)
    o = sweep_queries(lambda qb, pb: chunk_softmax_attn(qb, k, v, pb, k_pos), qf, q_pos)
    return o.reshape(B, T, MLA_WIDTH), c_kv, k_rope


def sb_branch(q_raw, k_raw, v_raw, past_k, past_v, q_pos, k_pos):
    B, T, _ = q_raw.shape
    q = q_raw.reshape(B, T, SB_HEADS, SB_HEAD_DIM)
    k_new = k_raw.reshape(B, T, SB_HEADS, SB_HEAD_DIM)
    v_new = v_raw.reshape(B, T, SB_HEADS, SB_HEAD_DIM)
    k_all = jnp.concatenate([past_k, k_new], axis=1)
    v_all = jnp.concatenate([past_v, v_new], axis=1)
    o = sweep_queries(lambda qb, pb: stick_breaking_attn(qb, k_all, v_all, pb, k_pos), q, q_pos)
    return o.reshape(B, T, SB_WIDTH), k_new, v_new


def layer(x, c, q_pos, past, P, l):
    S0, shift0, ckv0, kr0, k0, v0 = past
    past_len = ckv0.shape[1]
    k_pos = jnp.concatenate([jnp.arange(past_len, dtype=jnp.int32), q_pos])
    mod = jax.nn.silu(c.astype(jnp.float32)) @ P['w_ada'][l] + P['b_ada'][l]
    shift, scale, gate = jnp.split(mod[:, None, :], 3, axis=-1)
    h = rms_norm(x, P['norm_g'][l]) * (1.0 + scale) + shift
    proj = h @ P['w_in'][l]
    pr, zA, cq, ckv, kr, zB, sq, sk, sv, zC, gates = _split(proj, IN_SIZES)
    yA, S1, shift1 = rwkv7_branch(pr, shift0, S0, P, l)
    yB, ckv1, kr1 = mla_branch(cq, ckv, kr, ckv0, kr0, q_pos, k_pos, P, l)
    yC, k1, v1 = sb_branch(sq, sk, sv, k0, v0, q_pos, k_pos)
    gA, gB, gC = jnp.split(jax.nn.sigmoid(gates), 3, axis=-1)
    merged = (gA * ((yA * jax.nn.silu(zA)) @ P['w_br_rwkv'][l])
              + gB * ((yB * jax.nn.silu(zB)) @ P['w_br_mla'][l])
              + gC * ((yC * jax.nn.silu(zC)) @ P['w_br_sb'][l]))
    x = x + gate * (merged @ P['w_out'][l])
    return x, (S1, shift1, ckv1, kr1, k1, v1)


def trunk(x, c, q_pos, pasts, P):
    new = []
    for l in range(DEPTH):
        x, st = layer(x, c, q_pos, pasts[l], P, l)
        new.append(st)
    stacked = [jnp.stack([st[i] for st in new], axis=0) for i in range(6)]
    return x, stacked


def setup_inputs(seed: int = 0) -> dict:
    key = jax.random.key(seed)
    ks = iter(jax.random.split(key, 48))
    f32 = jnp.float32

    def nrm(shape, scale=1.0):
        return jax.random.normal(next(ks), shape, f32) * scale

    def gain(shape):
        return 1.0 + nrm(shape, 0.05)

    D, L = D_MODEL, DEPTH
    return {
        'x_prompt': nrm((BATCH, SEQ, D)),
        'x_sample': nrm((DEC_BATCH, DEC_SEQ, D)),
        'state_rwkv_wkv': nrm((L, DEC_BATCH, RWKV_HEADS, RWKV_HEAD_DIM, RWKV_HEAD_DIM), 0.5),
        'state_rwkv_shift': nrm((L, DEC_BATCH, 1, RWKV_SHIFT_WIDTH)),
        'cache_mla_ckv': nrm((L, DEC_BATCH, PAST_LEN, MLA_KV_LORA)),
        'cache_mla_krope': nrm((L, DEC_BATCH, PAST_LEN, MLA_ROPE)),
        'cache_sb_k': nrm((L, DEC_BATCH, PAST_LEN, SB_HEADS, SB_HEAD_DIM)),
        'cache_sb_v': nrm((L, DEC_BATCH, PAST_LEN, SB_HEADS, SB_HEAD_DIM)),
        'c_prompt': nrm((BATCH, D)),
        'c_sample': nrm((DEC_BATCH, D)),
        'w_ada': nrm((L, D, 3 * D), 0.5 * D ** -0.5),
        'b_ada': nrm((L, 3 * D), 0.02),
        'norm_g': gain((L, D)),
        'w_in': nrm((L, D, D_IN), D ** -0.5),
        'rwkv_mu': jax.random.uniform(next(ks), (L, RWKV_SHIFT_WIDTH), f32),
        'rwkv_w0': nrm((L, RWKV_WIDTH), 0.5),
        'rwkv_w_up': nrm((L, DECAY_LORA, RWKV_WIDTH), 0.5 * DECAY_LORA ** -0.5),
        'rwkv_a0': nrm((L, RWKV_WIDTH), 0.5),
        'rwkv_a_up': nrm((L, AAA_LORA, RWKV_WIDTH), 0.5 * AAA_LORA ** -0.5),
        'rwkv_k_k': 0.85 + nrm((L, RWKV_WIDTH), 0.05),
        'rwkv_k_a': gain((L, RWKV_WIDTH)),
        'rwkv_r_k': nrm((L, RWKV_WIDTH), 0.1),
        'rwkv_gn_g': gain((L, RWKV_WIDTH)),
        'rwkv_gn_b': nrm((L, RWKV_WIDTH), 0.02),
        'w_br_rwkv': nrm((L, RWKV_WIDTH, D), RWKV_WIDTH ** -0.5),
        'mla_q_norm': gain((L, MLA_Q_LORA)),
        'mla_w_uq': nrm((L, MLA_Q_LORA, MLA_HEADS * (MLA_NOPE + MLA_ROPE)), MLA_Q_LORA ** -0.5),
        'mla_kv_norm': gain((L, MLA_KV_LORA)),
        'mla_w_ukv': nrm((L, MLA_KV_LORA, MLA_HEADS * (MLA_NOPE + MLA_V)), MLA_KV_LORA ** -0.5),
        'mla_qn_nope': gain((L, MLA_NOPE)),
        'mla_qn_rope': gain((L, MLA_ROPE)),
        'mla_kn_nope': gain((L, MLA_NOPE)),
        'mla_kn_rope': gain((L, MLA_ROPE)),
        'w_br_mla': nrm((L, MLA_WIDTH, D), MLA_WIDTH ** -0.5),
        'w_br_sb': nrm((L, SB_WIDTH, D), SB_WIDTH ** -0.5),
        'w_out': nrm((L, D, D), D ** -0.5),
    }


def reference(x_prompt, x_sample, state_rwkv_wkv, state_rwkv_shift, cache_mla_ckv, cache_mla_krope,
              cache_sb_k, cache_sb_v, c_prompt, c_sample, w_ada, b_ada, norm_g, w_in, rwkv_mu, rwkv_w0,
              rwkv_w_up, rwkv_a0, rwkv_a_up, rwkv_k_k, rwkv_k_a, rwkv_r_k, rwkv_gn_g, rwkv_gn_b, w_br_rwkv,
              mla_q_norm, mla_w_uq, mla_kv_norm, mla_w_ukv, mla_qn_nope, mla_qn_rope, mla_kn_nope,
              mla_kn_rope, w_br_mla, w_br_sb, w_out):
    P = dict(w_ada=w_ada, b_ada=b_ada, norm_g=norm_g, w_in=w_in, rwkv_mu=rwkv_mu, rwkv_w0=rwkv_w0,
             rwkv_w_up=rwkv_w_up, rwkv_a0=rwkv_a0, rwkv_a_up=rwkv_a_up, rwkv_k_k=rwkv_k_k,
             rwkv_k_a=rwkv_k_a, rwkv_r_k=rwkv_r_k, rwkv_gn_g=rwkv_gn_g, rwkv_gn_b=rwkv_gn_b,
             w_br_rwkv=w_br_rwkv, mla_q_norm=mla_q_norm, mla_w_uq=mla_w_uq, mla_kv_norm=mla_kv_norm,
             mla_w_ukv=mla_w_ukv, mla_qn_nope=mla_qn_nope, mla_qn_rope=mla_qn_rope,
             mla_kn_nope=mla_kn_nope, mla_kn_rope=mla_kn_rope, w_br_mla=w_br_mla, w_br_sb=w_br_sb,
             w_out=w_out)
    Bp, Tp = x_prompt.shape[0], x_prompt.shape[1]
    Bs, Ts = x_sample.shape[0], x_sample.shape[1]
    past_len = cache_mla_ckv.shape[2]
    dt = x_prompt.dtype

    empty = (jnp.zeros((Bp, RWKV_HEADS, RWKV_HEAD_DIM, RWKV_HEAD_DIM), jnp.float32),
             jnp.zeros((Bp, 1, RWKV_SHIFT_WIDTH), dt),
             jnp.zeros((Bp, 0, MLA_KV_LORA), dt),
             jnp.zeros((Bp, 0, MLA_ROPE), dt),
             jnp.zeros((Bp, 0, SB_HEADS, SB_HEAD_DIM), dt),
             jnp.zeros((Bp, 0, SB_HEADS, SB_HEAD_DIM), dt))
    pos_p = jnp.arange(Tp, dtype=jnp.int32)
    y_prompt, st_p = trunk(x_prompt, c_prompt, pos_p, [empty] * DEPTH, P)
    wkv_p, shift_p, ckv_p, krope_p, sbk_p, sbv_p = st_p

    pasts_s = [(state_rwkv_wkv[l], state_rwkv_shift[l], cache_mla_ckv[l], cache_mla_krope[l],
                cache_sb_k[l], cache_sb_v[l]) for l in range(DEPTH)]
    pos_s = past_len + jnp.arange(Ts, dtype=jnp.int32)
    y_sample, st_s = trunk(x_sample, c_sample, pos_s, pasts_s, P)
    wkv_s, shift_s, ckv_s, krope_s, sbk_s, sbv_s = st_s

    return (y_prompt, y_sample, wkv_p, shift_p, ckv_p, krope_p, sbk_p, sbv_p,
            wkv_s, shift_s, ckv_s, krope_s, sbk_s, sbv_s)
```

```python
import functools

import numpy as np
import jax
import jax.numpy as jnp
from jax import lax
from jax.experimental import pallas as pl
from jax.experimental.pallas import tpu as pltpu

F32 = jnp.float32
BF16 = jnp.bfloat16

D_MODEL = 1024
CHUNK = 64
RMS_EPS = 1e-6
GN_EPS = 64e-5
NEG_BIG = -1e30
ROPE_THETA = 10000.0

HEADS = 8
HEAD_DIM = 64
WIDTH = HEADS * HEAD_DIM
PAIRS = HEADS // 2
LANES = 128
LORA = 64
SHIFT_W = 3 * WIDTH + 2 * LORA
Q_LORA = 384
KV_LORA = 256
ROPE = 32
NOPE = 64
QK = NOPE + ROPE
RWKV_CHUNK = 16

IN_GROUPS = (("pr", SHIFT_W), ("zA", WIDTH), ("cq", Q_LORA), ("ckv", KV_LORA), ("kr", ROPE),
             ("zB", WIDTH), ("sq", WIDTH), ("sk", WIDTH), ("sv", WIDTH), ("zC", WIDTH),
             ("gates", 3 * D_MODEL))

VMEM_LIMIT = 56 * 1024 * 1024


def _params(sem):
    return pltpu.CompilerParams(dimension_semantics=sem, vmem_limit_bytes=VMEM_LIMIT)


def _bdot(a, b):
    return jnp.dot(a.astype(BF16), b.astype(BF16), preferred_element_type=F32)


def _bdot_nt(a, b):
    return lax.dot_general(a.astype(BF16), b.astype(BF16), (((1,), (1,)), ((), ())),
                           preferred_element_type=F32)


def _bdot_tn(a, b):
    return lax.dot_general(a.astype(BF16), b.astype(BF16), (((0,), (0,)), ((), ())),
                           preferred_element_type=F32)


def _split(x, n):
    parts, r = [], x
    for _ in range(n):
        p = r.astype(BF16)
        parts.append(p)
        r = r - p.astype(F32)
    return parts


def _xdot_r(x, m01, n):
    m = m01.astype(BF16)
    return sum(jnp.dot(p, m, preferred_element_type=F32) for p in _split(x, n))


def _xdot_l(m01, x, n):
    m = m01.astype(BF16)
    return sum(jnp.dot(m, p, preferred_element_type=F32) for p in _split(x, n))


def _sigmoid(x):
    return 1.0 / (1.0 + jnp.exp(-x))


def _silu(x):
    return x * _sigmoid(x)


def _softplus(x):
    return jnp.maximum(x, 0.0) + jnp.log1p(jnp.exp(-jnp.abs(x)))


def _rms(x, g):
    return x * lax.rsqrt(jnp.mean(x * x, axis=-1, keepdims=True) + RMS_EPS) * g


def _iota2(shape, dim):
    return lax.broadcasted_iota(jnp.int32, shape, dim)


def _ada_kernel(c_ref, w_ref, b_ref, o_ref):
    o_ref[0] = _bdot(_silu(c_ref[...]), w_ref[0]) + b_ref[0]


def ada_mod(c, w_ada, b_ada):
    L, D, N = w_ada.shape
    rows = c.shape[0]
    tn = 1024
    return pl.pallas_call(
        _ada_kernel,
        grid=(L, N // tn),
        in_specs=[pl.BlockSpec((rows, D), lambda l, j: (0, 0)),
                  pl.BlockSpec((1, D, tn), lambda l, j: (l, 0, j)),
                  pl.BlockSpec((1, 1, tn), lambda l, j: (l, 0, j))],
        out_specs=pl.BlockSpec((1, rows, tn), lambda l, j: (l, 0, j)),
        out_shape=jax.ShapeDtypeStruct((L, rows, N), F32),
        compiler_params=_params(("parallel", "parallel")),
        name="ada_mod",
    )(c, w_ada, b_ada.reshape(L, 1, N))


def _inproj_kernel(x_ref, sc_ref, sh_ref, g_ref, *refs):
    n = len(IN_GROUPS)
    ws, outs = refs[:n], refs[n:]
    h = _rms(x_ref[...], g_ref[...]) * (1.0 + sc_ref[0]) + sh_ref[0]
    hb = h.astype(BF16)
    for w, o in zip(ws, outs):
        o[...] = jnp.dot(hb, w[...], preferred_element_type=F32)


def in_proj(x2, scale, shift, g, ws, T, tm):
    M, D = x2.shape
    assert T % tm == 0 and M % tm == 0
    per = T // tm
    row = lambda i: (i, 0)
    mod = lambda i: (i // per, 0, 0)
    const = lambda i: (0, 0)
    in_specs = [pl.BlockSpec((tm, D), row), pl.BlockSpec((1, 1, D), mod), pl.BlockSpec((1, 1, D), mod),
                pl.BlockSpec((1, D), const)]
    in_specs += [pl.BlockSpec((D, wd), const, pipeline_mode=pl.Buffered(1)) for _, wd in IN_GROUPS]
    out_specs = [pl.BlockSpec((tm, wd), row) for _, wd in IN_GROUPS]
    out_shape = [jax.ShapeDtypeStruct((M, wd), F32) for _, wd in IN_GROUPS]
    return pl.pallas_call(
        _inproj_kernel, grid=(M // tm,), in_specs=in_specs, out_specs=out_specs, out_shape=out_shape,
        compiler_params=_params(("parallel",)), name="in_proj",
    )(x2, scale, shift, g, *ws)


def _rwkv_kernel(n, pr_ref, past_ref, s0_ref, mu_ref, w0_ref, a0_ref, kk_ref, ka_ref, rk_ref,
                 gng_ref, gnb_ref, wup_ref, aup_ref, y_ref, sfin_ref, st_scr, last_scr):
    tb = pl.program_id(1)

    @pl.when(tb == 0)
    def _():
        st_scr[...] = s0_ref[0]
        last_scr[...] = past_ref[0]

    p = pr_ref[0]
    row = _iota2((n, 1), 0)
    prev = jnp.where(row == 0, last_scr[...], pltpu.roll(p, 1, axis=0))
    last_scr[...] = p[n - 1:n, :]
    xs = p + (prev - p) * mu_ref[...]
    r, k, v = xs[:, :WIDTH], xs[:, WIDTH:2 * WIDTH], xs[:, 2 * WIDTH:3 * WIDTH]
    wa = xs[:, 3 * WIDTH:]
    w_log = -_softplus(-(w0_ref[...] + _bdot(jnp.tanh(wa), wup_ref[...]))) - 0.5
    lw = -jnp.exp(w_log)
    a = _sigmoid(a0_ref[...] + _bdot(wa, aup_ref[...]))

    ri, ci = _iota2((LANES, LANES), 0), _iota2((LANES, LANES), 1)
    seg64 = ((ri >> 6) == (ci >> 6)).astype(BF16)
    bd2 = ((ri >> 6) == (ci >> 6)).astype(F32)

    def segsum(t):
        return jnp.concatenate(
            [_xdot_r(t[:, LANES * q:LANES * (q + 1)], seg64, 2) for q in range(PAIRS)], axis=-1)

    kkr = k * kk_ref[...]
    kk = kkr / jnp.maximum(jnp.sqrt(segsum(kkr * kkr)), 1e-12)
    k2 = k * (1.0 + (a - 1.0) * ka_ref[...])
    bonus = segsum(r * k2 * rk_ref[...]) * v

    tr, tc = _iota2((n, n), 0), _iota2((n, n), 1)
    same = (tr >> 4) == (tc >> 4)
    strict = (same & (tr > tc)).astype(F32)
    incl = (same & (tr >= tc)).astype(F32)
    eye = (tr == tc).astype(F32)
    g_log = _xdot_l(incl, lw, 3)
    g_end = _xdot_l(same.astype(F32), lw, 3)
    e_in, e_inv, e_rest = jnp.exp(g_log), jnp.exp(-g_log), jnp.exp(g_end - g_log)
    r_t = r * e_in
    kk_t = kk * jnp.exp(g_log - lw)
    kka = kk * a
    k_t, b_t = k2 * e_inv, kka * e_inv
    k_e, b_e = k2 * e_rest, kka * e_rest
    dec = jnp.exp(g_end)

    lane = _iota2((1, LANES), 1)
    head_mask = [(lane < HEAD_DIM).astype(F32), (lane >= HEAD_DIM).astype(F32)]

    for q in range(PAIRS):
        sl = slice(LANES * q, LANES * (q + 1))
        rp, kkp, vp = r_t[:, sl], kk_t[:, sl], v[:, sl]
        kb, bb, vb, kkb = k_t[:, sl].astype(BF16), b_t[:, sl].astype(BF16), vp.astype(BF16), kkp.astype(BF16)
        per_head = []
        for hm in head_mask:
            kkm, rm = (kkp * hm).astype(BF16), (rp * hm).astype(BF16)
            l_b = _bdot_nt(kkm, bb) * strict
            l_k = _bdot_nt(kkm, kb) * strict
            r_k = _bdot_nt(rm, kb) * incl
            r_b = _bdot_nt(rm, bb) * incl
            l2 = _bdot(l_b, l_b)
            l4 = _bdot(l2, l2)
            l8 = _bdot(l4, l4)
            t_inv = eye - l_b
            t_inv = t_inv + _bdot(t_inv, l2)
            t_inv = t_inv + _bdot(t_inv, l4)
            t_inv = t_inv + _bdot(t_inv, l8)
            w_h = _bdot(t_inv, kkb)
            u0_h = _bdot(t_inv, _bdot(l_k, vb))
            q_h = rp - _bdot(r_b, w_h)
            y0_h = _bdot(r_k, vb) - _bdot(r_b, u0_h)
            per_head.append((w_h, u0_h, q_h, y0_h))
        first = head_mask[0] > 0.5
        w_c, u0_c, q_c, y0_c = (jnp.where(first, x0, x1) for x0, x1 in zip(*per_head))

        st = st_scr[q]
        ys = []
        for c in range(n // RWKV_CHUNK):
            rs = slice(RWKV_CHUNK * c, RWKV_CHUNK * (c + 1))
            be = b_e[rs, sl]
            n_c = _bdot_tn(vp[rs], k_e[rs, sl]) - _bdot_tn(u0_c[rs], be)
            ys.append(_bdot_nt(q_c[rs], st) + y0_c[rs])
            z = _bdot_nt(st, w_c[rs])
            st = st * dec[RWKV_CHUNK * c:RWKV_CHUNK * c + 1, sl] + (n_c - _bdot(z, be)) * bd2
        st_scr[q] = st
        y = ys[0] if len(ys) == 1 else jnp.concatenate(ys, axis=0)
        mean = _xdot_r(y, seg64, 2) * (1.0 / HEAD_DIM)
        d = y - mean
        var = _xdot_r(d * d, seg64, 2) * (1.0 / HEAD_DIM)
        y_ref[0, :, sl] = d * lax.rsqrt(var + GN_EPS) * gng_ref[:, sl] + gnb_ref[:, sl] + bonus[:, sl]

    @pl.when(tb == pl.num_programs(1) - 1)
    def _():
        sfin_ref[0] = st_scr[...]


def rwkv_branch(pr, past_shift, s0_bd, wl, n):
    B, T, _ = pr.shape
    assert T % n == 0 and n % RWKV_CHUNK == 0
    vec = lambda wd: pl.BlockSpec((1, wd), lambda b, t: (0, 0))
    in_specs = [pl.BlockSpec((1, n, SHIFT_W), lambda b, t: (b, t, 0)),
                pl.BlockSpec((1, 1, SHIFT_W), lambda b, t: (b, 0, 0)),
                pl.BlockSpec((1, PAIRS, LANES, LANES), lambda b, t: (b, 0, 0, 0)),
                vec(SHIFT_W)] + [vec(WIDTH)] * 7 + [
                pl.BlockSpec((LANES, WIDTH), lambda b, t: (0, 0)),
                pl.BlockSpec((LANES, WIDTH), lambda b, t: (0, 0))]
    return pl.pallas_call(
        functools.partial(_rwkv_kernel, n),
        grid=(B, T // n),
        in_specs=in_specs,
        out_specs=[pl.BlockSpec((1, n, WIDTH), lambda b, t: (b, t, 0)),
                   pl.BlockSpec((1, PAIRS, LANES, LANES), lambda b, t: (b, 0, 0, 0))],
        out_shape=[jax.ShapeDtypeStruct((B, T, WIDTH), F32),
                   jax.ShapeDtypeStruct((B, PAIRS, LANES, LANES), F32)],
        scratch_shapes=[pltpu.VMEM((PAIRS, LANES, LANES), F32), pltpu.VMEM((1, SHIFT_W), F32)],
        compiler_params=_params(("parallel", "arbitrary")),
        name="rwkv7",
    )(pr, past_shift, s0_bd, wl["mu"], wl["w0"], wl["a0"], wl["k_k"], wl["k_a"], wl["r_k"],
      wl["gn_g"], wl["gn_b"], wl["w_up"], wl["a_up"])


def _state_to_bd(s):
    B = s.shape[0]
    s = s.reshape(B, PAIRS, 2, HEAD_DIM, HEAD_DIM)
    z = jnp.zeros_like(s[:, :, 0])
    top = jnp.concatenate([s[:, :, 0], z], axis=-1)
    bot = jnp.concatenate([z, s[:, :, 1]], axis=-1)
    return jnp.concatenate([top, bot], axis=-2)


def _bd_to_state(s):
    B = s.shape[0]
    h0 = s[:, :, :HEAD_DIM, :HEAD_DIM]
    h1 = s[:, :, HEAD_DIM:, HEAD_DIM:]
    return jnp.stack([h0, h1], axis=2).reshape(B, HEADS, HEAD_DIM, HEAD_DIM)


def _mla_q_kernel(cq_ref, ckv_ref, kr_ref, c96_ref, s96_ref, c32_ref, s32_ref, gq_ref, wuq_ref, gqn_ref,
                  gkv_ref, gkr_ref, p96_ref, p32_ref, q_ref, ckvn_ref, krope_ref):
    cqb = _rms(cq_ref[0], gq_ref[...]).astype(BF16)
    nope = _iota2((1, QK), 1) < NOPE
    c96, s96 = c96_ref[...], s96_ref[...]
    for h in range(HEADS):
        qh = jnp.dot(cqb, wuq_ref[h], preferred_element_type=F32)
        sq = qh * qh
        ss_n = jnp.sum(jnp.where(nope, sq, 0.0), axis=-1, keepdims=True)
        ss_r = jnp.sum(jnp.where(nope, 0.0, sq), axis=-1, keepdims=True)
        inv = jnp.where(nope, lax.rsqrt(ss_n * (1.0 / NOPE) + RMS_EPS), lax.rsqrt(ss_r * (1.0 / ROPE) + RMS_EPS))
        qn = qh * inv * gqn_ref[...]
        qo = qn * c96 + _xdot_r(qn, p96_ref[...], 3) * s96
        q_ref[0, h] = (qo * (QK ** -0.5)).astype(BF16)
    ckvn_ref[0] = _rms(ckv_ref[0], gkv_ref[...])
    krn = _rms(kr_ref[0], gkr_ref[...])
    krope_ref[0] = krn * c32_ref[...] + _xdot_r(krn, p32_ref[...], 3) * s32_ref[...]


def mla_q(cq, ckv, kr, tabs, wl, tm):
    B, T, _ = cq.shape
    c96, s96, c32, s32 = tabs
    tok = lambda wd: pl.BlockSpec((1, tm, wd), lambda b, t: (b, t, 0))
    tab = lambda wd: pl.BlockSpec((tm, wd), lambda b, t: (t, 0))
    vec = lambda wd: pl.BlockSpec((1, wd), lambda b, t: (0, 0))
    return pl.pallas_call(
        _mla_q_kernel,
        grid=(B, T // tm),
        in_specs=[tok(Q_LORA), tok(KV_LORA), tok(ROPE), tab(QK), tab(QK), tab(ROPE), tab(ROPE),
                  vec(Q_LORA), pl.BlockSpec((HEADS, Q_LORA, QK), lambda b, t: (0, 0, 0)), vec(QK),
                  vec(KV_LORA), vec(ROPE),
                  pl.BlockSpec((QK, QK), lambda b, t: (0, 0)), pl.BlockSpec((ROPE, ROPE), lambda b, t: (0, 0))],
        out_specs=[pl.BlockSpec((1, HEADS, tm, QK), lambda b, t: (b, 0, t, 0)), tok(KV_LORA), tok(ROPE)],
        out_shape=[jax.ShapeDtypeStruct((B, HEADS, T, QK), BF16),
                   jax.ShapeDtypeStruct((B, T, KV_LORA), F32),
                   jax.ShapeDtypeStruct((B, T, ROPE), F32)],
        compiler_params=_params(("parallel", "parallel")),
        name="mla_q",
    )(cq, ckv, kr, c96, s96, c32, s32, wl["q_norm"], wl["w_uq"], wl["qn"], wl["kv_norm"], wl["kn_rope"],
      wl["p96"], wl["p32"])


def _mla_kv_kernel(ckv_ref, kr_ref, wuk_ref, wuv_ref, gkn_ref, e_ref, k_ref, v_ref):
    cb = ckv_ref[0].astype(BF16)
    k_rope = _xdot_r(kr_ref[0], e_ref[...], 3)
    for h in range(HEADS):
        kh = jnp.dot(cb, wuk_ref[h], preferred_element_type=F32)
        ms = jnp.sum(kh * kh, axis=-1, keepdims=True) * (1.0 / NOPE)
        k_ref[0, h] = (kh * lax.rsqrt(ms + RMS_EPS) * gkn_ref[...] + k_rope).astype(BF16)
    for q in range(PAIRS):
        v_ref[0, q] = jnp.dot(cb, wuv_ref[q], preferred_element_type=F32).astype(BF16)


def mla_kv(ckv_all, kr_all, wl, ts):
    B, S, _ = ckv_all.shape
    return pl.pallas_call(
        _mla_kv_kernel,
        grid=(B, S // ts),
        in_specs=[pl.BlockSpec((1, ts, KV_LORA), lambda b, t: (b, t, 0)),
                  pl.BlockSpec((1, ts, ROPE), lambda b, t: (b, t, 0)),
                  pl.BlockSpec((HEADS, KV_LORA, QK), lambda b, t: (0, 0, 0)),
                  pl.BlockSpec((PAIRS, KV_LORA, LANES), lambda b, t: (0, 0, 0)),
                  pl.BlockSpec((1, QK), lambda b, t: (0, 0)),
                  pl.BlockSpec((ROPE, QK), lambda b, t: (0, 0))],
        out_specs=[pl.BlockSpec((1, HEADS, ts, QK), lambda b, t: (b, 0, t, 0)),
                   pl.BlockSpec((1, PAIRS, ts, LANES), lambda b, t: (b, 0, t, 0))],
        out_shape=[jax.ShapeDtypeStruct((B, HEADS, S, QK), BF16),
                   jax.ShapeDtypeStruct((B, PAIRS, S, LANES), BF16)],
        compiler_params=_params(("parallel", "parallel")),
        name="mla_kv",
    )(ckv_all, kr_all, wl["w_uk"], wl["w_uv"], wl["kn_nope"], wl["e96"])


def _mla_attn_kernel(tq, tk, past, s_len, q_ref, k_ref, v_ref, o_ref, m_scr, l_scr, acc_scr):
    qi, kj = pl.program_id(2), pl.program_id(3)

    @pl.when(kj == 0)
    def _():
        m_scr[...] = jnp.full(m_scr.shape, NEG_BIG, F32)
        l_scr[...] = jnp.zeros(l_scr.shape, F32)
        acc_scr[...] = jnp.zeros(acc_scr.shape, F32)

    @pl.when((kj * tk) // CHUNK <= (past + qi * tq + tq - 1) // CHUNK)
    def _():
        qpos = past + qi * tq + _iota2((tq, 1), 0)
        kidx = kj * tk + _iota2((1, tk), 1)
        mask = ((kidx >> 6) <= (qpos >> 6)) & (kidx < s_len)
        first = _iota2((1, LANES), 1) < HEAD_DIM
        pv, alphas = [], []
        for h in range(2):
            s = _bdot_nt(q_ref[0, h], k_ref[0, h])
            s = jnp.where(mask, s, NEG_BIG)
            m_prev = m_scr[h]
            m_new = jnp.maximum(m_prev, jnp.max(s, axis=-1, keepdims=True))
            alpha = jnp.exp(m_prev - m_new)
            p = jnp.where(mask, jnp.exp(s - m_new), 0.0)
            l_scr[h] = alpha * l_scr[h] + jnp.sum(p, axis=-1, keepdims=True)
            m_scr[h] = m_new
            pv.append(_bdot(p, v_ref[0, 0]))
            alphas.append(alpha)
        acc_scr[...] = acc_scr[...] * jnp.where(first, alphas[0], alphas[1]) + jnp.where(first, pv[0], pv[1])

    @pl.when(kj == pl.num_programs(3) - 1)
    def _():
        first = _iota2((1, LANES), 1) < HEAD_DIM
        o_ref[0] = acc_scr[...] / jnp.where(first, l_scr[0], l_scr[1])


def mla_attn(q, k, v, past, s_len, tq, tk):
    B, _, T, _ = q.shape
    S = k.shape[2]
    nk = S // tk

    def kblk(qi, kj):
        last = (((past + qi * tq + tq - 1) // CHUNK) * CHUNK + CHUNK - 1) // tk
        return jnp.minimum(kj, jnp.minimum(last, nk - 1))

    return pl.pallas_call(
        functools.partial(_mla_attn_kernel, tq, tk, past, s_len),
        grid=(B, PAIRS, T // tq, nk),
        in_specs=[pl.BlockSpec((1, 2, tq, QK), lambda b, p, i, j: (b, p, i, 0)),
                  pl.BlockSpec((1, 2, tk, QK), lambda b, p, i, j: (b, p, kblk(i, j), 0)),
                  pl.BlockSpec((1, 1, tk, LANES), lambda b, p, i, j: (b, p, kblk(i, j), 0))],
        out_specs=pl.BlockSpec((1, tq, LANES), lambda b, p, i, j: (b, i, p)),
        out_shape=jax.ShapeDtypeStruct((B, T, WIDTH), F32),
        scratch_shapes=[pltpu.VMEM((2, tq, 1), F32), pltpu.VMEM((2, tq, 1), F32), pltpu.VMEM((tq, LANES), F32)],
        compiler_params=_params(("parallel", "parallel", "parallel", "arbitrary")),
        name="mla_attn",
    )(q, k, v)


def _sb_last_block(past, qi, tq, tk, nk):
    return jnp.clip((past + qi * tq + tq - 2) // tk, 0, nk - 1)


def _sb_kernel(tq, tk, past, nk, q_ref, k_ref, v_ref, o_ref, carry_scr, acc_scr):
    qi, j = pl.program_id(2), pl.program_id(3)
    kb = _sb_last_block(past, qi, tq, tk, nk) - j

    @pl.when(j == 0)
    def _():
        carry_scr[...] = jnp.zeros(carry_scr.shape, F32)
        acc_scr[...] = jnp.zeros(acc_scr.shape, F32)

    @pl.when(kb >= 0)
    def _():
        qpos = past + qi * tq + _iota2((tq, 1), 0)
        kidx = kb * tk + _iota2((1, tk), 1)
        mask = kidx < qpos
        later = (_iota2((tk, tk), 0) > _iota2((tk, tk), 1)).astype(BF16)
        lane = _iota2((1, LANES), 1)
        first = lane < HEAD_DIM
        qv = q_ref[0] * (HEAD_DIM ** -0.5)
        kb16, vb16 = k_ref[0].astype(BF16), v_ref[0].astype(BF16)
        pv = []
        for h in range(2):
            qm = jnp.where(first if h == 0 else ~first, qv, 0.0)
            z = _bdot_nt(qm, kb16)
            ls = jnp.minimum(z, 0.0) - jnp.log1p(jnp.exp(-jnp.abs(z)))
            lb = jnp.where(mask, ls - z, 0.0)
            suffix = _xdot_r(lb, later, 2)
            c = carry_scr[h]
            a = jnp.where(mask, jnp.exp(ls + suffix + c), 0.0)
            pv.append(_bdot(a, vb16))
            carry_scr[h] = c + jnp.sum(lb, axis=-1, keepdims=True)
        acc_scr[...] += jnp.where(first, pv[0], pv[1])

    @pl.when(j == pl.num_programs(3) - 1)
    def _():
        o_ref[0] = acc_scr[...]


def sb_attn(q, k_all, v_all, past, tq, tk):
    B, T, _ = q.shape
    S = k_all.shape[1]
    nk = S // tk
    nj = int(min(nk, (past + T - 2) // tk + 1))
    kblk = lambda i, j: jnp.maximum(_sb_last_block(past, i, tq, tk, nk) - j, 0)
    return pl.pallas_call(
        functools.partial(_sb_kernel, tq, tk, past, nk),
        grid=(B, PAIRS, T // tq, nj),
        in_specs=[pl.BlockSpec((1, tq, LANES), lambda b, p, i, j: (b, i, p)),
                  pl.BlockSpec((1, tk, LANES), lambda b, p, i, j: (b, kblk(i, j), p)),
                  pl.BlockSpec((1, tk, LANES), lambda b, p, i, j: (b, kblk(i, j), p))],
        out_specs=pl.BlockSpec((1, tq, LANES), lambda b, p, i, j: (b, i, p)),
        out_shape=jax.ShapeDtypeStruct((B, T, WIDTH), F32),
        scratch_shapes=[pltpu.VMEM((2, tq, 1), F32), pltpu.VMEM((tq, LANES), F32)],
        compiler_params=_params(("parallel", "parallel", "parallel", "arbitrary")),
        name="sb_attn",
    )(q, k_all, v_all)


def _merge_kernel(x_ref, gm_ref, ya_ref, za_ref, yb_ref, zb_ref, yc_ref, zc_ref, g_ref,
                  wa_ref, wb_ref, wc_ref, wo_ref, o_ref):
    def branch(y_ref, z_ref, w_ref):
        return _bdot(y_ref[...] * _silu(z_ref[...]), w_ref[...])

    sg = _sigmoid(g_ref[...])
    merged = (sg[:, :D_MODEL] * branch(ya_ref, za_ref, wa_ref)
              + sg[:, D_MODEL:2 * D_MODEL] * branch(yb_ref, zb_ref, wb_ref)
              + sg[:, 2 * D_MODEL:] * branch(yc_ref, zc_ref, wc_ref))
    o_ref[...] = x_ref[...] + gm_ref[0] * _bdot(merged, wo_ref[...])


def merge_out(x2, gate_mod, ya, za, yb, zb, yc, zc, gates, wl, T, tm):
    M, D = x2.shape
    per = T // tm
    row = lambda wd: pl.BlockSpec((tm, wd), lambda i: (i, 0))
    const = lambda r, c: pl.BlockSpec((r, c), lambda i: (0, 0))
    return pl.pallas_call(
        _merge_kernel,
        grid=(M // tm,),
        in_specs=[row(D), pl.BlockSpec((1, 1, D), lambda i: (i // per, 0, 0))] + [row(WIDTH)] * 6 + [row(3 * D),
                  const(WIDTH, D), const(WIDTH, D), const(WIDTH, D), const(D, D)],
        out_specs=row(D),
        out_shape=jax.ShapeDtypeStruct((M, D), F32),
        compiler_params=_params(("parallel",)),
        name="merge_out",
    )(x2, gate_mod, ya, za, yb, zb, yc, zc, gates, wl["w_br_rwkv"], wl["w_br_mla"], wl["w_br_sb"], wl["w_out"])


def _rot_matrix(width, offset):
    half = ROPE // 2
    m = np.zeros((width, width), np.float32)
    for i in range(half):
        m[offset + half + i, offset + i] = -1.0
        m[offset + i, offset + half + i] = 1.0
    return jnp.asarray(m, BF16)


def _prep_layer(P, l):
    row = lambda a: a[l].reshape(1, -1)
    w_in = P["w_in"][l]
    offs = np.cumsum([0] + [wd for _, wd in IN_GROUPS])
    zeros_lora = jnp.zeros((LORA, WIDTH), F32)
    w_uq = P["mla_w_uq"][l].reshape(Q_LORA, HEADS, QK).transpose(1, 0, 2)
    w_ukv = P["mla_w_ukv"][l].reshape(KV_LORA, HEADS, 2 * HEAD_DIM)
    w_uk = jnp.pad(w_ukv[:, :, :NOPE].transpose(1, 0, 2), ((0, 0), (0, 0), (0, ROPE)))
    w_uv = w_ukv[:, :, NOPE:].reshape(KV_LORA, PAIRS, LANES).transpose(1, 0, 2)
    e96 = np.zeros((ROPE, QK), np.float32)
    e96[np.arange(ROPE), NOPE + np.arange(ROPE)] = 1.0
    return dict(
        norm_g=row(P["norm_g"]),
        w_in=[w_in[:, offs[i]:offs[i + 1]].astype(BF16) for i in range(len(IN_GROUPS))],
        mu=row(P["rwkv_mu"]), w0=row(P["rwkv_w0"]), a0=row(P["rwkv_a0"]), k_k=row(P["rwkv_k_k"]),
        k_a=row(P["rwkv_k_a"]), r_k=row(P["rwkv_r_k"]), gn_g=row(P["rwkv_gn_g"]), gn_b=row(P["rwkv_gn_b"]),
        w_up=jnp.concatenate([P["rwkv_w_up"][l], zeros_lora], axis=0).astype(BF16),
        a_up=jnp.concatenate([zeros_lora, P["rwkv_a_up"][l]], axis=0).astype(BF16),
        q_norm=row(P["mla_q_norm"]), w_uq=w_uq.astype(BF16),
        qn=jnp.concatenate([P["mla_qn_nope"][l], P["mla_qn_rope"][l]]).reshape(1, QK),
        kv_norm=row(P["mla_kv_norm"]), kn_rope=row(P["mla_kn_rope"]),
        kn_nope=jnp.pad(P["mla_kn_nope"][l], (0, ROPE)).reshape(1, QK),
        w_uk=w_uk.astype(BF16), w_uv=w_uv.astype(BF16),
        p96=_rot_matrix(QK, NOPE), p32=_rot_matrix(ROPE, 0), e96=jnp.asarray(e96, BF16),
        w_br_rwkv=P["w_br_rwkv"][l].astype(BF16), w_br_mla=P["w_br_mla"][l].astype(BF16),
        w_br_sb=P["w_br_sb"][l].astype(BF16), w_out=P["w_out"][l].astype(BF16),
    )


def _rope_tables(past, T):
    half = ROPE // 2
    inv = ROPE_THETA ** (-jnp.arange(half, dtype=F32) / half)
    ang = (past + jnp.arange(T, dtype=jnp.int32)).astype(F32)[:, None] * inv
    c32 = jnp.tile(jnp.cos(ang), (1, 2))
    s32 = jnp.tile(jnp.sin(ang), (1, 2))
    c96 = jnp.concatenate([jnp.ones((T, NOPE), F32), c32], axis=1)
    s96 = jnp.concatenate([jnp.zeros((T, NOPE), F32), s32], axis=1)
    return c96, s96, c32, s32


def _pad_rows(a, rows):
    return a if a.shape[1] == rows else jnp.pad(a, ((0, 0), (0, rows - a.shape[1])) + ((0, 0),) * (a.ndim - 2))


def _trunk(x, mods, pasts, layers, cfg):
    B, T, D = x.shape
    past = cfg["past"]
    S = past + T
    s_pad = -(-S // cfg["kv_mult"]) * cfg["kv_mult"]
    tabs = _rope_tables(past, T)
    x2 = x.reshape(B * T, D)
    new = []
    for l, wl in enumerate(layers):
        shift, scale, gate = (mods[l][:, None, i * D:(i + 1) * D] for i in range(3))
        outs = in_proj(x2, scale, shift, wl["norm_g"], wl["w_in"], T, cfg["tm"])
        pr, zA, cq, ckv, kr, zB, sq, sk, sv, zC, gates = outs
        seq = lambda a: a.reshape(B, T, a.shape[-1])
        if pasts is None:
            s0 = jnp.zeros((B, PAIRS, LANES, LANES), F32)
            shift0 = jnp.zeros((B, 1, SHIFT_W), F32)
        else:
            s0, shift0 = _state_to_bd(pasts[l][0]), pasts[l][1]
        pr3 = seq(pr)
        yA, s_fin = rwkv_branch(pr3, shift0, s0, wl, cfg["rwkv_n"])
        q, ckv_n, k_rope = mla_q(seq(cq), seq(ckv), seq(kr), tabs, wl, cfg["tm_q"])
        sk3, sv3 = seq(sk), seq(sv)
        if pasts is None:
            ckv_all, kr_all, k_all, v_all = ckv_n, k_rope, sk3, sv3
        else:
            _, _, ckv0, kr0, k0, v0 = pasts[l]
            ckv_all = jnp.concatenate([ckv0, ckv_n], axis=1)
            kr_all = jnp.concatenate([kr0, k_rope], axis=1)
            k_all = jnp.concatenate([k0.reshape(B, past, WIDTH), sk3], axis=1)
            v_all = jnp.concatenate([v0.reshape(B, past, WIDTH), sv3], axis=1)
        ckv_all, kr_all, k_all, v_all = (_pad_rows(a, s_pad) for a in (ckv_all, kr_all, k_all, v_all))
        kf, vf = mla_kv(ckv_all, kr_all, wl, cfg["ts"])
        yB = mla_attn(q, kf, vf, past, S, cfg["tq"], cfg["tk"])
        yC = sb_attn(seq(sq), k_all, v_all, past, cfg["sb_tq"], cfg["sb_tk"])
        x2 = merge_out(x2, gate, yA.reshape(B * T, WIDTH), zA, yB.reshape(B * T, WIDTH), zB,
                       yC.reshape(B * T, WIDTH), zC, gates, wl, T, cfg["tm"])
        new.append((_bd_to_state(s_fin), pr3[:, -1:], ckv_n, k_rope,
                    sk3.reshape(B, T, HEADS, HEAD_DIM), sv3.reshape(B, T, HEADS, HEAD_DIM)))
    stacked = [jnp.stack([st[i] for st in new], axis=0) for i in range(6)]
    return x2.reshape(B, T, D), stacked


PROMPT_CFG = dict(past=0, tm=256, rwkv_n=128, tm_q=256, ts=256, tq=512, tk=512, sb_tq=256, sb_tk=256, kv_mult=512)
SAMPLE_CFG = dict(past=2048, tm=16, rwkv_n=16, tm_q=16, ts=2176, tq=16, tk=2176, sb_tq=16, sb_tk=128, kv_mult=128)


def kernel(x_prompt, x_sample, state_rwkv_wkv, state_rwkv_shift, cache_mla_ckv, cache_mla_krope, cache_sb_k, cache_sb_v, c_prompt, c_sample, w_ada, b_ada, norm_g, w_in, rwkv_mu, rwkv_w0, rwkv_w_up, rwkv_a0, rwkv_a_up, rwkv_k_k, rwkv_k_a, rwkv_r_k, rwkv_gn_g, rwkv_gn_b, w_br_rwkv, mla_q_norm, mla_w_uq, mla_kv_norm, mla_w_ukv, mla_qn_nope, mla_qn_rope, mla_kn_nope, mla_kn_rope, w_br_mla, w_br_sb, w_out):
    P = dict(w_in=w_in, norm_g=norm_g, rwkv_mu=rwkv_mu, rwkv_w0=rwkv_w0, rwkv_w_up=rwkv_w_up, rwkv_a0=rwkv_a0,
             rwkv_a_up=rwkv_a_up, rwkv_k_k=rwkv_k_k, rwkv_k_a=rwkv_k_a, rwkv_r_k=rwkv_r_k, rwkv_gn_g=rwkv_gn_g,
             rwkv_gn_b=rwkv_gn_b, w_br_rwkv=w_br_rwkv, mla_q_norm=mla_q_norm, mla_w_uq=mla_w_uq,
             mla_kv_norm=mla_kv_norm, mla_w_ukv=mla_w_ukv, mla_qn_nope=mla_qn_nope, mla_qn_rope=mla_qn_rope,
             mla_kn_nope=mla_kn_nope, mla_kn_rope=mla_kn_rope, w_br_mla=w_br_mla, w_br_sb=w_br_sb, w_out=w_out)
    depth = w_in.shape[0]
    bp = x_prompt.shape[0]
    layers = [_prep_layer(P, l) for l in range(depth)]
    mods = ada_mod(jnp.concatenate([c_prompt, c_sample], axis=0), w_ada, b_ada)
    y_p, st_p = _trunk(x_prompt, mods[:, :bp], None, layers, PROMPT_CFG)
    pasts = [(state_rwkv_wkv[l], state_rwkv_shift[l], cache_mla_ckv[l], cache_mla_krope[l],
              cache_sb_k[l], cache_sb_v[l]) for l in range(depth)]
    assert cache_mla_ckv.shape[2] == SAMPLE_CFG["past"]
    y_s, st_s = _trunk(x_sample, mods[:, bp:], pasts, layers, SAMPLE_CFG)
    return (y_p, y_s, *st_p, *st_s)
```

```python
import functools

import numpy as np
import jax
import jax.numpy as jnp
from jax import lax
from jax.experimental import pallas as pl
from jax.experimental.pallas import tpu as pltpu

F32 = jnp.float32
BF16 = jnp.bfloat16

D_MODEL = 1024
CHUNK = 64
RMS_EPS = 1e-6
GN_EPS = 64e-5
NEG_BIG = -1e30
ROPE_THETA = 10000.0
LOG2E = 1.4426950408889634

HEADS = 8
HEAD_DIM = 64
WIDTH = HEADS * HEAD_DIM
PAIRS = HEADS // 2
LANES = 128
LORA = 64
SHIFT_W = 3 * WIDTH + 2 * LORA
Q_LORA = 384
KV_LORA = 256
ROPE = 32
NOPE = 64
QK = NOPE + ROPE
RWKV_CHUNK = 16

IN_GROUPS = (("pr", SHIFT_W), ("zA", WIDTH), ("cq", Q_LORA), ("ckv", KV_LORA), ("kr", ROPE),
             ("zB", WIDTH), ("sq", WIDTH), ("sk", WIDTH), ("sv", WIDTH), ("zC", WIDTH),
             ("gates", 3 * D_MODEL))

VMEM_LIMIT = 56 * 1024 * 1024


def _params(sem):
    return pltpu.CompilerParams(dimension_semantics=sem, vmem_limit_bytes=VMEM_LIMIT)


def _bdot(a, b):
    return jnp.dot(a.astype(BF16), b.astype(BF16), preferred_element_type=F32)


def _bdot_nt(a, b):
    return lax.dot_general(a.astype(BF16), b.astype(BF16), (((1,), (1,)), ((), ())),
                           preferred_element_type=F32)


def _bdot_tn(a, b):
    return lax.dot_general(a.astype(BF16), b.astype(BF16), (((0,), (0,)), ((), ())),
                           preferred_element_type=F32)


def _split(x, n):
    parts, r = [], x
    for _ in range(n):
        p = r.astype(BF16)
        parts.append(p)
        r = r - p.astype(F32)
    return parts


def _xdot_r(x, m01, n):
    m = m01.astype(BF16)
    return sum(jnp.dot(p, m, preferred_element_type=F32) for p in _split(x, n))


def _xdot_l(m01, x, n):
    m = m01.astype(BF16)
    return sum(jnp.dot(m, p, preferred_element_type=F32) for p in _split(x, n))


def _sigmoid(x):
    return 1.0 / (1.0 + jnp.exp(-x))


def _silu(x):
    return x * _sigmoid(x)


def _softplus(x):
    return jnp.maximum(x, 0.0) + jnp.log1p(jnp.exp(-jnp.abs(x)))


def _rms(x, g):
    return x * lax.rsqrt(jnp.mean(x * x, axis=-1, keepdims=True) + RMS_EPS) * g


def _iota2(shape, dim):
    return lax.broadcasted_iota(jnp.int32, shape, dim)


def _ada_kernel(c_ref, w_ref, b_ref, o_ref):
    o_ref[0] = _bdot(_silu(c_ref[...]), w_ref[0]) + b_ref[0]


def ada_mod(c, w_ada, b_ada):
    L, D, N = w_ada.shape
    rows = c.shape[0]
    tn = 1024
    return pl.pallas_call(
        _ada_kernel,
        grid=(L, N // tn),
        in_specs=[pl.BlockSpec((rows, D), lambda l, j: (0, 0)),
                  pl.BlockSpec((1, D, tn), lambda l, j: (l, 0, j)),
                  pl.BlockSpec((1, 1, tn), lambda l, j: (l, 0, j))],
        out_specs=pl.BlockSpec((1, rows, tn), lambda l, j: (l, 0, j)),
        out_shape=jax.ShapeDtypeStruct((L, rows, N), F32),
        compiler_params=_params(("parallel", "parallel")),
        name="ada_mod",
    )(c, w_ada, b_ada.reshape(L, 1, N))


def _inproj_kernel(x_ref, sc_ref, sh_ref, g_ref, *refs):
    n = len(IN_GROUPS)
    ws, outs = refs[:n], refs[n:]
    h = _rms(x_ref[...], g_ref[...]) * (1.0 + sc_ref[0]) + sh_ref[0]
    hb = h.astype(BF16)
    for w, o in zip(ws, outs):
        o[...] = jnp.dot(hb, w[...], preferred_element_type=F32)


def in_proj(x2, scale, shift, g, ws, T, tm):
    M, D = x2.shape
    assert T % tm == 0 and M % tm == 0
    per = T // tm
    row = lambda i: (i, 0)
    mod = lambda i: (i // per, 0, 0)
    const = lambda i: (0, 0)
    in_specs = [pl.BlockSpec((tm, D), row), pl.BlockSpec((1, 1, D), mod), pl.BlockSpec((1, 1, D), mod),
                pl.BlockSpec((1, D), const)]
    in_specs += [pl.BlockSpec((D, wd), const, pipeline_mode=pl.Buffered(1)) for _, wd in IN_GROUPS]
    out_specs = [pl.BlockSpec((tm, wd), row) for _, wd in IN_GROUPS]
    out_shape = [jax.ShapeDtypeStruct((M, wd), F32) for _, wd in IN_GROUPS]
    return pl.pallas_call(
        _inproj_kernel, grid=(M // tm,), in_specs=in_specs, out_specs=out_specs, out_shape=out_shape,
        compiler_params=_params(("parallel",)), name="in_proj",
    )(x2, scale, shift, g, *ws)


def _rwkv_kernel(n, pr_ref, past_ref, s0_ref, mu_ref, w0_ref, a0_ref, kk_ref, ka_ref, rk_ref,
                 gng_ref, gnb_ref, wup_ref, aup_ref, y_ref, sfin_ref, st_scr, last_scr):
    tb = pl.program_id(1)

    @pl.when(tb == 0)
    def _():
        st_scr[...] = s0_ref[0]
        last_scr[...] = past_ref[0]

    p = pr_ref[0]
    row = _iota2((n, 1), 0)
    prev = jnp.where(row == 0, last_scr[...], pltpu.roll(p, 1, axis=0))
    last_scr[...] = p[n - 1:n, :]
    xs = p + (prev - p) * mu_ref[...]
    r, k, v = xs[:, :WIDTH], xs[:, WIDTH:2 * WIDTH], xs[:, 2 * WIDTH:3 * WIDTH]
    wa = xs[:, 3 * WIDTH:]
    w_log = -_softplus(-(w0_ref[...] + _bdot(jnp.tanh(wa), wup_ref[...]))) - 0.5
    lw = -jnp.exp(w_log)
    a = _sigmoid(a0_ref[...] + _bdot(wa, aup_ref[...]))

    ri, ci = _iota2((LANES, LANES), 0), _iota2((LANES, LANES), 1)
    seg64 = ((ri >> 6) == (ci >> 6)).astype(BF16)
    bd2 = ((ri >> 6) == (ci >> 6)).astype(F32)

    def segsum(t):
        return jnp.concatenate(
            [_xdot_r(t[:, LANES * q:LANES * (q + 1)], seg64, 2) for q in range(PAIRS)], axis=-1)

    kkr = k * kk_ref[...]
    kk = kkr / jnp.maximum(jnp.sqrt(segsum(kkr * kkr)), 1e-12)
    k2 = k * (1.0 + (a - 1.0) * ka_ref[...])
    bonus = segsum(r * k2 * rk_ref[...]) * v

    tr, tc = _iota2((n, n), 0), _iota2((n, n), 1)
    same = (tr >> 4) == (tc >> 4)
    strict = (same & (tr > tc)).astype(F32)
    incl = (same & (tr >= tc)).astype(F32)
    eye = (tr == tc).astype(F32)
    g_log = _xdot_l(incl, lw, 3)
    g_end = _xdot_l(same.astype(F32), lw, 3)
    e_in, e_inv, e_rest = jnp.exp(g_log), jnp.exp(-g_log), jnp.exp(g_end - g_log)
    r_t = r * e_in
    kk_t = kk * jnp.exp(g_log - lw)
    kka = kk * a
    k_t, b_t = k2 * e_inv, kka * e_inv
    k_e, b_e = k2 * e_rest, kka * e_rest
    dec = jnp.exp(g_end)

    lane = _iota2((1, LANES), 1)
    head_mask = [(lane < HEAD_DIM).astype(F32), (lane >= HEAD_DIM).astype(F32)]

    sls = [slice(LANES * q, LANES * (q + 1)) for q in range(PAIRS)]
    units = [(q, hm) for q in range(PAIRS) for hm in head_mask]
    rp, kkp, vp = [r_t[:, s] for s in sls], [kk_t[:, s] for s in sls], [v[:, s] for s in sls]
    kb, bb = [k_t[:, s].astype(BF16) for s in sls], [b_t[:, s].astype(BF16) for s in sls]
    vb, kkb = [x.astype(BF16) for x in vp], [x.astype(BF16) for x in kkp]
    kkm = [(kkp[q] * hm).astype(BF16) for q, hm in units]
    rm = [(rp[q] * hm).astype(BF16) for q, hm in units]
    l_b = [_bdot_nt(kkm[u], bb[q]) * strict for u, (q, _) in enumerate(units)]
    l_k = [_bdot_nt(kkm[u], kb[q]) * strict for u, (q, _) in enumerate(units)]
    r_k = [_bdot_nt(rm[u], kb[q]) * incl for u, (q, _) in enumerate(units)]
    r_b = [_bdot_nt(rm[u], bb[q]) * incl for u, (q, _) in enumerate(units)]
    l2 = [_bdot(x, x) for x in l_b]
    l4 = [_bdot(x, x) for x in l2]
    l8 = [_bdot(x, x) for x in l4]
    t_inv = [eye - x for x in l_b]
    for pw in (l2, l4, l8):
        t_inv = [t + _bdot(t, x) for t, x in zip(t_inv, pw)]
    lkv = [_bdot(l_k[u], vb[q]) for u, (q, _) in enumerate(units)]
    w_h = [_bdot(t_inv[u], kkb[q]) for u, (q, _) in enumerate(units)]
    u0_h = [_bdot(t, x) for t, x in zip(t_inv, lkv)]
    q_h = [rp[q] - _bdot(r_b[u], w_h[u]) for u, (q, _) in enumerate(units)]
    y0_h = [_bdot(r_k[u], vb[q]) - _bdot(r_b[u], u0_h[u]) for u, (q, _) in enumerate(units)]
    first = head_mask[0] > 0.5
    pair = lambda xs: [jnp.where(first, xs[2 * q], xs[2 * q + 1]) for q in range(PAIRS)]
    w_c, u0_c, q_c, y0_c = pair(w_h), pair(u0_h), pair(q_h), pair(y0_h)

    st = [st_scr[q] for q in range(PAIRS)]
    ys = [[] for _ in range(PAIRS)]
    for c in range(n // RWKV_CHUNK):
        rs = slice(RWKV_CHUNK * c, RWKV_CHUNK * (c + 1))
        be = [b_e[rs, s] for s in sls]
        n_c = [_bdot_tn(vp[q][rs], k_e[rs, sls[q]]) - _bdot_tn(u0_c[q][rs], be[q]) for q in range(PAIRS)]
        for q in range(PAIRS):
            ys[q].append(_bdot_nt(q_c[q][rs], st[q]) + y0_c[q][rs])
        z = [_bdot_nt(st[q], w_c[q][rs]) for q in range(PAIRS)]
        zb = [_bdot(z[q], be[q]) for q in range(PAIRS)]
        st = [st[q] * dec[RWKV_CHUNK * c:RWKV_CHUNK * c + 1, sls[q]] + (n_c[q] - zb[q]) * bd2 for q in range(PAIRS)]
    for q in range(PAIRS):
        st_scr[q] = st[q]
    y = [x[0] if len(x) == 1 else jnp.concatenate(x, axis=0) for x in ys]
    mean = [_xdot_r(x, seg64, 2) * (1.0 / HEAD_DIM) for x in y]
    d = [x - m for x, m in zip(y, mean)]
    var = [_xdot_r(x * x, seg64, 2) * (1.0 / HEAD_DIM) for x in d]
    for q, s in enumerate(sls):
        y_ref[0, :, s] = d[q] * lax.rsqrt(var[q] + GN_EPS) * gng_ref[:, s] + gnb_ref[:, s] + bonus[:, s]

    @pl.when(tb == pl.num_programs(1) - 1)
    def _():
        sfin_ref[0] = st_scr[...]


def rwkv_branch(pr, past_shift, s0_bd, wl, n):
    B, T, _ = pr.shape
    assert T % n == 0 and n % RWKV_CHUNK == 0
    vec = lambda wd: pl.BlockSpec((1, wd), lambda b, t: (0, 0))
    in_specs = [pl.BlockSpec((1, n, SHIFT_W), lambda b, t: (b, t, 0)),
                pl.BlockSpec((1, 1, SHIFT_W), lambda b, t: (b, 0, 0)),
                pl.BlockSpec((1, PAIRS, LANES, LANES), lambda b, t: (b, 0, 0, 0)),
                vec(SHIFT_W)] + [vec(WIDTH)] * 7 + [
                pl.BlockSpec((LANES, WIDTH), lambda b, t: (0, 0)),
                pl.BlockSpec((LANES, WIDTH), lambda b, t: (0, 0))]
    return pl.pallas_call(
        functools.partial(_rwkv_kernel, n),
        grid=(B, T // n),
        in_specs=in_specs,
        out_specs=[pl.BlockSpec((1, n, WIDTH), lambda b, t: (b, t, 0)),
                   pl.BlockSpec((1, PAIRS, LANES, LANES), lambda b, t: (b, 0, 0, 0))],
        out_shape=[jax.ShapeDtypeStruct((B, T, WIDTH), F32),
                   jax.ShapeDtypeStruct((B, PAIRS, LANES, LANES), F32)],
        scratch_shapes=[pltpu.VMEM((PAIRS, LANES, LANES), F32), pltpu.VMEM((1, SHIFT_W), F32)],
        compiler_params=_params(("parallel", "arbitrary")),
        name="rwkv7",
    )(pr, past_shift, s0_bd, wl["mu"], wl["w0"], wl["a0"], wl["k_k"], wl["k_a"], wl["r_k"],
      wl["gn_g"], wl["gn_b"], wl["w_up"], wl["a_up"])


def _state_to_bd(s):
    B = s.shape[0]
    s = s.reshape(B, PAIRS, 2, HEAD_DIM, HEAD_DIM)
    z = jnp.zeros_like(s[:, :, 0])
    top = jnp.concatenate([s[:, :, 0], z], axis=-1)
    bot = jnp.concatenate([z, s[:, :, 1]], axis=-1)
    return jnp.concatenate([top, bot], axis=-2)


def _bd_to_state(s):
    B = s.shape[0]
    h0 = s[:, :, :HEAD_DIM, :HEAD_DIM]
    h1 = s[:, :, HEAD_DIM:, HEAD_DIM:]
    return jnp.stack([h0, h1], axis=2).reshape(B, HEADS, HEAD_DIM, HEAD_DIM)


def _mla_q_kernel(cq_ref, ckv_ref, kr_ref, c96_ref, s96_ref, c32_ref, s32_ref, gq_ref, wuq_ref, gqn_ref,
                  gkv_ref, gkr_ref, p96_ref, p32_ref, q_ref, ckvn_ref, krope_ref):
    cqb = _rms(cq_ref[0], gq_ref[...]).astype(BF16)
    nope = _iota2((1, QK), 1) < NOPE
    c96, s96 = c96_ref[...], s96_ref[...]
    for h in range(HEADS):
        qh = jnp.dot(cqb, wuq_ref[h], preferred_element_type=F32)
        sq = qh * qh
        ss_n = jnp.sum(jnp.where(nope, sq, 0.0), axis=-1, keepdims=True)
        ss_r = jnp.sum(jnp.where(nope, 0.0, sq), axis=-1, keepdims=True)
        inv = jnp.where(nope, lax.rsqrt(ss_n * (1.0 / NOPE) + RMS_EPS), lax.rsqrt(ss_r * (1.0 / ROPE) + RMS_EPS))
        qn = qh * inv * gqn_ref[...]
        qo = qn * c96 + _xdot_r(qn, p96_ref[...], 3) * s96
        q_ref[0, h] = (qo * (QK ** -0.5 * LOG2E)).astype(BF16)
    ckvn_ref[0] = _rms(ckv_ref[0], gkv_ref[...])
    krn = _rms(kr_ref[0], gkr_ref[...])
    krope_ref[0] = krn * c32_ref[...] + _xdot_r(krn, p32_ref[...], 3) * s32_ref[...]


def mla_q(cq, ckv, kr, tabs, wl, tm):
    B, T, _ = cq.shape
    c96, s96, c32, s32 = tabs
    tok = lambda wd: pl.BlockSpec((1, tm, wd), lambda b, t: (b, t, 0))
    tab = lambda wd: pl.BlockSpec((tm, wd), lambda b, t: (t, 0))
    vec = lambda wd: pl.BlockSpec((1, wd), lambda b, t: (0, 0))
    return pl.pallas_call(
        _mla_q_kernel,
        grid=(B, T // tm),
        in_specs=[tok(Q_LORA), tok(KV_LORA), tok(ROPE), tab(QK), tab(QK), tab(ROPE), tab(ROPE),
                  vec(Q_LORA), pl.BlockSpec((HEADS, Q_LORA, QK), lambda b, t: (0, 0, 0)), vec(QK),
                  vec(KV_LORA), vec(ROPE),
                  pl.BlockSpec((QK, QK), lambda b, t: (0, 0)), pl.BlockSpec((ROPE, ROPE), lambda b, t: (0, 0))],
        out_specs=[pl.BlockSpec((1, HEADS, tm, QK), lambda b, t: (b, 0, t, 0)), tok(KV_LORA), tok(ROPE)],
        out_shape=[jax.ShapeDtypeStruct((B, HEADS, T, QK), BF16),
                   jax.ShapeDtypeStruct((B, T, KV_LORA), F32),
                   jax.ShapeDtypeStruct((B, T, ROPE), F32)],
        compiler_params=_params(("parallel", "parallel")),
        name="mla_q",
    )(cq, ckv, kr, c96, s96, c32, s32, wl["q_norm"], wl["w_uq"], wl["qn"], wl["kv_norm"], wl["kn_rope"],
      wl["p96"], wl["p32"])


def _mla_kv_kernel(ckv_ref, kr_ref, wuk_ref, wuv_ref, gkn_ref, e_ref, k_ref, v_ref):
    cb = ckv_ref[0].astype(BF16)
    k_rope = _xdot_r(kr_ref[0], e_ref[...], 3)
    for h in range(HEADS):
        kh = jnp.dot(cb, wuk_ref[h], preferred_element_type=F32)
        ms = jnp.sum(kh * kh, axis=-1, keepdims=True) * (1.0 / NOPE)
        k_ref[0, h] = (kh * lax.rsqrt(ms + RMS_EPS) * gkn_ref[...] + k_rope).astype(BF16)
    for q in range(PAIRS):
        v_ref[0, q] = jnp.dot(cb, wuv_ref[q], preferred_element_type=F32).astype(BF16)


def mla_kv(ckv_all, kr_all, wl, ts):
    B, S, _ = ckv_all.shape
    return pl.pallas_call(
        _mla_kv_kernel,
        grid=(B, S // ts),
        in_specs=[pl.BlockSpec((1, ts, KV_LORA), lambda b, t: (b, t, 0)),
                  pl.BlockSpec((1, ts, ROPE), lambda b, t: (b, t, 0)),
                  pl.BlockSpec((HEADS, KV_LORA, QK), lambda b, t: (0, 0, 0)),
                  pl.BlockSpec((PAIRS, KV_LORA, LANES), lambda b, t: (0, 0, 0)),
                  pl.BlockSpec((1, QK), lambda b, t: (0, 0)),
                  pl.BlockSpec((ROPE, QK), lambda b, t: (0, 0))],
        out_specs=[pl.BlockSpec((1, HEADS, ts, QK), lambda b, t: (b, 0, t, 0)),
                   pl.BlockSpec((1, PAIRS, ts, LANES), lambda b, t: (b, 0, t, 0))],
        out_shape=[jax.ShapeDtypeStruct((B, HEADS, S, QK), BF16),
                   jax.ShapeDtypeStruct((B, PAIRS, S, LANES), BF16)],
        compiler_params=_params(("parallel", "parallel")),
        name="mla_kv",
    )(ckv_all, kr_all, wl["w_uk"], wl["w_uv"], wl["kn_nope"], wl["e96"])


def _mla_attn_kernel(tq, tk, past, s_len, q_ref, k_ref, v_ref, o_ref, m_scr, l_scr, acc_scr):
    nk = k_ref.shape[2] // tk
    reps = tk // LANES
    q0 = past + pl.program_id(2) * tq
    first = _iota2((1, LANES), 1) < HEAD_DIM
    m_scr[...] = jnp.full(m_scr.shape, NEG_BIG, F32)
    l_scr[...] = jnp.zeros(l_scr.shape, F32)
    acc_scr[...] = jnp.zeros(acc_scr.shape, F32)
    qs = [q_ref[0, h] for h in range(2)]

    def lane_chunks(x):
        return [x[:, LANES * c:LANES * (c + 1)] for c in range(reps)]

    def scores(kb, masked):
        ks = pl.multiple_of(kb * tk, tk)
        s = [_bdot_nt(qs[h], k_ref[0, h, pl.ds(ks, tk), :]) for h in range(2)]
        mask = None
        if masked:
            qpos = q0 + _iota2((tq, 1), 0)
            kidx = ks + _iota2((1, tk), 1)
            mask = ((kidx >> 6) <= (qpos >> 6)) & (kidx < s_len)
            s = [jnp.where(mask, x, NEG_BIG) for x in s]
        return ks, s, mask

    def max_block(kb, masked):
        _, s, _ = scores(kb, masked)
        for h in range(2):
            m_scr[h] = functools.reduce(jnp.maximum, lane_chunks(s[h]), m_scr[h])

    def pv_block(kb, masked):
        ks, s, mask = scores(kb, masked)
        vb = v_ref[0, 0, pl.ds(ks, tk), :]
        p = [jnp.exp2(s[h] - jnp.concatenate([m_scr[h]] * reps, axis=1)) for h in range(2)]
        if masked:
            p = [jnp.where(mask, x, 0.0) for x in p]
        for h in range(2):
            l_scr[h] = functools.reduce(jnp.add, lane_chunks(p[h]), l_scr[h])
        pv = [_bdot(p[h], vb) for h in range(2)]
        acc_scr[...] += jnp.where(first, pv[0], pv[1])

    n_full = jnp.minimum((q0 // CHUNK + 1) * CHUNK, s_len) // tk
    last = jnp.minimum((((q0 + tq - 1) // CHUNK) * CHUNK + CHUNK - 1) // tk, nk - 1)

    def sweep(fn):
        def run(masked):
            def body(kb, carry):
                fn(kb, masked)
                return carry
            return body
        lax.fori_loop(0, n_full, run(False), 0)
        lax.fori_loop(n_full, last + 1, run(True), 0)

    sweep(max_block)
    for h in range(2):
        m_scr[h] = jnp.broadcast_to(jnp.max(m_scr[h], axis=-1, keepdims=True), (tq, LANES))
    sweep(pv_block)
    l = [jnp.sum(l_scr[h], axis=-1, keepdims=True) for h in range(2)]
    o_ref[0] = acc_scr[...] / jnp.where(first, l[0], l[1])


def mla_attn(q, k, v, past, s_len, tq, tk):
    B, _, T, _ = q.shape
    S = k.shape[2]
    assert S % tk == 0 and T % tq == 0
    return pl.pallas_call(
        functools.partial(_mla_attn_kernel, tq, tk, past, s_len),
        grid=(B, PAIRS, T // tq),
        in_specs=[pl.BlockSpec((1, 2, tq, QK), lambda b, p, i: (b, p, i, 0)),
                  pl.BlockSpec((1, 2, S, QK), lambda b, p, i: (b, p, 0, 0)),
                  pl.BlockSpec((1, 1, S, LANES), lambda b, p, i: (b, p, 0, 0))],
        out_specs=pl.BlockSpec((1, tq, LANES), lambda b, p, i: (b, i, p)),
        out_shape=jax.ShapeDtypeStruct((B, T, WIDTH), F32),
        scratch_shapes=[pltpu.VMEM((2, tq, LANES), F32), pltpu.VMEM((2, tq, LANES), F32),
                        pltpu.VMEM((tq, LANES), F32)],
        compiler_params=_params(("parallel", "parallel", "arbitrary")),
        name="mla_attn",
    )(q, k, v)


def _sb_kernel(tq, tk, past, q_ref, k_ref, v_ref, sum_ref, o_ref, carry_scr, acc_scr):
    nk = k_ref.shape[1] // tk
    q0 = past + pl.program_id(2) * tq
    first = _iota2((1, LANES), 1) < HEAD_DIM
    carry_scr[...] = jnp.zeros(carry_scr.shape, F32)
    acc_scr[...] = jnp.zeros(acc_scr.shape, F32)
    qv = q_ref[0] * (HEAD_DIM ** -0.5 * LOG2E)
    qm = [jnp.where(first, qv, 0.0).astype(BF16), jnp.where(first, 0.0, qv).astype(BF16)]

    def block(kb, masked):
        ks = pl.multiple_of(kb * tk, tk)
        kblk, vblk = k_ref[0, pl.ds(ks, tk), :], v_ref[0, pl.ds(ks, tk), :]
        z = [_bdot_nt(qm[h], kblk) for h in range(2)]
        e = [jnp.exp2(-jnp.abs(x)) for x in z]
        w = [jnp.maximum(z[h], 0.0) + jnp.log2(1.0 + e[h]) for h in range(2)]
        if masked:
            mask = (ks + _iota2((1, tk), 1)) < (q0 + _iota2((tq, 1), 0))
            w = [jnp.where(mask, x, 0.0) for x in w]
        sums = [_xdot_r(x, sum_ref[...], 2) for x in w]
        c = [carry_scr[h] for h in range(2)]
        cb = [jnp.concatenate([x] * (tk // LANES), axis=1) for x in c]
        a = [jnp.exp2(z[h] - w[h] - sums[h] - cb[h]) for h in range(2)]
        if masked:
            a = [jnp.where(mask, x, 0.0) for x in a]
        pv = [_bdot(a[h], vblk) for h in range(2)]
        for h in range(2):
            carry_scr[h] = c[h] + jnp.sum(w[h], axis=-1, keepdims=True)
        acc_scr[...] += jnp.where(first, pv[0], pv[1])

    n_full = q0 // tk
    last = jnp.clip((q0 + tq - 2) // tk, 0, nk - 1)

    def run(masked, top):
        def body(i, carry):
            block(top - i, masked)
            return carry
        return body

    lax.fori_loop(0, last - n_full + 1, run(True, last), 0)
    lax.fori_loop(0, n_full, run(False, n_full - 1), 0)
    o_ref[0] = acc_scr[...]


def sb_attn(q, k_all, v_all, past, tq, tk):
    B, T, _ = q.shape
    S = k_all.shape[1]
    assert S % tk == 0 and T % tq == 0 and tk % LANES == 0
    sum_mat = jnp.asarray(np.arange(tk)[:, None] > np.arange(tk)[None, :], BF16)
    return pl.pallas_call(
        functools.partial(_sb_kernel, tq, tk, past),
        grid=(B, PAIRS, T // tq),
        in_specs=[pl.BlockSpec((1, tq, LANES), lambda b, p, i: (b, i, p)),
                  pl.BlockSpec((1, S, LANES), lambda b, p, i: (b, 0, p)),
                  pl.BlockSpec((1, S, LANES), lambda b, p, i: (b, 0, p)),
                  pl.BlockSpec((tk, tk), lambda b, p, i: (0, 0))],
        out_specs=pl.BlockSpec((1, tq, LANES), lambda b, p, i: (b, i, p)),
        out_shape=jax.ShapeDtypeStruct((B, T, WIDTH), F32),
        scratch_shapes=[pltpu.VMEM((2, tq, LANES), F32), pltpu.VMEM((tq, LANES), F32)],
        compiler_params=_params(("parallel", "parallel", "arbitrary")),
        name="sb_attn",
    )(q, k_all, v_all, sum_mat)


def _merge_kernel(x_ref, gm_ref, ya_ref, za_ref, yb_ref, zb_ref, yc_ref, zc_ref, g_ref,
                  wa_ref, wb_ref, wc_ref, wo_ref, o_ref):
    def branch(y_ref, z_ref, w_ref):
        return _bdot(y_ref[...] * _silu(z_ref[...]), w_ref[...])

    sg = _sigmoid(g_ref[...])
    merged = (sg[:, :D_MODEL] * branch(ya_ref, za_ref, wa_ref)
              + sg[:, D_MODEL:2 * D_MODEL] * branch(yb_ref, zb_ref, wb_ref)
              + sg[:, 2 * D_MODEL:] * branch(yc_ref, zc_ref, wc_ref))
    o_ref[...] = x_ref[...] + gm_ref[0] * _bdot(merged, wo_ref[...])


def merge_out(x2, gate_mod, ya, za, yb, zb, yc, zc, gates, wl, T, tm):
    M, D = x2.shape
    per = T // tm
    row = lambda wd: pl.BlockSpec((tm, wd), lambda i: (i, 0))
    const = lambda r, c: pl.BlockSpec((r, c), lambda i: (0, 0))
    return pl.pallas_call(
        _merge_kernel,
        grid=(M // tm,),
        in_specs=[row(D), pl.BlockSpec((1, 1, D), lambda i: (i // per, 0, 0))] + [row(WIDTH)] * 6 + [row(3 * D),
                  const(WIDTH, D), const(WIDTH, D), const(WIDTH, D), const(D, D)],
        out_specs=row(D),
        out_shape=jax.ShapeDtypeStruct((M, D), F32),
        compiler_params=_params(("parallel",)),
        name="merge_out",
    )(x2, gate_mod, ya, za, yb, zb, yc, zc, gates, wl["w_br_rwkv"], wl["w_br_mla"], wl["w_br_sb"], wl["w_out"])


def _rot_matrix(width, offset):
    half = ROPE // 2
    m = np.zeros((width, width), np.float32)
    for i in range(half):
        m[offset + half + i, offset + i] = -1.0
        m[offset + i, offset + half + i] = 1.0
    return jnp.asarray(m, BF16)


def _prep_layer(P, l):
    row = lambda a: a[l].reshape(1, -1)
    w_in = P["w_in"][l]
    offs = np.cumsum([0] + [wd for _, wd in IN_GROUPS])
    zeros_lora = jnp.zeros((LORA, WIDTH), F32)
    w_uq = P["mla_w_uq"][l].reshape(Q_LORA, HEADS, QK).transpose(1, 0, 2)
    w_ukv = P["mla_w_ukv"][l].reshape(KV_LORA, HEADS, 2 * HEAD_DIM)
    w_uk = jnp.pad(w_ukv[:, :, :NOPE].transpose(1, 0, 2), ((0, 0), (0, 0), (0, ROPE)))
    w_uv = w_ukv[:, :, NOPE:].reshape(KV_LORA, PAIRS, LANES).transpose(1, 0, 2)
    e96 = np.zeros((ROPE, QK), np.float32)
    e96[np.arange(ROPE), NOPE + np.arange(ROPE)] = 1.0
    return dict(
        norm_g=row(P["norm_g"]),
        w_in=[w_in[:, offs[i]:offs[i + 1]].astype(BF16) for i in range(len(IN_GROUPS))],
        mu=row(P["rwkv_mu"]), w0=row(P["rwkv_w0"]), a0=row(P["rwkv_a0"]), k_k=row(P["rwkv_k_k"]),
        k_a=row(P["rwkv_k_a"]), r_k=row(P["rwkv_r_k"]), gn_g=row(P["rwkv_gn_g"]), gn_b=row(P["rwkv_gn_b"]),
        w_up=jnp.concatenate([P["rwkv_w_up"][l], zeros_lora], axis=0).astype(BF16),
        a_up=jnp.concatenate([zeros_lora, P["rwkv_a_up"][l]], axis=0).astype(BF16),
        q_norm=row(P["mla_q_norm"]), w_uq=w_uq.astype(BF16),
        qn=jnp.concatenate([P["mla_qn_nope"][l], P["mla_qn_rope"][l]]).reshape(1, QK),
        kv_norm=row(P["mla_kv_norm"]), kn_rope=row(P["mla_kn_rope"]),
        kn_nope=jnp.pad(P["mla_kn_nope"][l], (0, ROPE)).reshape(1, QK),
        w_uk=w_uk.astype(BF16), w_uv=w_uv.astype(BF16),
        p96=_rot_matrix(QK, NOPE), p32=_rot_matrix(ROPE, 0), e96=jnp.asarray(e96, BF16),
        w_br_rwkv=P["w_br_rwkv"][l].astype(BF16), w_br_mla=P["w_br_mla"][l].astype(BF16),
        w_br_sb=P["w_br_sb"][l].astype(BF16), w_out=P["w_out"][l].astype(BF16),
    )


def _rope_tables(past, T):
    half = ROPE // 2
    inv = ROPE_THETA ** (-jnp.arange(half, dtype=F32) / half)
    ang = (past + jnp.arange(T, dtype=jnp.int32)).astype(F32)[:, None] * inv
    c32 = jnp.tile(jnp.cos(ang), (1, 2))
    s32 = jnp.tile(jnp.sin(ang), (1, 2))
    c96 = jnp.concatenate([jnp.ones((T, NOPE), F32), c32], axis=1)
    s96 = jnp.concatenate([jnp.zeros((T, NOPE), F32), s32], axis=1)
    return c96, s96, c32, s32


def _pad_rows(a, rows):
    return a if a.shape[1] == rows else jnp.pad(a, ((0, 0), (0, rows - a.shape[1])) + ((0, 0),) * (a.ndim - 2))


def _trunk(x, mods, pasts, layers, cfg):
    B, T, D = x.shape
    past = cfg["past"]
    S = past + T
    s_pad = -(-S // cfg["kv_mult"]) * cfg["kv_mult"]
    tabs = _rope_tables(past, T)
    x2 = x.reshape(B * T, D)
    new = []
    for l, wl in enumerate(layers):
        shift, scale, gate = (mods[l][:, None, i * D:(i + 1) * D] for i in range(3))
        outs = in_proj(x2, scale, shift, wl["norm_g"], wl["w_in"], T, cfg["tm"])
        pr, zA, cq, ckv, kr, zB, sq, sk, sv, zC, gates = outs
        seq = lambda a: a.reshape(B, T, a.shape[-1])
        if pasts is None:
            s0 = jnp.zeros((B, PAIRS, LANES, LANES), F32)
            shift0 = jnp.zeros((B, 1, SHIFT_W), F32)
        else:
            s0, shift0 = _state_to_bd(pasts[l][0]), pasts[l][1]
        pr3 = seq(pr)
        yA, s_fin = rwkv_branch(pr3, shift0, s0, wl, cfg["rwkv_n"])
        q, ckv_n, k_rope = mla_q(seq(cq), seq(ckv), seq(kr), tabs, wl, cfg["tm_q"])
        sk3, sv3 = seq(sk), seq(sv)
        if pasts is None:
            ckv_all, kr_all, k_all, v_all = ckv_n, k_rope, sk3, sv3
        else:
            _, _, ckv0, kr0, k0, v0 = pasts[l]
            ckv_all = jnp.concatenate([ckv0, ckv_n], axis=1)
            kr_all = jnp.concatenate([kr0, k_rope], axis=1)
            k_all = jnp.concatenate([k0.reshape(B, past, WIDTH), sk3], axis=1)
            v_all = jnp.concatenate([v0.reshape(B, past, WIDTH), sv3], axis=1)
        ckv_all, kr_all, k_all, v_all = (_pad_rows(a, s_pad) for a in (ckv_all, kr_all, k_all, v_all))
        kf, vf = mla_kv(ckv_all, kr_all, wl, cfg["ts"])
        yB = mla_attn(q, kf, vf, past, S, cfg["tq"], cfg["tk"])
        yC = sb_attn(seq(sq), k_all.astype(BF16), v_all.astype(BF16), past, cfg["sb_tq"], cfg["sb_tk"])
        x2 = merge_out(x2, gate, yA.reshape(B * T, WIDTH), zA, yB.reshape(B * T, WIDTH), zB,
                       yC.reshape(B * T, WIDTH), zC, gates, wl, T, cfg["tm"])
        new.append((_bd_to_state(s_fin), pr3[:, -1:], ckv_n, k_rope,
                    sk3.reshape(B, T, HEADS, HEAD_DIM), sv3.reshape(B, T, HEADS, HEAD_DIM)))
    stacked = [jnp.stack([st[i] for st in new], axis=0) for i in range(6)]
    return x2.reshape(B, T, D), stacked


PROMPT_CFG = dict(past=0, tm=256, rwkv_n=128, tm_q=256, ts=256, tq=512, tk=512, sb_tq=256, sb_tk=256, kv_mult=512)
SAMPLE_CFG = dict(past=2048, tm=16, rwkv_n=16, tm_q=16, ts=2176, tq=16, tk=2176, sb_tq=16, sb_tk=128, kv_mult=128)


def kernel(x_prompt, x_sample, state_rwkv_wkv, state_rwkv_shift, cache_mla_ckv, cache_mla_krope, cache_sb_k, cache_sb_v, c_prompt, c_sample, w_ada, b_ada, norm_g, w_in, rwkv_mu, rwkv_w0, rwkv_w_up, rwkv_a0, rwkv_a_up, rwkv_k_k, rwkv_k_a, rwkv_r_k, rwkv_gn_g, rwkv_gn_b, w_br_rwkv, mla_q_norm, mla_w_uq, mla_kv_norm, mla_w_ukv, mla_qn_nope, mla_qn_rope, mla_kn_nope, mla_kn_rope, w_br_mla, w_br_sb, w_out):
    P = dict(w_in=w_in, norm_g=norm_g, rwkv_mu=rwkv_mu, rwkv_w0=rwkv_w0, rwkv_w_up=rwkv_w_up, rwkv_a0=rwkv_a0,
             rwkv_a_up=rwkv_a_up, rwkv_k_k=rwkv_k_k, rwkv_k_a=rwkv_k_a, rwkv_r_k=rwkv_r_k, rwkv_gn_g=rwkv_gn_g,
             rwkv_gn_b=rwkv_gn_b, w_br_rwkv=w_br_rwkv, mla_q_norm=mla_q_norm, mla_w_uq=mla_w_uq,
             mla_kv_norm=mla_kv_norm, mla_w_ukv=mla_w_ukv, mla_qn_nope=mla_qn_nope, mla_qn_rope=mla_qn_rope,
             mla_kn_nope=mla_kn_nope, mla_kn_rope=mla_kn_rope, w_br_mla=w_br_mla, w_br_sb=w_br_sb, w_out=w_out)
    depth = w_in.shape[0]
    bp = x_prompt.shape[0]
    layers = [_prep_layer(P, l) for l in range(depth)]
    mods = ada_mod(jnp.concatenate([c_prompt, c_sample], axis=0), w_ada, b_ada)
    y_p, st_p = _trunk(x_prompt, mods[:, :bp], None, layers, PROMPT_CFG)
    pasts = [(state_rwkv_wkv[l], state_rwkv_shift[l], cache_mla_ckv[l], cache_mla_krope[l],
              cache_sb_k[l], cache_sb_v[l]) for l in range(depth)]
    assert cache_mla_ckv.shape[2] == SAMPLE_CFG["past"]
    y_s, st_s = _trunk(x_sample, mods[:, bp:], pasts, layers, SAMPLE_CFG)
    return (y_p, y_s, *st_p, *st_s)
```

```python
import functools

import numpy as np
import jax
import jax.numpy as jnp
from jax import lax
from jax.experimental import pallas as pl
from jax.experimental.pallas import tpu as pltpu

F32 = jnp.float32
BF16 = jnp.bfloat16

D_MODEL = 1024
CHUNK = 64
RMS_EPS = 1e-6
GN_EPS = 64e-5
NEG_BIG = -1e30
ROPE_THETA = 10000.0
LOG2E = 1.4426950408889634

HEADS = 8
HEAD_DIM = 64
WIDTH = HEADS * HEAD_DIM
PAIRS = HEADS // 2
LANES = 128
LORA = 64
SHIFT_W = 3 * WIDTH + 2 * LORA
Q_LORA = 384
KV_LORA = 256
ROPE = 32
NOPE = 64
QK = NOPE + ROPE
RWKV_CHUNK = 16

IN_GROUPS = (("pr", SHIFT_W), ("zA", WIDTH), ("cq", Q_LORA), ("ckv", KV_LORA), ("kr", ROPE),
             ("zB", WIDTH), ("sq", WIDTH), ("sk", WIDTH), ("sv", WIDTH), ("zC", WIDTH),
             ("gates", 3 * D_MODEL))
BF16_COPIES = ("sk", "sv")

VMEM_LIMIT = 56 * 1024 * 1024


def _params(sem):
    return pltpu.CompilerParams(dimension_semantics=sem, vmem_limit_bytes=VMEM_LIMIT)


def _bdot(a, b):
    return jnp.dot(a.astype(BF16), b.astype(BF16), preferred_element_type=F32)


def _bdot_nt(a, b):
    return lax.dot_general(a.astype(BF16), b.astype(BF16), (((1,), (1,)), ((), ())),
                           preferred_element_type=F32)


def _bdot_tn(a, b):
    return lax.dot_general(a.astype(BF16), b.astype(BF16), (((0,), (0,)), ((), ())),
                           preferred_element_type=F32)


def _split(x, n):
    parts, r = [], x
    for _ in range(n):
        p = r.astype(BF16)
        parts.append(p)
        r = r - p.astype(F32)
    return parts


def _xdot_r(x, m01, n):
    m = m01.astype(BF16)
    return sum(jnp.dot(p, m, preferred_element_type=F32) for p in _split(x, n))


def _xdot_l(m01, x, n):
    m = m01.astype(BF16)
    return sum(jnp.dot(m, p, preferred_element_type=F32) for p in _split(x, n))


def _sigmoid(x):
    return 1.0 / (1.0 + jnp.exp(-x))


def _silu(x):
    return x * _sigmoid(x)


def _softplus(x):
    return jnp.maximum(x, 0.0) + jnp.log1p(jnp.exp(-jnp.abs(x)))


def _rms(x, g):
    return x * lax.rsqrt(jnp.mean(x * x, axis=-1, keepdims=True) + RMS_EPS) * g


def _iota2(shape, dim):
    return lax.broadcasted_iota(jnp.int32, shape, dim)


def _ada_kernel(c_ref, w_ref, b_ref, o_ref):
    o_ref[0] = _bdot(_silu(c_ref[...]), w_ref[0]) + b_ref[0]


def ada_mod(c, w_ada, b_ada):
    L, D, N = w_ada.shape
    rows = c.shape[0]
    tn = 1024
    return pl.pallas_call(
        _ada_kernel,
        grid=(L, N // tn),
        in_specs=[pl.BlockSpec((rows, D), lambda l, j: (0, 0)),
                  pl.BlockSpec((1, D, tn), lambda l, j: (l, 0, j)),
                  pl.BlockSpec((1, 1, tn), lambda l, j: (l, 0, j))],
        out_specs=pl.BlockSpec((1, rows, tn), lambda l, j: (l, 0, j)),
        out_shape=jax.ShapeDtypeStruct((L, rows, N), F32),
        compiler_params=_params(("parallel", "parallel")),
        name="ada_mod",
    )(c, w_ada, b_ada.reshape(L, 1, N))


def _inproj_kernel(x_ref, sc_ref, sh_ref, g_ref, *refs):
    n = len(IN_GROUPS)
    ws, outs = refs[:n], refs[n:]
    h = _rms(x_ref[...], g_ref[...]) * (1.0 + sc_ref[0]) + sh_ref[0]
    hb = h.astype(BF16)
    extra = iter(outs[n:])
    for (name, _), w, o in zip(IN_GROUPS, ws, outs):
        res = jnp.dot(hb, w[...], preferred_element_type=F32)
        o[...] = res
        if name in BF16_COPIES:
            next(extra)[...] = res.astype(BF16)


def in_proj(x2, scale, shift, g, ws, T, tm):
    M, D = x2.shape
    assert T % tm == 0 and M % tm == 0
    per = T // tm
    row = lambda i: (i, 0)
    mod = lambda i: (i // per, 0, 0)
    const = lambda i: (0, 0)
    in_specs = [pl.BlockSpec((tm, D), row), pl.BlockSpec((1, 1, D), mod), pl.BlockSpec((1, 1, D), mod),
                pl.BlockSpec((1, D), const)]
    in_specs += [pl.BlockSpec((D, wd), const, pipeline_mode=pl.Buffered(1)) for _, wd in IN_GROUPS]
    out_specs = [pl.BlockSpec((tm, wd), row) for _, wd in IN_GROUPS]
    out_shape = [jax.ShapeDtypeStruct((M, wd), F32) for _, wd in IN_GROUPS]
    out_specs += [pl.BlockSpec((tm, WIDTH), row) for _ in BF16_COPIES]
    out_shape += [jax.ShapeDtypeStruct((M, WIDTH), BF16) for _ in BF16_COPIES]
    return pl.pallas_call(
        _inproj_kernel, grid=(M // tm,), in_specs=in_specs, out_specs=out_specs, out_shape=out_shape,
        compiler_params=_params(("parallel",)), name="in_proj",
    )(x2, scale, shift, g, *ws)


def _rwkv_kernel(n, pr_ref, past_ref, s0_ref, mu_ref, w0_ref, a0_ref, kk_ref, ka_ref, rk_ref,
                 gng_ref, gnb_ref, wup_ref, aup_ref, y_ref, sfin_ref, st_scr, last_scr):
    tb = pl.program_id(1)

    @pl.when(tb == 0)
    def _():
        st_scr[...] = s0_ref[0]
        last_scr[...] = past_ref[0]

    p = pr_ref[0]
    row = _iota2((n, 1), 0)
    prev = jnp.where(row == 0, last_scr[...], pltpu.roll(p, 1, axis=0))
    last_scr[...] = p[n - 1:n, :]
    xs = p + (prev - p) * mu_ref[...]
    r, k, v = xs[:, :WIDTH], xs[:, WIDTH:2 * WIDTH], xs[:, 2 * WIDTH:3 * WIDTH]
    wa = xs[:, 3 * WIDTH:]
    w_log = -_softplus(-(w0_ref[...] + _bdot(jnp.tanh(wa), wup_ref[...]))) - 0.5
    lw = -jnp.exp(w_log)
    a = _sigmoid(a0_ref[...] + _bdot(wa, aup_ref[...]))

    ri, ci = _iota2((LANES, LANES), 0), _iota2((LANES, LANES), 1)
    seg64 = ((ri >> 6) == (ci >> 6)).astype(BF16)
    bd2 = ((ri >> 6) == (ci >> 6)).astype(F32)

    def segsum(t):
        return jnp.concatenate(
            [_xdot_r(t[:, LANES * q:LANES * (q + 1)], seg64, 2) for q in range(PAIRS)], axis=-1)

    kkr = k * kk_ref[...]
    kk = kkr / jnp.maximum(jnp.sqrt(segsum(kkr * kkr)), 1e-12)
    k2 = k * (1.0 + (a - 1.0) * ka_ref[...])
    bonus = segsum(r * k2 * rk_ref[...]) * v

    tr, tc = _iota2((n, n), 0), _iota2((n, n), 1)
    same = (tr >> 4) == (tc >> 4)
    strict = (same & (tr > tc)).astype(F32)
    incl = (same & (tr >= tc)).astype(F32)
    eye = (tr == tc).astype(F32)
    g_log = _xdot_l(incl, lw, 3)
    g_end = _xdot_l(same.astype(F32), lw, 3)
    e_in, e_inv, e_rest = jnp.exp(g_log), jnp.exp(-g_log), jnp.exp(g_end - g_log)
    r_t = r * e_in
    kk_t = kk * jnp.exp(g_log - lw)
    kka = kk * a
    k_t, b_t = k2 * e_inv, kka * e_inv
    k_e, b_e = k2 * e_rest, kka * e_rest
    dec = jnp.exp(g_end)

    lane = _iota2((1, LANES), 1)
    head_mask = [(lane < HEAD_DIM).astype(F32), (lane >= HEAD_DIM).astype(F32)]

    sls = [slice(LANES * q, LANES * (q + 1)) for q in range(PAIRS)]
    units = [(q, hm) for q in range(PAIRS) for hm in head_mask]
    rp, kkp, vp = [r_t[:, s] for s in sls], [kk_t[:, s] for s in sls], [v[:, s] for s in sls]
    kb, bb = [k_t[:, s].astype(BF16) for s in sls], [b_t[:, s].astype(BF16) for s in sls]
    vb, kkb = [x.astype(BF16) for x in vp], [x.astype(BF16) for x in kkp]
    kkm = [(kkp[q] * hm).astype(BF16) for q, hm in units]
    rm = [(rp[q] * hm).astype(BF16) for q, hm in units]
    l_b = [_bdot_nt(kkm[u], bb[q]) * strict for u, (q, _) in enumerate(units)]
    l_k = [_bdot_nt(kkm[u], kb[q]) * strict for u, (q, _) in enumerate(units)]
    r_k = [_bdot_nt(rm[u], kb[q]) * incl for u, (q, _) in enumerate(units)]
    r_b = [_bdot_nt(rm[u], bb[q]) * incl for u, (q, _) in enumerate(units)]
    l2 = [_bdot(x, x) for x in l_b]
    l4 = [_bdot(x, x) for x in l2]
    l8 = [_bdot(x, x) for x in l4]
    t_inv = [eye - x for x in l_b]
    for pw in (l2, l4, l8):
        t_inv = [t + _bdot(t, x) for t, x in zip(t_inv, pw)]
    lkv = [_bdot(l_k[u], vb[q]) for u, (q, _) in enumerate(units)]
    w_h = [_bdot(t_inv[u], kkb[q]) for u, (q, _) in enumerate(units)]
    u0_h = [_bdot(t, x) for t, x in zip(t_inv, lkv)]
    q_h = [rp[q] - _bdot(r_b[u], w_h[u]) for u, (q, _) in enumerate(units)]
    y0_h = [_bdot(r_k[u], vb[q]) - _bdot(r_b[u], u0_h[u]) for u, (q, _) in enumerate(units)]
    first = head_mask[0] > 0.5
    pair = lambda xs: [jnp.where(first, xs[2 * q], xs[2 * q + 1]) for q in range(PAIRS)]
    w_c, u0_c, q_c, y0_c = pair(w_h), pair(u0_h), pair(q_h), pair(y0_h)

    st = [st_scr[q] for q in range(PAIRS)]
    ys = [[] for _ in range(PAIRS)]
    for c in range(n // RWKV_CHUNK):
        rs = slice(RWKV_CHUNK * c, RWKV_CHUNK * (c + 1))
        be = [b_e[rs, s] for s in sls]
        n_c = [_bdot_tn(vp[q][rs], k_e[rs, sls[q]]) - _bdot_tn(u0_c[q][rs], be[q]) for q in range(PAIRS)]
        for q in range(PAIRS):
            ys[q].append(_bdot_nt(q_c[q][rs], st[q]) + y0_c[q][rs])
        z = [_bdot_nt(st[q], w_c[q][rs]) for q in range(PAIRS)]
        zb = [_bdot(z[q], be[q]) for q in range(PAIRS)]
        st = [st[q] * dec[RWKV_CHUNK * c:RWKV_CHUNK * c + 1, sls[q]] + (n_c[q] - zb[q]) * bd2 for q in range(PAIRS)]
    for q in range(PAIRS):
        st_scr[q] = st[q]
    y = [x[0] if len(x) == 1 else jnp.concatenate(x, axis=0) for x in ys]
    mean = [_xdot_r(x, seg64, 2) * (1.0 / HEAD_DIM) for x in y]
    d = [x - m for x, m in zip(y, mean)]
    var = [_xdot_r(x * x, seg64, 2) * (1.0 / HEAD_DIM) for x in d]
    for q, s in enumerate(sls):
        y_ref[0, :, s] = d[q] * lax.rsqrt(var[q] + GN_EPS) * gng_ref[:, s] + gnb_ref[:, s] + bonus[:, s]

    @pl.when(tb == pl.num_programs(1) - 1)
    def _():
        sfin_ref[0] = st_scr[...]


def rwkv_branch(pr, past_shift, s0_bd, wl, n):
    B, T, _ = pr.shape
    assert T % n == 0 and n % RWKV_CHUNK == 0
    vec = lambda wd: pl.BlockSpec((1, wd), lambda b, t: (0, 0))
    in_specs = [pl.BlockSpec((1, n, SHIFT_W), lambda b, t: (b, t, 0)),
                pl.BlockSpec((1, 1, SHIFT_W), lambda b, t: (b, 0, 0)),
                pl.BlockSpec((1, PAIRS, LANES, LANES), lambda b, t: (b, 0, 0, 0)),
                vec(SHIFT_W)] + [vec(WIDTH)] * 7 + [
                pl.BlockSpec((LANES, WIDTH), lambda b, t: (0, 0)),
                pl.BlockSpec((LANES, WIDTH), lambda b, t: (0, 0))]
    return pl.pallas_call(
        functools.partial(_rwkv_kernel, n),
        grid=(B, T // n),
        in_specs=in_specs,
        out_specs=[pl.BlockSpec((1, n, WIDTH), lambda b, t: (b, t, 0)),
                   pl.BlockSpec((1, PAIRS, LANES, LANES), lambda b, t: (b, 0, 0, 0))],
        out_shape=[jax.ShapeDtypeStruct((B, T, WIDTH), F32),
                   jax.ShapeDtypeStruct((B, PAIRS, LANES, LANES), F32)],
        scratch_shapes=[pltpu.VMEM((PAIRS, LANES, LANES), F32), pltpu.VMEM((1, SHIFT_W), F32)],
        compiler_params=_params(("parallel", "arbitrary")),
        name="rwkv7",
    )(pr, past_shift, s0_bd, wl["mu"], wl["w0"], wl["a0"], wl["k_k"], wl["k_a"], wl["r_k"],
      wl["gn_g"], wl["gn_b"], wl["w_up"], wl["a_up"])


def _state_to_bd(s):
    B = s.shape[0]
    s = s.reshape(B, PAIRS, 2, HEAD_DIM, HEAD_DIM)
    z = jnp.zeros_like(s[:, :, 0])
    top = jnp.concatenate([s[:, :, 0], z], axis=-1)
    bot = jnp.concatenate([z, s[:, :, 1]], axis=-1)
    return jnp.concatenate([top, bot], axis=-2)


def _bd_to_state(s):
    B = s.shape[0]
    h0 = s[:, :, :HEAD_DIM, :HEAD_DIM]
    h1 = s[:, :, HEAD_DIM:, HEAD_DIM:]
    return jnp.stack([h0, h1], axis=2).reshape(B, HEADS, HEAD_DIM, HEAD_DIM)


def _mla_q_kernel(cq_ref, ckv_ref, kr_ref, c96_ref, s96_ref, c32_ref, s32_ref, gq_ref, wuq_ref, gqn_ref,
                  gkv_ref, gkr_ref, p96_ref, p32_ref, q_ref, ckvn_ref, krope_ref):
    cqb = _rms(cq_ref[0], gq_ref[...]).astype(BF16)
    nope = _iota2((1, QK), 1) < NOPE
    c96, s96 = c96_ref[...], s96_ref[...]
    for h in range(HEADS):
        qh = jnp.dot(cqb, wuq_ref[h], preferred_element_type=F32)
        sq = qh * qh
        ss_n = jnp.sum(jnp.where(nope, sq, 0.0), axis=-1, keepdims=True)
        ss_r = jnp.sum(jnp.where(nope, 0.0, sq), axis=-1, keepdims=True)
        inv = jnp.where(nope, lax.rsqrt(ss_n * (1.0 / NOPE) + RMS_EPS), lax.rsqrt(ss_r * (1.0 / ROPE) + RMS_EPS))
        qn = qh * inv * gqn_ref[...]
        qo = qn * c96 + _xdot_r(qn, p96_ref[...], 3) * s96
        q_ref[0, h] = (qo * (QK ** -0.5 * LOG2E)).astype(BF16)
    ckvn_ref[0] = _rms(ckv_ref[0], gkv_ref[...])
    krn = _rms(kr_ref[0], gkr_ref[...])
    krope_ref[0] = krn * c32_ref[...] + _xdot_r(krn, p32_ref[...], 3) * s32_ref[...]


def mla_q(cq, ckv, kr, tabs, wl, tm):
    B, T, _ = cq.shape
    c96, s96, c32, s32 = tabs
    tok = lambda wd: pl.BlockSpec((1, tm, wd), lambda b, t: (b, t, 0))
    tab = lambda wd: pl.BlockSpec((tm, wd), lambda b, t: (t, 0))
    vec = lambda wd: pl.BlockSpec((1, wd), lambda b, t: (0, 0))
    return pl.pallas_call(
        _mla_q_kernel,
        grid=(B, T // tm),
        in_specs=[tok(Q_LORA), tok(KV_LORA), tok(ROPE), tab(QK), tab(QK), tab(ROPE), tab(ROPE),
                  vec(Q_LORA), pl.BlockSpec((HEADS, Q_LORA, QK), lambda b, t: (0, 0, 0)), vec(QK),
                  vec(KV_LORA), vec(ROPE),
                  pl.BlockSpec((QK, QK), lambda b, t: (0, 0)), pl.BlockSpec((ROPE, ROPE), lambda b, t: (0, 0))],
        out_specs=[pl.BlockSpec((1, HEADS, tm, QK), lambda b, t: (b, 0, t, 0)), tok(KV_LORA), tok(ROPE)],
        out_shape=[jax.ShapeDtypeStruct((B, HEADS, T, QK), BF16),
                   jax.ShapeDtypeStruct((B, T, KV_LORA), F32),
                   jax.ShapeDtypeStruct((B, T, ROPE), F32)],
        compiler_params=_params(("parallel", "parallel")),
        name="mla_q",
    )(cq, ckv, kr, c96, s96, c32, s32, wl["q_norm"], wl["w_uq"], wl["qn"], wl["kv_norm"], wl["kn_rope"],
      wl["p96"], wl["p32"])


def _mla_kv_kernel(ckv_ref, kr_ref, wuk_ref, wuv_ref, gkn_ref, e_ref, k_ref, v_ref):
    cb = ckv_ref[0].astype(BF16)
    k_rope = _xdot_r(kr_ref[0], e_ref[...], 3)
    for h in range(HEADS):
        kh = jnp.dot(cb, wuk_ref[h], preferred_element_type=F32)
        ms = jnp.sum(kh * kh, axis=-1, keepdims=True) * (1.0 / NOPE)
        k_ref[0, h] = (kh * lax.rsqrt(ms + RMS_EPS) * gkn_ref[...] + k_rope).astype(BF16)
    for q in range(PAIRS):
        v_ref[0, q] = jnp.dot(cb, wuv_ref[q], preferred_element_type=F32).astype(BF16)


def mla_kv(ckv_all, kr_all, wl, ts):
    B, S, _ = ckv_all.shape
    return pl.pallas_call(
        _mla_kv_kernel,
        grid=(B, S // ts),
        in_specs=[pl.BlockSpec((1, ts, KV_LORA), lambda b, t: (b, t, 0)),
                  pl.BlockSpec((1, ts, ROPE), lambda b, t: (b, t, 0)),
                  pl.BlockSpec((HEADS, KV_LORA, QK), lambda b, t: (0, 0, 0)),
                  pl.BlockSpec((PAIRS, KV_LORA, LANES), lambda b, t: (0, 0, 0)),
                  pl.BlockSpec((1, QK), lambda b, t: (0, 0)),
                  pl.BlockSpec((ROPE, QK), lambda b, t: (0, 0))],
        out_specs=[pl.BlockSpec((1, HEADS, ts, QK), lambda b, t: (b, 0, t, 0)),
                   pl.BlockSpec((1, PAIRS, ts, LANES), lambda b, t: (b, 0, t, 0))],
        out_shape=[jax.ShapeDtypeStruct((B, HEADS, S, QK), BF16),
                   jax.ShapeDtypeStruct((B, PAIRS, S, LANES), BF16)],
        compiler_params=_params(("parallel", "parallel")),
        name="mla_kv",
    )(ckv_all, kr_all, wl["w_uk"], wl["w_uv"], wl["kn_nope"], wl["e96"])


def _mla_attn_kernel(tq, tk, past, s_len, q_ref, k_ref, v_ref, o_ref, m_scr, l_scr, acc_scr):
    nk = k_ref.shape[2] // tk
    reps = tk // LANES
    q0 = past + pl.program_id(2) * tq
    first = _iota2((1, LANES), 1) < HEAD_DIM
    m_scr[...] = jnp.full(m_scr.shape, NEG_BIG, F32)
    l_scr[...] = jnp.zeros(l_scr.shape, F32)
    acc_scr[...] = jnp.zeros(acc_scr.shape, F32)
    qs = [q_ref[0, h] for h in range(2)]

    def lane_chunks(x):
        return [x[:, LANES * c:LANES * (c + 1)] for c in range(reps)]

    def scores(kb, masked):
        ks = pl.multiple_of(kb * tk, tk)
        s = [_bdot_nt(qs[h], k_ref[0, h, pl.ds(ks, tk), :]) for h in range(2)]
        if masked:
            limit = jnp.minimum((((q0 + _iota2((tq, 1), 0)) >> 6) + 1) << 6, s_len)
            visible = (ks + _iota2((1, tk), 1)) < limit
            s = [jnp.where(visible, x, NEG_BIG) for x in s]
        return ks, s

    def block(kb, masked):
        ks, s = scores(kb, masked)
        vb = v_ref[0, 0, pl.ds(ks, tk), :]
        m_old = [m_scr[h] for h in range(2)]
        m_blk = [functools.reduce(jnp.maximum, lane_chunks(x)) for x in s]
        m_new = [jnp.maximum(m_old[h], jnp.max(m_blk[h], axis=-1, keepdims=True)) for h in range(2)]
        alpha = [jnp.exp2(m_old[h] - m_new[h]) for h in range(2)]
        p = [jnp.exp2(s[h] - jnp.concatenate([m_new[h]] * reps, axis=1)) for h in range(2)]
        for h in range(2):
            l_scr[h] = functools.reduce(jnp.add, lane_chunks(p[h]), alpha[h] * l_scr[h])
            m_scr[h] = m_new[h]
        pv = [_bdot(p[h], vb) for h in range(2)]
        acc_scr[...] = acc_scr[...] * jnp.where(first, alpha[0], alpha[1]) + jnp.where(first, pv[0], pv[1])

    n_full = jnp.minimum((q0 // CHUNK + 1) * CHUNK, s_len) // tk
    last = jnp.minimum((((q0 + tq - 1) // CHUNK) * CHUNK + CHUNK - 1) // tk, nk - 1)

    def run(masked):
        def body(kb, carry):
            block(kb, masked)
            return carry
        return body

    lax.fori_loop(0, n_full, run(False), 0)
    lax.fori_loop(n_full, last + 1, run(True), 0)
    l = [jnp.sum(l_scr[h], axis=-1, keepdims=True) for h in range(2)]
    o_ref[0] = acc_scr[...] / jnp.where(first, l[0], l[1])


def mla_attn(q, k, v, past, s_len, tq, tk):
    B, _, T, _ = q.shape
    S = k.shape[2]
    assert S % tk == 0 and T % tq == 0
    return pl.pallas_call(
        functools.partial(_mla_attn_kernel, tq, tk, past, s_len),
        grid=(B, PAIRS, T // tq),
        in_specs=[pl.BlockSpec((1, 2, tq, QK), lambda b, p, i: (b, p, i, 0)),
                  pl.BlockSpec((1, 2, S, QK), lambda b, p, i: (b, p, 0, 0)),
                  pl.BlockSpec((1, 1, S, LANES), lambda b, p, i: (b, p, 0, 0))],
        out_specs=pl.BlockSpec((1, tq, LANES), lambda b, p, i: (b, i, p)),
        out_shape=jax.ShapeDtypeStruct((B, T, WIDTH), F32),
        scratch_shapes=[pltpu.VMEM((2, tq, LANES), F32), pltpu.VMEM((2, tq, LANES), F32),
                        pltpu.VMEM((tq, LANES), F32)],
        compiler_params=_params(("parallel", "parallel", "arbitrary")),
        name="mla_attn",
    )(q, k, v)


def _sb_kernel(tq, tk, past, q_ref, k_ref, v_ref, sum_ref, o_ref, carry_scr, acc_scr, qm_scr):
    nk = k_ref.shape[1] // tk
    q0 = past + pl.program_id(2) * tq
    first = _iota2((1, LANES), 1) < HEAD_DIM
    carry_scr[...] = jnp.zeros(carry_scr.shape, F32)
    acc_scr[...] = jnp.zeros(acc_scr.shape, F32)
    qv = q_ref[0] * (HEAD_DIM ** -0.5 * LOG2E)
    qm_scr[0] = jnp.where(first, qv, 0.0).astype(BF16)
    qm_scr[1] = jnp.where(first, 0.0, qv).astype(BF16)

    def key_mask(kb):
        return (kb * tk + _iota2((1, tk), 1)) < (q0 + _iota2((tq, 1), 0))

    def stage_z(kb):
        kblk = k_ref[0, pl.ds(pl.multiple_of(kb * tk, tk), tk), :]
        return [_bdot_nt(qm_scr[h], kblk) for h in range(2)]

    def stage_w(z, mask):
        split, tot = [], []
        for x in z:
            neg_abs = lax.bitcast_convert_type(lax.bitcast_convert_type(x, jnp.uint32) | jnp.uint32(0x80000000), F32)
            w = jnp.maximum(x, 0.0) + jnp.log2(1.0 + jnp.exp2(neg_abs))
            if mask is not None:
                w = jnp.where(mask, w, 0.0)
            hi = w.astype(BF16)
            split.append(jnp.concatenate([hi, (w - hi.astype(F32)).astype(BF16)], axis=1))
            tot.append(jnp.sum(w, axis=-1, keepdims=True))
        return split, tot

    def stage_incl(split):
        return [jnp.dot(x, sum_ref[...], preferred_element_type=F32) for x in split]

    def stage_a(z, incl, c, mask):
        cb = [jnp.concatenate([x] * (tk // LANES), axis=1) for x in c]
        a = [jnp.exp2(z[h] - incl[h] - cb[h]) for h in range(2)]
        if mask is not None:
            a = [jnp.where(mask, x, 0.0) for x in a]
        return a

    def stage_pv(a, kb):
        vblk = v_ref[0, pl.ds(pl.multiple_of(kb * tk, tk), tk), :]
        pv = [_bdot(x, vblk) for x in a]
        return jnp.where(first, pv[0], pv[1])

    def one_block(kb, masked):
        mask = key_mask(kb) if masked else None
        z = stage_z(kb)
        split, tot = stage_w(z, mask)
        c = [carry_scr[h] for h in range(2)]
        a = stage_a(z, stage_incl(split), c, mask)
        acc_scr[...] += stage_pv(a, kb)
        for h in range(2):
            carry_scr[h] = c[h] + tot[h]

    def two_blocks(kb):
        z1, z2 = stage_z(kb), stage_z(kb - 1)
        split1, tot1 = stage_w(z1, None)
        incl1 = stage_incl(split1)
        split2, tot2 = stage_w(z2, None)
        incl2 = stage_incl(split2)
        c1 = [carry_scr[h] for h in range(2)]
        pv1 = stage_pv(stage_a(z1, incl1, c1, None), kb)
        c2 = [c1[h] + tot1[h] for h in range(2)]
        pv2 = stage_pv(stage_a(z2, incl2, c2, None), kb - 1)
        acc_scr[...] += pv1 + pv2
        for h in range(2):
            carry_scr[h] = c2[h] + tot2[h]

    n_full = q0 // tk
    last = jnp.clip((q0 + tq - 2) // tk, 0, nk - 1)

    def loop(count, fn):
        def body(i, carry):
            fn(i)
            return carry
        lax.fori_loop(0, count, body, 0)

    loop(last - n_full + 1, lambda i: one_block(last - i, True))
    loop(n_full // 2, lambda i: two_blocks(n_full - 1 - 2 * i))
    loop(n_full % 2, lambda i: one_block(0, False))
    o_ref[0] = acc_scr[...]


def sb_attn(q, k_all, v_all, past, tq, tk):
    B, T, _ = q.shape
    S = k_all.shape[1]
    assert S % tk == 0 and T % tq == 0 and tk % LANES == 0
    incl = np.arange(tk)[:, None] >= np.arange(tk)[None, :]
    sum_mat = jnp.asarray(np.concatenate([incl, incl], axis=0), BF16)
    return pl.pallas_call(
        functools.partial(_sb_kernel, tq, tk, past),
        grid=(B, PAIRS, T // tq),
        in_specs=[pl.BlockSpec((1, tq, LANES), lambda b, p, i: (b, i, p)),
                  pl.BlockSpec((1, S, LANES), lambda b, p, i: (b, 0, p)),
                  pl.BlockSpec((1, S, LANES), lambda b, p, i: (b, 0, p)),
                  pl.BlockSpec((2 * tk, tk), lambda b, p, i: (0, 0))],
        out_specs=pl.BlockSpec((1, tq, LANES), lambda b, p, i: (b, i, p)),
        out_shape=jax.ShapeDtypeStruct((B, T, WIDTH), F32),
        scratch_shapes=[pltpu.VMEM((2, tq, LANES), F32), pltpu.VMEM((tq, LANES), F32),
                        pltpu.VMEM((2, tq, LANES), BF16)],
        compiler_params=_params(("parallel", "parallel", "arbitrary")),
        name="sb_attn",
    )(q, k_all, v_all, sum_mat)


def _merge_kernel(x_ref, gm_ref, ya_ref, za_ref, yb_ref, zb_ref, yc_ref, zc_ref, g_ref,
                  wa_ref, wb_ref, wc_ref, wo_ref, o_ref):
    def branch(y_ref, z_ref, w_ref):
        return _bdot(y_ref[...] * _silu(z_ref[...]), w_ref[...])

    sg = _sigmoid(g_ref[...])
    merged = (sg[:, :D_MODEL] * branch(ya_ref, za_ref, wa_ref)
              + sg[:, D_MODEL:2 * D_MODEL] * branch(yb_ref, zb_ref, wb_ref)
              + sg[:, 2 * D_MODEL:] * branch(yc_ref, zc_ref, wc_ref))
    o_ref[...] = x_ref[...] + gm_ref[0] * _bdot(merged, wo_ref[...])


def merge_out(x2, gate_mod, ya, za, yb, zb, yc, zc, gates, wl, T, tm):
    M, D = x2.shape
    per = T // tm
    row = lambda wd: pl.BlockSpec((tm, wd), lambda i: (i, 0))
    const = lambda r, c: pl.BlockSpec((r, c), lambda i: (0, 0))
    return pl.pallas_call(
        _merge_kernel,
        grid=(M // tm,),
        in_specs=[row(D), pl.BlockSpec((1, 1, D), lambda i: (i // per, 0, 0))] + [row(WIDTH)] * 6 + [row(3 * D),
                  const(WIDTH, D), const(WIDTH, D), const(WIDTH, D), const(D, D)],
        out_specs=row(D),
        out_shape=jax.ShapeDtypeStruct((M, D), F32),
        compiler_params=_params(("parallel",)),
        name="merge_out",
    )(x2, gate_mod, ya, za, yb, zb, yc, zc, gates, wl["w_br_rwkv"], wl["w_br_mla"], wl["w_br_sb"], wl["w_out"])


def _rot_matrix(width, offset):
    half = ROPE // 2
    m = np.zeros((width, width), np.float32)
    for i in range(half):
        m[offset + half + i, offset + i] = -1.0
        m[offset + i, offset + half + i] = 1.0
    return jnp.asarray(m, BF16)


def _prep_layer(P, l):
    row = lambda a: a[l].reshape(1, -1)
    w_in = P["w_in"][l]
    offs = np.cumsum([0] + [wd for _, wd in IN_GROUPS])
    zeros_lora = jnp.zeros((LORA, WIDTH), F32)
    w_uq = P["mla_w_uq"][l].reshape(Q_LORA, HEADS, QK).transpose(1, 0, 2)
    w_ukv = P["mla_w_ukv"][l].reshape(KV_LORA, HEADS, 2 * HEAD_DIM)
    w_uk = jnp.pad(w_ukv[:, :, :NOPE].transpose(1, 0, 2), ((0, 0), (0, 0), (0, ROPE)))
    w_uv = w_ukv[:, :, NOPE:].reshape(KV_LORA, PAIRS, LANES).transpose(1, 0, 2)
    e96 = np.zeros((ROPE, QK), np.float32)
    e96[np.arange(ROPE), NOPE + np.arange(ROPE)] = 1.0
    return dict(
        norm_g=row(P["norm_g"]),
        w_in=[w_in[:, offs[i]:offs[i + 1]].astype(BF16) for i in range(len(IN_GROUPS))],
        mu=row(P["rwkv_mu"]), w0=row(P["rwkv_w0"]), a0=row(P["rwkv_a0"]), k_k=row(P["rwkv_k_k"]),
        k_a=row(P["rwkv_k_a"]), r_k=row(P["rwkv_r_k"]), gn_g=row(P["rwkv_gn_g"]), gn_b=row(P["rwkv_gn_b"]),
        w_up=jnp.concatenate([P["rwkv_w_up"][l], zeros_lora], axis=0).astype(BF16),
        a_up=jnp.concatenate([zeros_lora, P["rwkv_a_up"][l]], axis=0).astype(BF16),
        q_norm=row(P["mla_q_norm"]), w_uq=w_uq.astype(BF16),
        qn=jnp.concatenate([P["mla_qn_nope"][l], P["mla_qn_rope"][l]]).reshape(1, QK),
        kv_norm=row(P["mla_kv_norm"]), kn_rope=row(P["mla_kn_rope"]),
        kn_nope=jnp.pad(P["mla_kn_nope"][l], (0, ROPE)).reshape(1, QK),
        w_uk=w_uk.astype(BF16), w_uv=w_uv.astype(BF16),
        p96=_rot_matrix(QK, NOPE), p32=_rot_matrix(ROPE, 0), e96=jnp.asarray(e96, BF16),
        w_br_rwkv=P["w_br_rwkv"][l].astype(BF16), w_br_mla=P["w_br_mla"][l].astype(BF16),
        w_br_sb=P["w_br_sb"][l].astype(BF16), w_out=P["w_out"][l].astype(BF16),
    )


def _rope_tables(past, T):
    half = ROPE // 2
    inv = ROPE_THETA ** (-jnp.arange(half, dtype=F32) / half)
    ang = (past + jnp.arange(T, dtype=jnp.int32)).astype(F32)[:, None] * inv
    c32 = jnp.tile(jnp.cos(ang), (1, 2))
    s32 = jnp.tile(jnp.sin(ang), (1, 2))
    c96 = jnp.concatenate([jnp.ones((T, NOPE), F32), c32], axis=1)
    s96 = jnp.concatenate([jnp.zeros((T, NOPE), F32), s32], axis=1)
    return c96, s96, c32, s32


def _pad_rows(a, rows):
    return a if a.shape[1] == rows else jnp.pad(a, ((0, 0), (0, rows - a.shape[1])) + ((0, 0),) * (a.ndim - 2))


def _trunk(x, mods, pasts, layers, cfg):
    B, T, D = x.shape
    past = cfg["past"]
    S = past + T
    s_pad = -(-S // cfg["kv_mult"]) * cfg["kv_mult"]
    tabs = _rope_tables(past, T)
    x2 = x.reshape(B * T, D)
    new = []
    for l, wl in enumerate(layers):
        shift, scale, gate = (mods[l][:, None, i * D:(i + 1) * D] for i in range(3))
        outs = in_proj(x2, scale, shift, wl["norm_g"], wl["w_in"], T, cfg["tm"])
        pr, zA, cq, ckv, kr, zB, sq, sk, sv, zC, gates, sk_b, sv_b = outs
        seq = lambda a: a.reshape(B, T, a.shape[-1])
        if pasts is None:
            s0 = jnp.zeros((B, PAIRS, LANES, LANES), F32)
            shift0 = jnp.zeros((B, 1, SHIFT_W), F32)
        else:
            s0, shift0 = _state_to_bd(pasts[l][0]), pasts[l][1]
        pr3 = seq(pr)
        yA, s_fin = rwkv_branch(pr3, shift0, s0, wl, cfg["rwkv_n"])
        q, ckv_n, k_rope = mla_q(seq(cq), seq(ckv), seq(kr), tabs, wl, cfg["tm_q"])
        sk3, sv3 = seq(sk), seq(sv)
        if pasts is None:
            ckv_all, kr_all, k_all, v_all = ckv_n, k_rope, seq(sk_b), seq(sv_b)
        else:
            _, _, ckv0, kr0, k0, v0 = pasts[l]
            ckv_all = jnp.concatenate([ckv0, ckv_n], axis=1)
            kr_all = jnp.concatenate([kr0, k_rope], axis=1)
            k_all = jnp.concatenate([k0.reshape(B, past, WIDTH).astype(BF16), seq(sk_b)], axis=1)
            v_all = jnp.concatenate([v0.reshape(B, past, WIDTH).astype(BF16), seq(sv_b)], axis=1)
        ckv_all, kr_all, k_all, v_all = (_pad_rows(a, s_pad) for a in (ckv_all, kr_all, k_all, v_all))
        kf, vf = mla_kv(ckv_all, kr_all, wl, cfg["ts"])
        yB = mla_attn(q, kf, vf, past, S, cfg["tq"], cfg["tk"])
        yC = sb_attn(seq(sq), k_all, v_all, past, cfg["sb_tq"], cfg["sb_tk"])
        x2 = merge_out(x2, gate, yA.reshape(B * T, WIDTH), zA, yB.reshape(B * T, WIDTH), zB,
                       yC.reshape(B * T, WIDTH), zC, gates, wl, T, cfg["tm"])
        new.append((_bd_to_state(s_fin), pr3[:, -1:], ckv_n, k_rope,
                    sk3.reshape(B, T, HEADS, HEAD_DIM), sv3.reshape(B, T, HEADS, HEAD_DIM)))
    stacked = [jnp.stack([st[i] for st in new], axis=0) for i in range(6)]
    return x2.reshape(B, T, D), stacked


PROMPT_CFG = dict(past=0, tm=256, rwkv_n=128, tm_q=256, ts=256, tq=512, tk=1024, sb_tq=256, sb_tk=256, kv_mult=1024)
SAMPLE_CFG = dict(past=2048, tm=16, rwkv_n=16, tm_q=16, ts=2560, tq=16, tk=2560, sb_tq=16, sb_tk=512, kv_mult=512)


def kernel(x_prompt, x_sample, state_rwkv_wkv, state_rwkv_shift, cache_mla_ckv, cache_mla_krope, cache_sb_k, cache_sb_v, c_prompt, c_sample, w_ada, b_ada, norm_g, w_in, rwkv_mu, rwkv_w0, rwkv_w_up, rwkv_a0, rwkv_a_up, rwkv_k_k, rwkv_k_a, rwkv_r_k, rwkv_gn_g, rwkv_gn_b, w_br_rwkv, mla_q_norm, mla_w_uq, mla_kv_norm, mla_w_ukv, mla_qn_nope, mla_qn_rope, mla_kn_nope, mla_kn_rope, w_br_mla, w_br_sb, w_out):
    P = dict(w_in=w_in, norm_g=norm_g, rwkv_mu=rwkv_mu, rwkv_w0=rwkv_w0, rwkv_w_up=rwkv_w_up, rwkv_a0=rwkv_a0,
             rwkv_a_up=rwkv_a_up, rwkv_k_k=rwkv_k_k, rwkv_k_a=rwkv_k_a, rwkv_r_k=rwkv_r_k, rwkv_gn_g=rwkv_gn_g,
             rwkv_gn_b=rwkv_gn_b, w_br_rwkv=w_br_rwkv, mla_q_norm=mla_q_norm, mla_w_uq=mla_w_uq,
             mla_kv_norm=mla_kv_norm, mla_w_ukv=mla_w_ukv, mla_qn_nope=mla_qn_nope, mla_qn_rope=mla_qn_rope,
             mla_kn_nope=mla_kn_nope, mla_kn_rope=mla_kn_rope, w_br_mla=w_br_mla, w_br_sb=w_br_sb, w_out=w_out)
    depth = w_in.shape[0]
    bp = x_prompt.shape[0]
    layers = [_prep_layer(P, l) for l in range(depth)]
    mods = ada_mod(jnp.concatenate([c_prompt, c_sample], axis=0), w_ada, b_ada)
    y_p, st_p = _trunk(x_prompt, mods[:, :bp], None, layers, PROMPT_CFG)
    pasts = [(state_rwkv_wkv[l], state_rwkv_shift[l], cache_mla_ckv[l], cache_mla_krope[l],
              cache_sb_k[l], cache_sb_v[l]) for l in range(depth)]
    assert cache_mla_ckv.shape[2] == SAMPLE_CFG["past"]
    y_s, st_s = _trunk(x_sample, mods[:, bp:], pasts, layers, SAMPLE_CFG)
    return (y_p, y_s, *st_p, *st_s)
```

```python
import functools

import numpy as np
import jax
import jax.numpy as jnp
from jax import lax
from jax.experimental import pallas as pl
from jax.experimental.pallas import tpu as pltpu

F32 = jnp.float32
BF16 = jnp.bfloat16

D_MODEL = 1024
CHUNK = 64
RMS_EPS = 1e-6
GN_EPS = 64e-5
NEG_BIG = -1e30
ROPE_THETA = 10000.0
LOG2E = 1.4426950408889634

HEADS = 8
HEAD_DIM = 64
WIDTH = HEADS * HEAD_DIM
PAIRS = HEADS // 2
LANES = 128
LORA = 64
SHIFT_W = 3 * WIDTH + 2 * LORA
Q_LORA = 384
KV_LORA = 256
ROPE = 32
NOPE = 64
QK = NOPE + ROPE
RWKV_CHUNK = 64

IN_GROUPS = (("pr", SHIFT_W), ("zA", WIDTH), ("cq", Q_LORA), ("ckv", KV_LORA), ("kr", ROPE),
             ("zB", WIDTH), ("sq", WIDTH), ("sk", WIDTH), ("sv", WIDTH), ("zC", WIDTH),
             ("gates", 3 * D_MODEL))
BF16_COPIES = ("sk", "sv")

VMEM_LIMIT = 56 * 1024 * 1024


def _params(sem):
    return pltpu.CompilerParams(dimension_semantics=sem, vmem_limit_bytes=VMEM_LIMIT)


def _bdot(a, b):
    return jnp.dot(a.astype(BF16), b.astype(BF16), preferred_element_type=F32)


def _bdot_nt(a, b):
    return lax.dot_general(a.astype(BF16), b.astype(BF16), (((1,), (1,)), ((), ())),
                           preferred_element_type=F32)


def _bdot_tn(a, b):
    return lax.dot_general(a.astype(BF16), b.astype(BF16), (((0,), (0,)), ((), ())),
                           preferred_element_type=F32)


def _split(x, n):
    parts, r = [], x
    for _ in range(n):
        p = r.astype(BF16)
        parts.append(p)
        r = r - p.astype(F32)
    return parts


def _xdot_r(x, m01, n):
    m = m01.astype(BF16)
    return sum(jnp.dot(p, m, preferred_element_type=F32) for p in _split(x, n))


def _xdot_l(m01, x, n):
    m = m01.astype(BF16)
    return sum(jnp.dot(m, p, preferred_element_type=F32) for p in _split(x, n))


def _sigmoid(x):
    return 1.0 / (1.0 + jnp.exp(-x))


def _silu(x):
    return x * _sigmoid(x)


def _softplus(x):
    return jnp.maximum(x, 0.0) + jnp.log1p(jnp.exp(-jnp.abs(x)))


def _rms(x, g):
    return x * lax.rsqrt(jnp.mean(x * x, axis=-1, keepdims=True) + RMS_EPS) * g


def _iota2(shape, dim):
    return lax.broadcasted_iota(jnp.int32, shape, dim)


def _ada_kernel(c_ref, w_ref, b_ref, o_ref):
    o_ref[0] = _bdot(_silu(c_ref[...]), w_ref[0]) + b_ref[0]


def ada_mod(c, w_ada, b_ada):
    L, D, N = w_ada.shape
    rows = c.shape[0]
    tn = 1024
    return pl.pallas_call(
        _ada_kernel,
        grid=(L, N // tn),
        in_specs=[pl.BlockSpec((rows, D), lambda l, j: (0, 0)),
                  pl.BlockSpec((1, D, tn), lambda l, j: (l, 0, j)),
                  pl.BlockSpec((1, 1, tn), lambda l, j: (l, 0, j))],
        out_specs=pl.BlockSpec((1, rows, tn), lambda l, j: (l, 0, j)),
        out_shape=jax.ShapeDtypeStruct((L, rows, N), F32),
        compiler_params=_params(("parallel", "parallel")),
        name="ada_mod",
    )(c, w_ada, b_ada.reshape(L, 1, N))


def _inproj_kernel(x_ref, sc_ref, sh_ref, g_ref, *refs):
    n = len(IN_GROUPS)
    ws, outs = refs[:n], refs[n:]
    h = _rms(x_ref[...], g_ref[...]) * (1.0 + sc_ref[0]) + sh_ref[0]
    hb = h.astype(BF16)
    extra = iter(outs[n:])
    for (name, _), w, o in zip(IN_GROUPS, ws, outs):
        res = jnp.dot(hb, w[...], preferred_element_type=F32)
        o[...] = res
        if name in BF16_COPIES:
            next(extra)[...] = res.astype(BF16)


def in_proj(x2, scale, shift, g, ws, T, tm):
    M, D = x2.shape
    assert T % tm == 0 and M % tm == 0
    per = T // tm
    row = lambda i: (i, 0)
    mod = lambda i: (i // per, 0, 0)
    const = lambda i: (0, 0)
    in_specs = [pl.BlockSpec((tm, D), row), pl.BlockSpec((1, 1, D), mod), pl.BlockSpec((1, 1, D), mod),
                pl.BlockSpec((1, D), const)]
    in_specs += [pl.BlockSpec((D, wd), const, pipeline_mode=pl.Buffered(1)) for _, wd in IN_GROUPS]
    out_specs = [pl.BlockSpec((tm, wd), row) for _, wd in IN_GROUPS]
    out_shape = [jax.ShapeDtypeStruct((M, wd), F32) for _, wd in IN_GROUPS]
    out_specs += [pl.BlockSpec((tm, WIDTH), row) for _ in BF16_COPIES]
    out_shape += [jax.ShapeDtypeStruct((M, WIDTH), BF16) for _ in BF16_COPIES]
    return pl.pallas_call(
        _inproj_kernel, grid=(M // tm,), in_specs=in_specs, out_specs=out_specs, out_shape=out_shape,
        compiler_params=_params(("parallel",)), name="in_proj",
    )(x2, scale, shift, g, *ws)


def _rwkv_kernel(n, pr_ref, past_ref, s0_ref, mu_ref, w0_ref, a0_ref, kk_ref, ka_ref, rk_ref,
                 gng_ref, gnb_ref, wup_ref, aup_ref, y_ref, sfin_ref, st_scr, last_scr):
    tb = pl.program_id(1)

    @pl.when(tb == 0)
    def _():
        st_scr[...] = s0_ref[0]
        last_scr[...] = past_ref[0]

    p = pr_ref[0]
    row = _iota2((n, 1), 0)
    prev = jnp.where(row == 0, last_scr[...], pltpu.roll(p, 1, axis=0))
    last_scr[...] = p[n - 1:n, :]
    xs = p + (prev - p) * mu_ref[...]
    r, k, v = xs[:, :WIDTH], xs[:, WIDTH:2 * WIDTH], xs[:, 2 * WIDTH:3 * WIDTH]
    wa = xs[:, 3 * WIDTH:]
    w_log = -_softplus(-(w0_ref[...] + _bdot(jnp.tanh(wa), wup_ref[...]))) - 0.5
    lw = -jnp.exp(w_log)
    a = _sigmoid(a0_ref[...] + _bdot(wa, aup_ref[...]))

    ri, ci = _iota2((LANES, LANES), 0), _iota2((LANES, LANES), 1)
    seg64 = ((ri >> 6) == (ci >> 6)).astype(BF16)
    bd2 = ((ri >> 6) == (ci >> 6)).astype(F32)

    def segsum(t):
        return jnp.concatenate(
            [_xdot_r(t[:, LANES * q:LANES * (q + 1)], seg64, 2) for q in range(PAIRS)], axis=-1)

    kkr = k * kk_ref[...]
    kk = kkr / jnp.maximum(jnp.sqrt(segsum(kkr * kkr)), 1e-12)
    k2 = k * (1.0 + (a - 1.0) * ka_ref[...])
    bonus = segsum(r * k2 * rk_ref[...]) * v

    chunk = min(RWKV_CHUNK, n)
    shift = chunk.bit_length() - 1
    tr, tc = _iota2((n, n), 0), _iota2((n, n), 1)
    same = (tr >> shift) == (tc >> shift)
    strict = (same & (tr > tc)).astype(F32)
    incl = (same & (tr >= tc)).astype(F32)
    eye = (tr == tc).astype(F32)
    lw_parts = _split(lw, 2)
    cum = lambda m01: sum(jnp.dot(m01.astype(BF16), part, preferred_element_type=F32) for part in lw_parts)
    g_log = cum(incl)
    g_end = cum(same)
    e_in, e_inv, e_rest = jnp.exp(g_log), jnp.exp(-g_log), jnp.exp(g_end - g_log)
    r_t = r * e_in
    kk_t = kk * jnp.exp(g_log - lw)
    kka = kk * a
    k_t, b_t = k2 * e_inv, kka * e_inv
    k_e, b_e = k2 * e_rest, kka * e_rest
    dec = jnp.exp(g_end)

    lane = _iota2((1, LANES), 1)
    head_mask = [(lane < HEAD_DIM).astype(F32), (lane >= HEAD_DIM).astype(F32)]

    sls = [slice(LANES * q, LANES * (q + 1)) for q in range(PAIRS)]
    units = [(q, hm) for q in range(PAIRS) for hm in head_mask]
    rp, kkp, vp = [r_t[:, s] for s in sls], [kk_t[:, s] for s in sls], [v[:, s] for s in sls]
    kb, bb = [k_t[:, s].astype(BF16) for s in sls], [b_t[:, s].astype(BF16) for s in sls]
    vb, kkb = [x.astype(BF16) for x in vp], [x.astype(BF16) for x in kkp]
    kkm = [(kkp[q] * hm).astype(BF16) for q, hm in units]
    rm = [(rp[q] * hm).astype(BF16) for q, hm in units]
    l_b = [_bdot_nt(kkm[u], bb[q]) * strict for u, (q, _) in enumerate(units)]
    l_k = [_bdot_nt(kkm[u], kb[q]) * strict for u, (q, _) in enumerate(units)]
    r_k = [_bdot_nt(rm[u], kb[q]) * incl for u, (q, _) in enumerate(units)]
    r_b = [_bdot_nt(rm[u], bb[q]) * incl for u, (q, _) in enumerate(units)]
    sub = min(16, chunk)
    sub_shift = sub.bit_length() - 1
    inner = ((tr >> sub_shift) == (tc >> sub_shift)).astype(F32)
    pw = [x * inner for x in l_b]
    t_inv = [eye - x for x in pw]
    for _ in range(sub_shift - 1):
        pw = [_bdot(x, x) for x in pw]
        t_inv = [t + _bdot(t, x) for t, x in zip(t_inv, pw)]
    for level in range(sub_shift, shift):
        joined = (((tr >> (level + 1)) == (tc >> (level + 1))) & ((tr >> level) != (tc >> level))).astype(F32)
        t_inv = [t - _bdot(_bdot(t, x * joined), t) for t, x in zip(t_inv, l_b)]
    lkv = [_bdot(l_k[u], vb[q]) for u, (q, _) in enumerate(units)]
    w_h = [_bdot(t_inv[u], kkb[q]) for u, (q, _) in enumerate(units)]
    u0_h = [_bdot(t, x) for t, x in zip(t_inv, lkv)]
    q_h = [rp[q] - _bdot(r_b[u], w_h[u]) for u, (q, _) in enumerate(units)]
    y0_h = [_bdot(r_k[u], vb[q]) - _bdot(r_b[u], u0_h[u]) for u, (q, _) in enumerate(units)]
    first = head_mask[0] > 0.5
    pair = lambda xs: [jnp.where(first, xs[2 * q], xs[2 * q + 1]) for q in range(PAIRS)]
    w_c, u0_c, q_c, y0_c = pair(w_h), pair(u0_h), pair(q_h), pair(y0_h)

    st = [st_scr[q] for q in range(PAIRS)]
    ys = [[] for _ in range(PAIRS)]
    rows = [slice(chunk * c, chunk * (c + 1)) for c in range(n // chunk)]
    n_c = [[(_bdot_tn(vp[q][rs], k_e[rs, sls[q]]) - _bdot_tn(u0_c[q][rs], b_e[rs, sls[q]])) * bd2
            for q in range(PAIRS)] for rs in rows]
    a_c = [[_bdot_tn(w_c[q][rs], b_e[rs, sls[q]]) * bd2 for q in range(PAIRS)] for rs in rows]
    for c, rs in enumerate(rows):
        for q in range(PAIRS):
            ys[q].append(_bdot_nt(q_c[q][rs], st[q]) + y0_c[q][rs])
        sa = [_bdot(st[q], a_c[c][q]) for q in range(PAIRS)]
        st = [st[q] * dec[chunk * c:chunk * c + 1, sls[q]] + (n_c[c][q] - sa[q]) for q in range(PAIRS)]
    for q in range(PAIRS):
        st_scr[q] = st[q]
    y = [x[0] if len(x) == 1 else jnp.concatenate(x, axis=0) for x in ys]
    mean = [_xdot_r(x, seg64, 2) * (1.0 / HEAD_DIM) for x in y]
    d = [x - m for x, m in zip(y, mean)]
    var = [_xdot_r(x * x, seg64, 2) * (1.0 / HEAD_DIM) for x in d]
    for q, s in enumerate(sls):
        y_ref[0, :, s] = d[q] * lax.rsqrt(var[q] + GN_EPS) * gng_ref[:, s] + gnb_ref[:, s] + bonus[:, s]

    @pl.when(tb == pl.num_programs(1) - 1)
    def _():
        sfin_ref[0] = st_scr[...]


def rwkv_branch(pr, past_shift, s0_bd, wl, n):
    B, T, _ = pr.shape
    assert T % n == 0 and n % min(RWKV_CHUNK, n) == 0 and n & (n - 1) == 0
    vec = lambda wd: pl.BlockSpec((1, wd), lambda b, t: (0, 0))
    in_specs = [pl.BlockSpec((1, n, SHIFT_W), lambda b, t: (b, t, 0)),
                pl.BlockSpec((1, 1, SHIFT_W), lambda b, t: (b, 0, 0)),
                pl.BlockSpec((1, PAIRS, LANES, LANES), lambda b, t: (b, 0, 0, 0)),
                vec(SHIFT_W)] + [vec(WIDTH)] * 7 + [
                pl.BlockSpec((LANES, WIDTH), lambda b, t: (0, 0)),
                pl.BlockSpec((LANES, WIDTH), lambda b, t: (0, 0))]
    return pl.pallas_call(
        functools.partial(_rwkv_kernel, n),
        grid=(B, T // n),
        in_specs=in_specs,
        out_specs=[pl.BlockSpec((1, n, WIDTH), lambda b, t: (b, t, 0)),
                   pl.BlockSpec((1, PAIRS, LANES, LANES), lambda b, t: (b, 0, 0, 0))],
        out_shape=[jax.ShapeDtypeStruct((B, T, WIDTH), F32),
                   jax.ShapeDtypeStruct((B, PAIRS, LANES, LANES), F32)],
        scratch_shapes=[pltpu.VMEM((PAIRS, LANES, LANES), F32), pltpu.VMEM((1, SHIFT_W), F32)],
        compiler_params=_params(("parallel", "arbitrary")),
        name="rwkv7",
    )(pr, past_shift, s0_bd, wl["mu"], wl["w0"], wl["a0"], wl["k_k"], wl["k_a"], wl["r_k"],
      wl["gn_g"], wl["gn_b"], wl["w_up"], wl["a_up"])


def _state_to_bd(s):
    B = s.shape[0]
    s = s.reshape(B, PAIRS, 2, HEAD_DIM, HEAD_DIM)
    z = jnp.zeros_like(s[:, :, 0])
    top = jnp.concatenate([s[:, :, 0], z], axis=-1)
    bot = jnp.concatenate([z, s[:, :, 1]], axis=-1)
    return jnp.concatenate([top, bot], axis=-2)


def _bd_to_state(s):
    B = s.shape[0]
    h0 = s[:, :, :HEAD_DIM, :HEAD_DIM]
    h1 = s[:, :, HEAD_DIM:, HEAD_DIM:]
    return jnp.stack([h0, h1], axis=2).reshape(B, HEADS, HEAD_DIM, HEAD_DIM)


def _mla_q_kernel(cq_ref, ckv_ref, kr_ref, c96_ref, s96_ref, c32_ref, s32_ref, gq_ref, wuq_ref, gqn_ref,
                  gkv_ref, gkr_ref, p96_ref, p32_ref, q_ref, ckvn_ref, krope_ref):
    cqb = _rms(cq_ref[0], gq_ref[...]).astype(BF16)
    nope = _iota2((1, QK), 1) < NOPE
    c96, s96 = c96_ref[...], s96_ref[...]
    for h in range(HEADS):
        qh = jnp.dot(cqb, wuq_ref[h], preferred_element_type=F32)
        sq = qh * qh
        ss_n = jnp.sum(jnp.where(nope, sq, 0.0), axis=-1, keepdims=True)
        ss_r = jnp.sum(jnp.where(nope, 0.0, sq), axis=-1, keepdims=True)
        inv = jnp.where(nope, lax.rsqrt(ss_n * (1.0 / NOPE) + RMS_EPS), lax.rsqrt(ss_r * (1.0 / ROPE) + RMS_EPS))
        qn = qh * inv * gqn_ref[...]
        qo = qn * c96 + _xdot_r(qn, p96_ref[...], 3) * s96
        q_ref[0, h] = (qo * (QK ** -0.5 * LOG2E)).astype(BF16)
    ckvn_ref[0] = _rms(ckv_ref[0], gkv_ref[...])
    krn = _rms(kr_ref[0], gkr_ref[...])
    krope_ref[0] = krn * c32_ref[...] + _xdot_r(krn, p32_ref[...], 3) * s32_ref[...]


def mla_q(cq, ckv, kr, tabs, wl, tm):
    B, T, _ = cq.shape
    c96, s96, c32, s32 = tabs
    tok = lambda wd: pl.BlockSpec((1, tm, wd), lambda b, t: (b, t, 0))
    tab = lambda wd: pl.BlockSpec((tm, wd), lambda b, t: (t, 0))
    vec = lambda wd: pl.BlockSpec((1, wd), lambda b, t: (0, 0))
    return pl.pallas_call(
        _mla_q_kernel,
        grid=(B, T // tm),
        in_specs=[tok(Q_LORA), tok(KV_LORA), tok(ROPE), tab(QK), tab(QK), tab(ROPE), tab(ROPE),
                  vec(Q_LORA), pl.BlockSpec((HEADS, Q_LORA, QK), lambda b, t: (0, 0, 0)), vec(QK),
                  vec(KV_LORA), vec(ROPE),
                  pl.BlockSpec((QK, QK), lambda b, t: (0, 0)), pl.BlockSpec((ROPE, ROPE), lambda b, t: (0, 0))],
        out_specs=[pl.BlockSpec((1, HEADS, tm, QK), lambda b, t: (b, 0, t, 0)), tok(KV_LORA), tok(ROPE)],
        out_shape=[jax.ShapeDtypeStruct((B, HEADS, T, QK), BF16),
                   jax.ShapeDtypeStruct((B, T, KV_LORA), F32),
                   jax.ShapeDtypeStruct((B, T, ROPE), F32)],
        compiler_params=_params(("parallel", "parallel")),
        name="mla_q",
    )(cq, ckv, kr, c96, s96, c32, s32, wl["q_norm"], wl["w_uq"], wl["qn"], wl["kv_norm"], wl["kn_rope"],
      wl["p96"], wl["p32"])


def _mla_kv_kernel(ckv_ref, kr_ref, wuk_ref, wuv_ref, gkn_ref, e_ref, k_ref, v_ref):
    cb = ckv_ref[0].astype(BF16)
    k_rope = _xdot_r(kr_ref[0], e_ref[...], 3)
    for h in range(HEADS):
        kh = jnp.dot(cb, wuk_ref[h], preferred_element_type=F32)
        ms = jnp.sum(kh * kh, axis=-1, keepdims=True) * (1.0 / NOPE)
        k_ref[0, h] = (kh * lax.rsqrt(ms + RMS_EPS) * gkn_ref[...] + k_rope).astype(BF16)
    for q in range(PAIRS):
        v_ref[0, q] = jnp.dot(cb, wuv_ref[q], preferred_element_type=F32).astype(BF16)


def mla_kv(ckv_all, kr_all, wl, ts):
    B, S, _ = ckv_all.shape
    return pl.pallas_call(
        _mla_kv_kernel,
        grid=(B, S // ts),
        in_specs=[pl.BlockSpec((1, ts, KV_LORA), lambda b, t: (b, t, 0)),
                  pl.BlockSpec((1, ts, ROPE), lambda b, t: (b, t, 0)),
                  pl.BlockSpec((HEADS, KV_LORA, QK), lambda b, t: (0, 0, 0)),
                  pl.BlockSpec((PAIRS, KV_LORA, LANES), lambda b, t: (0, 0, 0)),
                  pl.BlockSpec((1, QK), lambda b, t: (0, 0)),
                  pl.BlockSpec((ROPE, QK), lambda b, t: (0, 0))],
        out_specs=[pl.BlockSpec((1, HEADS, ts, QK), lambda b, t: (b, 0, t, 0)),
                   pl.BlockSpec((1, PAIRS, ts, LANES), lambda b, t: (b, 0, t, 0))],
        out_shape=[jax.ShapeDtypeStruct((B, HEADS, S, QK), BF16),
                   jax.ShapeDtypeStruct((B, PAIRS, S, LANES), BF16)],
        compiler_params=_params(("parallel", "parallel")),
        name="mla_kv",
    )(ckv_all, kr_all, wl["w_uk"], wl["w_uv"], wl["kn_nope"], wl["e96"])


def _mla_attn_kernel(tq, tk, past, s_len, q_ref, k_ref, v_ref, o_ref, m_scr, l_scr, acc_scr):
    nk = k_ref.shape[2] // tk
    reps = tk // LANES
    q0 = past + pl.program_id(2) * tq
    first = _iota2((1, LANES), 1) < HEAD_DIM
    m_scr[...] = jnp.full(m_scr.shape, NEG_BIG, F32)
    l_scr[...] = jnp.zeros(l_scr.shape, F32)
    acc_scr[...] = jnp.zeros(acc_scr.shape, F32)
    qs = [q_ref[0, h] for h in range(2)]

    def lane_chunks(x):
        return [x[:, LANES * c:LANES * (c + 1)] for c in range(reps)]

    def scores(kb, masked):
        ks = pl.multiple_of(kb * tk, tk)
        s = [_bdot_nt(qs[h], k_ref[0, h, pl.ds(ks, tk), :]) for h in range(2)]
        if masked:
            limit = jnp.minimum((((q0 + _iota2((tq, 1), 0)) >> 6) + 1) << 6, s_len)
            visible = (ks + _iota2((1, tk), 1)) < limit
            s = [jnp.where(visible, x, NEG_BIG) for x in s]
        return ks, s

    def block(kb, masked):
        ks, s = scores(kb, masked)
        vb = v_ref[0, 0, pl.ds(ks, tk), :]
        m_old = [m_scr[h] for h in range(2)]
        m_blk = [functools.reduce(jnp.maximum, lane_chunks(x)) for x in s]
        m_new = [jnp.maximum(m_old[h], jnp.max(m_blk[h], axis=-1, keepdims=True)) for h in range(2)]
        alpha = [jnp.exp2(m_old[h] - m_new[h]) for h in range(2)]
        p = [jnp.exp2(s[h] - jnp.concatenate([m_new[h]] * reps, axis=1)) for h in range(2)]
        for h in range(2):
            l_scr[h] = functools.reduce(jnp.add, lane_chunks(p[h]), alpha[h] * l_scr[h])
            m_scr[h] = m_new[h]
        pv = [_bdot(p[h], vb) for h in range(2)]
        acc_scr[...] = acc_scr[...] * jnp.where(first, alpha[0], alpha[1]) + jnp.where(first, pv[0], pv[1])

    n_full = jnp.minimum((q0 // CHUNK + 1) * CHUNK, s_len) // tk
    last = jnp.minimum((((q0 + tq - 1) // CHUNK) * CHUNK + CHUNK - 1) // tk, nk - 1)

    def run(masked):
        def body(kb, carry):
            block(kb, masked)
            return carry
        return body

    lax.fori_loop(0, n_full, run(False), 0)
    lax.fori_loop(n_full, last + 1, run(True), 0)
    l = [jnp.sum(l_scr[h], axis=-1, keepdims=True) for h in range(2)]
    o_ref[0] = acc_scr[...] / jnp.where(first, l[0], l[1])


def mla_attn(q, k, v, past, s_len, tq, tk):
    B, _, T, _ = q.shape
    S = k.shape[2]
    assert S % tk == 0 and T % tq == 0
    return pl.pallas_call(
        functools.partial(_mla_attn_kernel, tq, tk, past, s_len),
        grid=(B, PAIRS, T // tq),
        in_specs=[pl.BlockSpec((1, 2, tq, QK), lambda b, p, i: (b, p, i, 0)),
                  pl.BlockSpec((1, 2, S, QK), lambda b, p, i: (b, p, 0, 0)),
                  pl.BlockSpec((1, 1, S, LANES), lambda b, p, i: (b, p, 0, 0))],
        out_specs=pl.BlockSpec((1, tq, LANES), lambda b, p, i: (b, i, p)),
        out_shape=jax.ShapeDtypeStruct((B, T, WIDTH), F32),
        scratch_shapes=[pltpu.VMEM((2, tq, LANES), F32), pltpu.VMEM((2, tq, LANES), F32),
                        pltpu.VMEM((tq, LANES), F32)],
        compiler_params=_params(("parallel", "parallel", "arbitrary")),
        name="mla_attn",
    )(q, k, v)


def _sb_kernel(tq, tk, past, q_ref, k_ref, v_ref, sum_ref, o_ref, carry_scr, acc_scr, qm_scr, z_scr):
    nk = k_ref.shape[1] // tk
    q0 = past + pl.program_id(2) * tq
    first = _iota2((1, LANES), 1) < HEAD_DIM
    carry_scr[...] = jnp.zeros(carry_scr.shape, F32)
    acc_scr[...] = jnp.zeros(acc_scr.shape, F32)
    qv = q_ref[0] * (HEAD_DIM ** -0.5 * LOG2E)
    qm_scr[0] = jnp.where(first, qv, 0.0).astype(BF16)
    qm_scr[1] = jnp.where(first, 0.0, qv).astype(BF16)

    def key_mask(kb):
        return (kb * tk + _iota2((1, tk), 1)) < (q0 + _iota2((tq, 1), 0))

    def stage_z(kb):
        kblk = k_ref[0, pl.ds(pl.multiple_of(kb * tk, tk), tk), :]
        return [_bdot_nt(qm_scr[h], kblk) for h in range(2)]

    def stage_w(z, mask):
        split, tot = [], []
        for read in z:
            x = read()
            neg_abs = lax.bitcast_convert_type(lax.bitcast_convert_type(x, jnp.uint32) | jnp.uint32(0x80000000), F32)
            w = jnp.maximum(x, 0.0) + jnp.log2(1.0 + jnp.exp2(neg_abs))
            if mask is not None:
                w = jnp.where(mask, w, 0.0)
            hi = w.astype(BF16)
            split.append(jnp.concatenate([hi, (w - hi.astype(F32)).astype(BF16)], axis=1))
            tot.append(jnp.sum(w, axis=-1, keepdims=True))
        return split, tot

    def stage_incl(split):
        return [jnp.dot(x, sum_ref[...], preferred_element_type=F32) for x in split]

    def stage_a(z, incl, c, mask):
        cb = [jnp.concatenate([x] * (tk // LANES), axis=1) for x in c]
        a = [jnp.exp2(z[h]() - incl[h] - cb[h]) for h in range(2)]
        if mask is not None:
            a = [jnp.where(mask, x, 0.0) for x in a]
        return a

    def stage_pv(a, kb):
        vblk = v_ref[0, pl.ds(pl.multiple_of(kb * tk, tk), tk), :]
        pv = [_bdot(x, vblk) for x in a]
        return jnp.where(first, pv[0], pv[1])

    def one_block(kb, masked):
        mask = key_mask(kb) if masked else None
        z = [lambda v=v: v for v in stage_z(kb)]
        split, tot = stage_w(z, mask)
        c = [carry_scr[h] for h in range(2)]
        a = stage_a(z, stage_incl(split), c, mask)
        acc_scr[...] += stage_pv(a, kb)
        for h in range(2):
            carry_scr[h] = c[h] + tot[h]

    def prefetch_z(kb):
        for j in range(2):
            z = stage_z(jnp.maximum(kb - j, 0))
            for h in range(2):
                z_scr[j, h] = z[h]

    def two_blocks(i):
        kb = n_full - 1 - 2 * i
        z1, z2 = ([lambda v=z_scr[j, h]: v for h in range(2)] for j in range(2))
        split1, tot1 = stage_w(z1, None)
        incl1 = stage_incl(split1)
        split2, tot2 = stage_w(z2, None)
        incl2 = stage_incl(split2)
        prefetch_z(kb - 2)
        c1 = [carry_scr[h] for h in range(2)]
        pv1 = stage_pv(stage_a(z1, incl1, c1, None), kb)
        c2 = [c1[h] + tot1[h] for h in range(2)]
        pv2 = stage_pv(stage_a(z2, incl2, c2, None), kb - 1)
        acc_scr[...] += pv1 + pv2
        for h in range(2):
            carry_scr[h] = c2[h] + tot2[h]

    n_full = q0 // tk
    last = jnp.clip((q0 + tq - 2) // tk, 0, nk - 1)

    def loop(count, fn):
        def body(i, carry):
            fn(i)
            return carry
        lax.fori_loop(0, count, body, 0)

    loop(last - n_full + 1, lambda i: one_block(last - i, True))
    prefetch_z(n_full - 1)
    loop(n_full // 2, two_blocks)
    loop(n_full % 2, lambda i: one_block(0, False))
    o_ref[0] = acc_scr[...]


def sb_attn(q, k_all, v_all, past, tq, tk):
    B, T, _ = q.shape
    S = k_all.shape[1]
    assert S % tk == 0 and T % tq == 0 and tk % LANES == 0
    incl = np.arange(tk)[:, None] >= np.arange(tk)[None, :]
    sum_mat = jnp.asarray(np.concatenate([incl, incl], axis=0), BF16)
    return pl.pallas_call(
        functools.partial(_sb_kernel, tq, tk, past),
        grid=(B, PAIRS, T // tq),
        in_specs=[pl.BlockSpec((1, tq, LANES), lambda b, p, i: (b, i, p)),
                  pl.BlockSpec((1, S, LANES), lambda b, p, i: (b, 0, p)),
                  pl.BlockSpec((1, S, LANES), lambda b, p, i: (b, 0, p)),
                  pl.BlockSpec((2 * tk, tk), lambda b, p, i: (0, 0))],
        out_specs=pl.BlockSpec((1, tq, LANES), lambda b, p, i: (b, i, p)),
        out_shape=jax.ShapeDtypeStruct((B, T, WIDTH), F32),
        scratch_shapes=[pltpu.VMEM((2, tq, LANES), F32), pltpu.VMEM((tq, LANES), F32),
                        pltpu.VMEM((2, tq, LANES), BF16), pltpu.VMEM((2, 2, tq, tk), F32)],
        compiler_params=_params(("parallel", "parallel", "arbitrary")),
        name="sb_attn",
    )(q, k_all, v_all, sum_mat)


def _merge_kernel(x_ref, gm_ref, ya_ref, za_ref, yb_ref, zb_ref, yc_ref, zc_ref, g_ref,
                  wa_ref, wb_ref, wc_ref, wo_ref, o_ref):
    def branch(y_ref, z_ref, w_ref):
        return _bdot(y_ref[...] * _silu(z_ref[...]), w_ref[...])

    sg = _sigmoid(g_ref[...])
    merged = (sg[:, :D_MODEL] * branch(ya_ref, za_ref, wa_ref)
              + sg[:, D_MODEL:2 * D_MODEL] * branch(yb_ref, zb_ref, wb_ref)
              + sg[:, 2 * D_MODEL:] * branch(yc_ref, zc_ref, wc_ref))
    o_ref[...] = x_ref[...] + gm_ref[0] * _bdot(merged, wo_ref[...])


def merge_out(x2, gate_mod, ya, za, yb, zb, yc, zc, gates, wl, T, tm):
    M, D = x2.shape
    per = T // tm
    row = lambda wd: pl.BlockSpec((tm, wd), lambda i: (i, 0))
    const = lambda r, c: pl.BlockSpec((r, c), lambda i: (0, 0))
    return pl.pallas_call(
        _merge_kernel,
        grid=(M // tm,),
        in_specs=[row(D), pl.BlockSpec((1, 1, D), lambda i: (i // per, 0, 0))] + [row(WIDTH)] * 6 + [row(3 * D),
                  const(WIDTH, D), const(WIDTH, D), const(WIDTH, D), const(D, D)],
        out_specs=row(D),
        out_shape=jax.ShapeDtypeStruct((M, D), F32),
        compiler_params=_params(("parallel",)),
        name="merge_out",
    )(x2, gate_mod, ya, za, yb, zb, yc, zc, gates, wl["w_br_rwkv"], wl["w_br_mla"], wl["w_br_sb"], wl["w_out"])


def _rot_matrix(width, offset):
    half = ROPE // 2
    m = np.zeros((width, width), np.float32)
    for i in range(half):
        m[offset + half + i, offset + i] = -1.0
        m[offset + i, offset + half + i] = 1.0
    return jnp.asarray(m, BF16)


def _prep_layer(P, l):
    row = lambda a: a[l].reshape(1, -1)
    w_in = P["w_in"][l]
    offs = np.cumsum([0] + [wd for _, wd in IN_GROUPS])
    zeros_lora = jnp.zeros((LORA, WIDTH), F32)
    w_uq = P["mla_w_uq"][l].reshape(Q_LORA, HEADS, QK).transpose(1, 0, 2)
    w_ukv = P["mla_w_ukv"][l].reshape(KV_LORA, HEADS, 2 * HEAD_DIM)
    w_uk = jnp.pad(w_ukv[:, :, :NOPE].transpose(1, 0, 2), ((0, 0), (0, 0), (0, ROPE)))
    w_uv = w_ukv[:, :, NOPE:].reshape(KV_LORA, PAIRS, LANES).transpose(1, 0, 2)
    e96 = np.zeros((ROPE, QK), np.float32)
    e96[np.arange(ROPE), NOPE + np.arange(ROPE)] = 1.0
    return dict(
        norm_g=row(P["norm_g"]),
        w_in=[w_in[:, offs[i]:offs[i + 1]].astype(BF16) for i in range(len(IN_GROUPS))],
        mu=row(P["rwkv_mu"]), w0=row(P["rwkv_w0"]), a0=row(P["rwkv_a0"]), k_k=row(P["rwkv_k_k"]),
        k_a=row(P["rwkv_k_a"]), r_k=row(P["rwkv_r_k"]), gn_g=row(P["rwkv_gn_g"]), gn_b=row(P["rwkv_gn_b"]),
        w_up=jnp.concatenate([P["rwkv_w_up"][l], zeros_lora], axis=0).astype(BF16),
        a_up=jnp.concatenate([zeros_lora, P["rwkv_a_up"][l]], axis=0).astype(BF16),
        q_norm=row(P["mla_q_norm"]), w_uq=w_uq.astype(BF16),
        qn=jnp.concatenate([P["mla_qn_nope"][l], P["mla_qn_rope"][l]]).reshape(1, QK),
        kv_norm=row(P["mla_kv_norm"]), kn_rope=row(P["mla_kn_rope"]),
        kn_nope=jnp.pad(P["mla_kn_nope"][l], (0, ROPE)).reshape(1, QK),
        w_uk=w_uk.astype(BF16), w_uv=w_uv.astype(BF16),
        p96=_rot_matrix(QK, NOPE), p32=_rot_matrix(ROPE, 0), e96=jnp.asarray(e96, BF16),
        w_br_rwkv=P["w_br_rwkv"][l].astype(BF16), w_br_mla=P["w_br_mla"][l].astype(BF16),
        w_br_sb=P["w_br_sb"][l].astype(BF16), w_out=P["w_out"][l].astype(BF16),
    )


def _rope_tables(past, T):
    half = ROPE // 2
    inv = ROPE_THETA ** (-jnp.arange(half, dtype=F32) / half)
    ang = (past + jnp.arange(T, dtype=jnp.int32)).astype(F32)[:, None] * inv
    c32 = jnp.tile(jnp.cos(ang), (1, 2))
    s32 = jnp.tile(jnp.sin(ang), (1, 2))
    c96 = jnp.concatenate([jnp.ones((T, NOPE), F32), c32], axis=1)
    s96 = jnp.concatenate([jnp.zeros((T, NOPE), F32), s32], axis=1)
    return c96, s96, c32, s32


def _pad_rows(a, rows):
    return a if a.shape[1] == rows else jnp.pad(a, ((0, 0), (0, rows - a.shape[1])) + ((0, 0),) * (a.ndim - 2))


def _trunk(x, mods, pasts, layers, cfg):
    B, T, D = x.shape
    past = cfg["past"]
    S = past + T
    s_pad = -(-S // cfg["kv_mult"]) * cfg["kv_mult"]
    tabs = _rope_tables(past, T)
    x2 = x.reshape(B * T, D)
    new = []
    for l, wl in enumerate(layers):
        shift, scale, gate = (mods[l][:, None, i * D:(i + 1) * D] for i in range(3))
        outs = in_proj(x2, scale, shift, wl["norm_g"], wl["w_in"], T, cfg["tm"])
        pr, zA, cq, ckv, kr, zB, sq, sk, sv, zC, gates, sk_b, sv_b = outs
        seq = lambda a: a.reshape(B, T, a.shape[-1])
        if pasts is None:
            s0 = jnp.zeros((B, PAIRS, LANES, LANES), F32)
            shift0 = jnp.zeros((B, 1, SHIFT_W), F32)
        else:
            s0, shift0 = _state_to_bd(pasts[l][0]), pasts[l][1]
        pr3 = seq(pr)
        yA, s_fin = rwkv_branch(pr3, shift0, s0, wl, cfg["rwkv_n"])
        q, ckv_n, k_rope = mla_q(seq(cq), seq(ckv), seq(kr), tabs, wl, cfg["tm_q"])
        sk3, sv3 = seq(sk), seq(sv)
        if pasts is None:
            ckv_all, kr_all, k_all, v_all = ckv_n, k_rope, seq(sk_b), seq(sv_b)
        else:
            _, _, ckv0, kr0, k0, v0 = pasts[l]
            ckv_all = jnp.concatenate([ckv0, ckv_n], axis=1)
            kr_all = jnp.concatenate([kr0, k_rope], axis=1)
            k_all = jnp.concatenate([k0.reshape(B, past, WIDTH).astype(BF16), seq(sk_b)], axis=1)
            v_all = jnp.concatenate([v0.reshape(B, past, WIDTH).astype(BF16), seq(sv_b)], axis=1)
        ckv_all, kr_all, k_all, v_all = (_pad_rows(a, s_pad) for a in (ckv_all, kr_all, k_all, v_all))
        kf, vf = mla_kv(ckv_all, kr_all, wl, cfg["ts"])
        yB = mla_attn(q, kf, vf, past, S, cfg["tq"], cfg["tk"])
        yC = sb_attn(seq(sq), k_all, v_all, past, cfg["sb_tq"], cfg["sb_tk"])
        x2 = merge_out(x2, gate, yA.reshape(B * T, WIDTH), zA, yB.reshape(B * T, WIDTH), zB,
                       yC.reshape(B * T, WIDTH), zC, gates, wl, T, cfg["tm"])
        new.append((_bd_to_state(s_fin), pr3[:, -1:], ckv_n, k_rope,
                    sk3.reshape(B, T, HEADS, HEAD_DIM), sv3.reshape(B, T, HEADS, HEAD_DIM)))
    stacked = [jnp.stack([st[i] for st in new], axis=0) for i in range(6)]
    return x2.reshape(B, T, D), stacked


PROMPT_CFG = dict(past=0, tm=256, rwkv_n=128, tm_q=256, ts=256, tq=512, tk=1024, sb_tq=256, sb_tk=256, kv_mult=1024)
SAMPLE_CFG = dict(past=2048, tm=16, rwkv_n=16, tm_q=16, ts=2560, tq=16, tk=2560, sb_tq=16, sb_tk=512, kv_mult=512)


def kernel(x_prompt, x_sample, state_rwkv_wkv, state_rwkv_shift, cache_mla_ckv, cache_mla_krope, cache_sb_k, cache_sb_v, c_prompt, c_sample, w_ada, b_ada, norm_g, w_in, rwkv_mu, rwkv_w0, rwkv_w_up, rwkv_a0, rwkv_a_up, rwkv_k_k, rwkv_k_a, rwkv_r_k, rwkv_gn_g, rwkv_gn_b, w_br_rwkv, mla_q_norm, mla_w_uq, mla_kv_norm, mla_w_ukv, mla_qn_nope, mla_qn_rope, mla_kn_nope, mla_kn_rope, w_br_mla, w_br_sb, w_out):
    P = dict(w_in=w_in, norm_g=norm_g, rwkv_mu=rwkv_mu, rwkv_w0=rwkv_w0, rwkv_w_up=rwkv_w_up, rwkv_a0=rwkv_a0,
             rwkv_a_up=rwkv_a_up, rwkv_k_k=rwkv_k_k, rwkv_k_a=rwkv_k_a, rwkv_r_k=rwkv_r_k, rwkv_gn_g=rwkv_gn_g,
             rwkv_gn_b=rwkv_gn_b, w_br_rwkv=w_br_rwkv, mla_q_norm=mla_q_norm, mla_w_uq=mla_w_uq,
             mla_kv_norm=mla_kv_norm, mla_w_ukv=mla_w_ukv, mla_qn_nope=mla_qn_nope, mla_qn_rope=mla_qn_rope,
             mla_kn_nope=mla_kn_nope, mla_kn_rope=mla_kn_rope, w_br_mla=w_br_mla, w_br_sb=w_br_sb, w_out=w_out)
    depth = w_in.shape[0]
    bp = x_prompt.shape[0]
    layers = [_prep_layer(P, l) for l in range(depth)]
    mods = ada_mod(jnp.concatenate([c_prompt, c_sample], axis=0), w_ada, b_ada)
    y_p, st_p = _trunk(x_prompt, mods[:, :bp], None, layers, PROMPT_CFG)
    pasts = [(state_rwkv_wkv[l], state_rwkv_shift[l], cache_mla_ckv[l], cache_mla_krope[l],
              cache_sb_k[l], cache_sb_v[l]) for l in range(depth)]
    assert cache_mla_ckv.shape[2] == SAMPLE_CFG["past"]
    y_s, st_s = _trunk(x_sample, mods[:, bp:], pasts, layers, SAMPLE_CFG)
    return (y_p, y_s, *st_p, *st_s)
```

```python
import functools

import numpy as np
import jax
import jax.numpy as jnp
from jax import lax
from jax.experimental import pallas as pl
from jax.experimental.pallas import tpu as pltpu

F32 = jnp.float32
BF16 = jnp.bfloat16

D_MODEL = 1024
CHUNK = 64
RMS_EPS = 1e-6
GN_EPS = 64e-5
NEG_BIG = -1e30
ROPE_THETA = 10000.0
LOG2E = 1.4426950408889634

HEADS = 8
HEAD_DIM = 64
WIDTH = HEADS * HEAD_DIM
PAIRS = HEADS // 2
LANES = 128
LORA = 64
SHIFT_W = 3 * WIDTH + 2 * LORA
Q_LORA = 384
KV_LORA = 256
ROPE = 32
NOPE = 64
QK = NOPE + ROPE
RWKV_CHUNK = 64

IN_GROUPS = (("pr", SHIFT_W), ("zA", WIDTH), ("cq", Q_LORA), ("ckv", KV_LORA), ("kr", ROPE),
             ("zB", WIDTH), ("sq", WIDTH), ("sk", WIDTH), ("sv", WIDTH), ("zC", WIDTH),
             ("gates", 3 * D_MODEL))
BF16_COPIES = ("sk", "sv")
BF16_GROUPS = ("zA", "zB", "zC", "gates")

VMEM_LIMIT = 56 * 1024 * 1024


def _params(sem):
    return pltpu.CompilerParams(dimension_semantics=sem, vmem_limit_bytes=VMEM_LIMIT)


def _bdot(a, b):
    return jnp.dot(a.astype(BF16), b.astype(BF16), preferred_element_type=F32)


def _bdot_nt(a, b):
    return lax.dot_general(a.astype(BF16), b.astype(BF16), (((1,), (1,)), ((), ())),
                           preferred_element_type=F32)


def _bdot_tn(a, b):
    return lax.dot_general(a.astype(BF16), b.astype(BF16), (((0,), (0,)), ((), ())),
                           preferred_element_type=F32)


def _split(x, n):
    parts, r = [], x
    for _ in range(n):
        p = r.astype(BF16)
        parts.append(p)
        r = r - p.astype(F32)
    return parts


def _xdot_r(x, m01, n):
    m = m01.astype(BF16)
    return sum(jnp.dot(p, m, preferred_element_type=F32) for p in _split(x, n))


def _xdot_l(m01, x, n):
    m = m01.astype(BF16)
    return sum(jnp.dot(m, p, preferred_element_type=F32) for p in _split(x, n))


def _sigmoid(x):
    return 1.0 / (1.0 + jnp.exp(-x))


def _silu(x):
    return x * _sigmoid(x)


def _softplus(x):
    return jnp.maximum(x, 0.0) + jnp.log(1.0 + jnp.exp(-jnp.abs(x)))


def _rms(x, g):
    return x * lax.rsqrt(jnp.mean(x * x, axis=-1, keepdims=True) + RMS_EPS) * g


def _iota2(shape, dim):
    return lax.broadcasted_iota(jnp.int32, shape, dim)


def _ada_kernel(c_ref, w_ref, b_ref, o_ref):
    o_ref[0] = _bdot(_silu(c_ref[...]), w_ref[0]) + b_ref[0]


def ada_mod(c, w_ada, b_ada):
    L, D, N = w_ada.shape
    rows = c.shape[0]
    tn = 1024
    return pl.pallas_call(
        _ada_kernel,
        grid=(L, N // tn),
        in_specs=[pl.BlockSpec((rows, D), lambda l, j: (0, 0)),
                  pl.BlockSpec((1, D, tn), lambda l, j: (l, 0, j)),
                  pl.BlockSpec((1, 1, tn), lambda l, j: (l, 0, j))],
        out_specs=pl.BlockSpec((1, rows, tn), lambda l, j: (l, 0, j)),
        out_shape=jax.ShapeDtypeStruct((L, rows, N), F32),
        compiler_params=_params(("parallel", "parallel")),
        name="ada_mod",
    )(c, w_ada, b_ada.reshape(L, 1, N))


def _inproj_kernel(x_ref, sc_ref, sh_ref, g_ref, *refs):
    n = len(IN_GROUPS)
    ws, outs = refs[:n], refs[n:]
    h = _rms(x_ref[...], g_ref[...]) * (1.0 + sc_ref[0]) + sh_ref[0]
    hb = h.astype(BF16)
    extra = iter(outs[n:])
    for (name, _), w, o in zip(IN_GROUPS, ws, outs):
        res = jnp.dot(hb, w[...], preferred_element_type=F32)
        o[...] = res.astype(o.dtype)
        if name in BF16_COPIES:
            next(extra)[...] = res.astype(BF16)


def in_proj(x2, scale, shift, g, ws, T, tm):
    M, D = x2.shape
    assert T % tm == 0 and M % tm == 0
    per = T // tm
    row = lambda i: (i, 0)
    mod = lambda i: (i // per, 0, 0)
    const = lambda i: (0, 0)
    in_specs = [pl.BlockSpec((tm, D), row), pl.BlockSpec((1, 1, D), mod), pl.BlockSpec((1, 1, D), mod),
                pl.BlockSpec((1, D), const)]
    in_specs += [pl.BlockSpec((D, wd), const, pipeline_mode=pl.Buffered(1)) for _, wd in IN_GROUPS]
    out_specs = [pl.BlockSpec((tm, wd), row) for _, wd in IN_GROUPS]
    out_shape = [jax.ShapeDtypeStruct((M, wd), BF16 if name in BF16_GROUPS else F32) for name, wd in IN_GROUPS]
    out_specs += [pl.BlockSpec((tm, WIDTH), row) for _ in BF16_COPIES]
    out_shape += [jax.ShapeDtypeStruct((M, WIDTH), BF16) for _ in BF16_COPIES]
    return pl.pallas_call(
        _inproj_kernel, grid=(M // tm,), in_specs=in_specs, out_specs=out_specs, out_shape=out_shape,
        compiler_params=_params(("parallel",)), name="in_proj",
    )(x2, scale, shift, g, *ws)


def _rwkv_kernel(n, pr_ref, past_ref, s0_ref, mu_ref, w0_ref, a0_ref, kk_ref, ka_ref, rk_ref,
                 gng_ref, gnb_ref, wup_ref, aup_ref, y_ref, sfin_ref, st_scr, last_scr):
    tb = pl.program_id(1)

    @pl.when(tb == 0)
    def _():
        st_scr[...] = s0_ref[0]
        last_scr[...] = past_ref[0]

    p = pr_ref[0]
    row = _iota2((n, 1), 0)
    prev = jnp.where(row == 0, last_scr[...], pltpu.roll(p, 1, axis=0))
    last_scr[...] = p[n - 1:n, :]
    xs = p + (prev - p) * mu_ref[...]
    r, k, v = xs[:, :WIDTH], xs[:, WIDTH:2 * WIDTH], xs[:, 2 * WIDTH:3 * WIDTH]
    wa = xs[:, 3 * WIDTH:]
    w_log = -_softplus(-(w0_ref[...] + _bdot(jnp.tanh(wa), wup_ref[...]))) - 0.5
    lw = -jnp.exp(w_log)
    a = _sigmoid(a0_ref[...] + _bdot(wa, aup_ref[...]))

    ri, ci = _iota2((LANES, LANES), 0), _iota2((LANES, LANES), 1)
    seg64 = ((ri >> 6) == (ci >> 6)).astype(BF16)
    bd2 = ((ri >> 6) == (ci >> 6)).astype(F32)

    def segsum(t):
        return jnp.concatenate(
            [_xdot_r(t[:, LANES * q:LANES * (q + 1)], seg64, 2) for q in range(PAIRS)], axis=-1)

    kkr = k * kk_ref[...]
    kk = kkr * jnp.minimum(lax.rsqrt(segsum(kkr * kkr)), 1e12)
    k2 = k * (1.0 + (a - 1.0) * ka_ref[...])
    bonus = segsum(r * k2 * rk_ref[...]) * v

    chunk = min(RWKV_CHUNK, n)
    shift = chunk.bit_length() - 1
    tr, tc = _iota2((n, n), 0), _iota2((n, n), 1)
    same = (tr >> shift) == (tc >> shift)
    strict = (same & (tr > tc)).astype(F32)
    incl = (same & (tr >= tc)).astype(F32)
    eye = (tr == tc).astype(F32)
    lw_parts = _split(lw, 2)
    cum = lambda m01: sum(jnp.dot(m01.astype(BF16), part, preferred_element_type=F32) for part in lw_parts)
    g_log = cum(incl)
    g_end = cum(same)
    e_in, e_inv, e_rest = jnp.exp(g_log), jnp.exp(-g_log), jnp.exp(g_end - g_log)
    r_t = r * e_in
    kk_t = kk * jnp.exp(g_log - lw)
    kka = kk * a
    k_t, b_t = k2 * e_inv, kka * e_inv
    k_e, b_e = k2 * e_rest, kka * e_rest
    dec = jnp.exp(g_end)

    lane = _iota2((1, LANES), 1)
    head_mask = [(lane < HEAD_DIM).astype(F32), (lane >= HEAD_DIM).astype(F32)]

    sls = [slice(LANES * q, LANES * (q + 1)) for q in range(PAIRS)]
    units = [(q, hm) for q in range(PAIRS) for hm in head_mask]
    rp, kkp, vp = [r_t[:, s] for s in sls], [kk_t[:, s] for s in sls], [v[:, s] for s in sls]
    kb, bb = [k_t[:, s].astype(BF16) for s in sls], [b_t[:, s].astype(BF16) for s in sls]
    vb, kkb = [x.astype(BF16) for x in vp], [x.astype(BF16) for x in kkp]
    kkm = [(kkp[q] * hm).astype(BF16) for q, hm in units]
    rm = [(rp[q] * hm).astype(BF16) for q, hm in units]
    l_b = [_bdot_nt(kkm[u], bb[q]) * strict for u, (q, _) in enumerate(units)]
    l_k = [_bdot_nt(kkm[u], kb[q]) * strict for u, (q, _) in enumerate(units)]
    r_k = [_bdot_nt(rm[u], kb[q]) * incl for u, (q, _) in enumerate(units)]
    r_b = [_bdot_nt(rm[u], bb[q]) * incl for u, (q, _) in enumerate(units)]
    sub = min(16, chunk)
    sub_shift = sub.bit_length() - 1
    inner = ((tr >> sub_shift) == (tc >> sub_shift)).astype(F32)
    pw = [x * inner for x in l_b]
    t_inv = [eye - x for x in pw]
    for _ in range(sub_shift - 1):
        pw = [_bdot(x, x) for x in pw]
        t_inv = [t + _bdot(t, x) for t, x in zip(t_inv, pw)]
    for level in range(sub_shift, shift):
        joined = (((tr >> (level + 1)) == (tc >> (level + 1))) & ((tr >> level) != (tc >> level))).astype(F32)
        t_inv = [t - _bdot(_bdot(t, x * joined), t) for t, x in zip(t_inv, l_b)]
    lkv = [_bdot(l_k[u], vb[q]) for u, (q, _) in enumerate(units)]
    w_h = [_bdot(t_inv[u], kkb[q]) for u, (q, _) in enumerate(units)]
    u0_h = [_bdot(t, x) for t, x in zip(t_inv, lkv)]
    q_h = [rp[q] - _bdot(r_b[u], w_h[u]) for u, (q, _) in enumerate(units)]
    y0_h = [_bdot(r_k[u], vb[q]) - _bdot(r_b[u], u0_h[u]) for u, (q, _) in enumerate(units)]
    first = head_mask[0] > 0.5
    pair = lambda xs: [jnp.where(first, xs[2 * q], xs[2 * q + 1]) for q in range(PAIRS)]
    w_c, u0_c, q_c, y0_c = pair(w_h), pair(u0_h), pair(q_h), pair(y0_h)

    st = [st_scr[q] for q in range(PAIRS)]
    ys = [[] for _ in range(PAIRS)]
    rows = [slice(chunk * c, chunk * (c + 1)) for c in range(n // chunk)]
    n_c = [[(_bdot_tn(vp[q][rs], k_e[rs, sls[q]]) - _bdot_tn(u0_c[q][rs], b_e[rs, sls[q]])) * bd2
            for q in range(PAIRS)] for rs in rows]
    a_c = [[_bdot_tn(w_c[q][rs], b_e[rs, sls[q]]) * bd2 for q in range(PAIRS)] for rs in rows]
    for c, rs in enumerate(rows):
        for q in range(PAIRS):
            ys[q].append(_bdot_nt(q_c[q][rs], st[q]) + y0_c[q][rs])
        sa = [_bdot(st[q], a_c[c][q]) for q in range(PAIRS)]
        st = [st[q] * dec[chunk * c:chunk * c + 1, sls[q]] + (n_c[c][q] - sa[q]) for q in range(PAIRS)]
    for q in range(PAIRS):
        st_scr[q] = st[q]
    y = [x[0] if len(x) == 1 else jnp.concatenate(x, axis=0) for x in ys]
    mean = [_xdot_r(x, seg64, 2) * (1.0 / HEAD_DIM) for x in y]
    d = [x - m for x, m in zip(y, mean)]
    var = [_xdot_r(x * x, seg64, 2) * (1.0 / HEAD_DIM) for x in d]
    for q, s in enumerate(sls):
        yn = d[q] * lax.rsqrt(var[q] + GN_EPS) * gng_ref[:, s] + gnb_ref[:, s] + bonus[:, s]
        y_ref[0, :, s] = yn.astype(y_ref.dtype)

    @pl.when(tb == pl.num_programs(1) - 1)
    def _():
        sfin_ref[0] = st_scr[...]


def rwkv_branch(pr, past_shift, s0_bd, wl, n):
    B, T, _ = pr.shape
    assert T % n == 0 and n % min(RWKV_CHUNK, n) == 0 and n & (n - 1) == 0
    vec = lambda wd: pl.BlockSpec((1, wd), lambda b, t: (0, 0))
    in_specs = [pl.BlockSpec((1, n, SHIFT_W), lambda b, t: (b, t, 0)),
                pl.BlockSpec((1, 1, SHIFT_W), lambda b, t: (b, 0, 0)),
                pl.BlockSpec((1, PAIRS, LANES, LANES), lambda b, t: (b, 0, 0, 0)),
                vec(SHIFT_W)] + [vec(WIDTH)] * 7 + [
                pl.BlockSpec((LANES, WIDTH), lambda b, t: (0, 0)),
                pl.BlockSpec((LANES, WIDTH), lambda b, t: (0, 0))]
    return pl.pallas_call(
        functools.partial(_rwkv_kernel, n),
        grid=(B, T // n),
        in_specs=in_specs,
        out_specs=[pl.BlockSpec((1, n, WIDTH), lambda b, t: (b, t, 0)),
                   pl.BlockSpec((1, PAIRS, LANES, LANES), lambda b, t: (b, 0, 0, 0))],
        out_shape=[jax.ShapeDtypeStruct((B, T, WIDTH), BF16),
                   jax.ShapeDtypeStruct((B, PAIRS, LANES, LANES), F32)],
        scratch_shapes=[pltpu.VMEM((PAIRS, LANES, LANES), F32), pltpu.VMEM((1, SHIFT_W), F32)],
        compiler_params=_params(("parallel", "arbitrary")),
        name="rwkv7",
    )(pr, past_shift, s0_bd, wl["mu"], wl["w0"], wl["a0"], wl["k_k"], wl["k_a"], wl["r_k"],
      wl["gn_g"], wl["gn_b"], wl["w_up"], wl["a_up"])


def _state_to_bd(s):
    B = s.shape[0]
    s = s.reshape(B, PAIRS, 2, HEAD_DIM, HEAD_DIM)
    z = jnp.zeros_like(s[:, :, 0])
    top = jnp.concatenate([s[:, :, 0], z], axis=-1)
    bot = jnp.concatenate([z, s[:, :, 1]], axis=-1)
    return jnp.concatenate([top, bot], axis=-2)


def _bd_to_state(s):
    B = s.shape[0]
    h0 = s[:, :, :HEAD_DIM, :HEAD_DIM]
    h1 = s[:, :, HEAD_DIM:, HEAD_DIM:]
    return jnp.stack([h0, h1], axis=2).reshape(B, HEADS, HEAD_DIM, HEAD_DIM)


def _mla_q_kernel(cq_ref, ckv_ref, kr_ref, c96_ref, s96_ref, c32_ref, s32_ref, gq_ref, wuq_ref, gqn_ref,
                  gkv_ref, gkr_ref, p96_ref, p32_ref, q_ref, ckvn_ref, krope_ref):
    cqb = _rms(cq_ref[0], gq_ref[...]).astype(BF16)
    nope = _iota2((1, QK), 1) < NOPE
    c96, s96 = c96_ref[...], s96_ref[...]
    for h in range(HEADS):
        qh = jnp.dot(cqb, wuq_ref[h], preferred_element_type=F32)
        sq = qh * qh
        ss_n = jnp.sum(jnp.where(nope, sq, 0.0), axis=-1, keepdims=True)
        ss_r = jnp.sum(jnp.where(nope, 0.0, sq), axis=-1, keepdims=True)
        inv = jnp.where(nope, lax.rsqrt(ss_n * (1.0 / NOPE) + RMS_EPS), lax.rsqrt(ss_r * (1.0 / ROPE) + RMS_EPS))
        qn = qh * inv * gqn_ref[...]
        qo = qn * c96 + _xdot_r(qn, p96_ref[...], 3) * s96
        q_ref[0, h] = (qo * (QK ** -0.5 * LOG2E)).astype(BF16)
    ckvn_ref[0] = _rms(ckv_ref[0], gkv_ref[...])
    krn = _rms(kr_ref[0], gkr_ref[...])
    krope_ref[0] = krn * c32_ref[...] + _xdot_r(krn, p32_ref[...], 3) * s32_ref[...]


def mla_q(cq, ckv, kr, tabs, wl, tm):
    B, T, _ = cq.shape
    c96, s96, c32, s32 = tabs
    tok = lambda wd: pl.BlockSpec((1, tm, wd), lambda b, t: (b, t, 0))
    tab = lambda wd: pl.BlockSpec((tm, wd), lambda b, t: (t, 0))
    vec = lambda wd: pl.BlockSpec((1, wd), lambda b, t: (0, 0))
    return pl.pallas_call(
        _mla_q_kernel,
        grid=(B, T // tm),
        in_specs=[tok(Q_LORA), tok(KV_LORA), tok(ROPE), tab(QK), tab(QK), tab(ROPE), tab(ROPE),
                  vec(Q_LORA), pl.BlockSpec((HEADS, Q_LORA, QK), lambda b, t: (0, 0, 0)), vec(QK),
                  vec(KV_LORA), vec(ROPE),
                  pl.BlockSpec((QK, QK), lambda b, t: (0, 0)), pl.BlockSpec((ROPE, ROPE), lambda b, t: (0, 0))],
        out_specs=[pl.BlockSpec((1, HEADS, tm, QK), lambda b, t: (b, 0, t, 0)), tok(KV_LORA), tok(ROPE)],
        out_shape=[jax.ShapeDtypeStruct((B, HEADS, T, QK), BF16),
                   jax.ShapeDtypeStruct((B, T, KV_LORA), F32),
                   jax.ShapeDtypeStruct((B, T, ROPE), F32)],
        compiler_params=_params(("parallel", "parallel")),
        name="mla_q",
    )(cq, ckv, kr, c96, s96, c32, s32, wl["q_norm"], wl["w_uq"], wl["qn"], wl["kv_norm"], wl["kn_rope"],
      wl["p96"], wl["p32"])


def _mla_kv_kernel(ckv_ref, kr_ref, wuk_ref, wuv_ref, gkn_ref, e_ref, k_ref, v_ref):
    cb = ckv_ref[0].astype(BF16)
    k_rope = _xdot_r(kr_ref[0], e_ref[...], 3)
    for h in range(HEADS):
        kh = jnp.dot(cb, wuk_ref[h], preferred_element_type=F32)
        ms = jnp.sum(kh * kh, axis=-1, keepdims=True) * (1.0 / NOPE)
        k_ref[0, h] = (kh * lax.rsqrt(ms + RMS_EPS) * gkn_ref[...] + k_rope).astype(BF16)
    for q in range(PAIRS):
        v_ref[0, q] = jnp.dot(cb, wuv_ref[q], preferred_element_type=F32).astype(BF16)


def mla_kv(ckv_all, kr_all, wl, ts):
    B, S, _ = ckv_all.shape
    return pl.pallas_call(
        _mla_kv_kernel,
        grid=(B, S // ts),
        in_specs=[pl.BlockSpec((1, ts, KV_LORA), lambda b, t: (b, t, 0)),
                  pl.BlockSpec((1, ts, ROPE), lambda b, t: (b, t, 0)),
                  pl.BlockSpec((HEADS, KV_LORA, QK), lambda b, t: (0, 0, 0)),
                  pl.BlockSpec((PAIRS, KV_LORA, LANES), lambda b, t: (0, 0, 0)),
                  pl.BlockSpec((1, QK), lambda b, t: (0, 0)),
                  pl.BlockSpec((ROPE, QK), lambda b, t: (0, 0))],
        out_specs=[pl.BlockSpec((1, HEADS, ts, QK), lambda b, t: (b, 0, t, 0)),
                   pl.BlockSpec((1, PAIRS, ts, LANES), lambda b, t: (b, 0, t, 0))],
        out_shape=[jax.ShapeDtypeStruct((B, HEADS, S, QK), BF16),
                   jax.ShapeDtypeStruct((B, PAIRS, S, LANES), BF16)],
        compiler_params=_params(("parallel", "parallel")),
        name="mla_kv",
    )(ckv_all, kr_all, wl["w_uk"], wl["w_uv"], wl["kn_nope"], wl["e96"])


def _mla_attn_kernel(tq, tk, past, s_len, q_ref, k_ref, v_ref, o_ref, m_scr, l_scr, acc_scr):
    nk = k_ref.shape[2] // tk
    reps = tk // LANES
    q0 = past + pl.program_id(2) * tq
    first = _iota2((1, LANES), 1) < HEAD_DIM
    m_scr[...] = jnp.full(m_scr.shape, NEG_BIG, F32)
    l_scr[...] = jnp.zeros(l_scr.shape, F32)
    acc_scr[...] = jnp.zeros(acc_scr.shape, F32)
    qs = [q_ref[0, h] for h in range(2)]

    def lane_chunks(x):
        return [x[:, LANES * c:LANES * (c + 1)] for c in range(reps)]

    def scores(kb, masked):
        ks = pl.multiple_of(kb * tk, tk)
        s = [_bdot_nt(qs[h], k_ref[0, h, pl.ds(ks, tk), :]) for h in range(2)]
        if masked:
            limit = jnp.minimum((((q0 + _iota2((tq, 1), 0)) >> 6) + 1) << 6, s_len)
            visible = (ks + _iota2((1, tk), 1)) < limit
            s = [jnp.where(visible, x, NEG_BIG) for x in s]
        return ks, s

    def block(kb, masked):
        ks, s = scores(kb, masked)
        vb = v_ref[0, 0, pl.ds(ks, tk), :]
        m_old = [m_scr[h] for h in range(2)]
        m_blk = [functools.reduce(jnp.maximum, lane_chunks(x)) for x in s]
        m_new = [jnp.maximum(m_old[h], jnp.max(m_blk[h], axis=-1, keepdims=True)) for h in range(2)]
        alpha = [jnp.exp2(m_old[h] - m_new[h]) for h in range(2)]
        p = [jnp.exp2(s[h] - jnp.concatenate([m_new[h]] * reps, axis=1)) for h in range(2)]
        for h in range(2):
            l_scr[h] = functools.reduce(jnp.add, lane_chunks(p[h]), alpha[h] * l_scr[h])
            m_scr[h] = m_new[h]
        pv = [_bdot(p[h], vb) for h in range(2)]
        acc_scr[...] = acc_scr[...] * jnp.where(first, alpha[0], alpha[1]) + jnp.where(first, pv[0], pv[1])

    n_full = jnp.minimum((q0 // CHUNK + 1) * CHUNK, s_len) // tk
    last = jnp.minimum((((q0 + tq - 1) // CHUNK) * CHUNK + CHUNK - 1) // tk, nk - 1)

    def run(masked):
        def body(kb, carry):
            block(kb, masked)
            return carry
        return body

    lax.fori_loop(0, n_full, run(False), 0)
    lax.fori_loop(n_full, last + 1, run(True), 0)
    l = [jnp.sum(l_scr[h], axis=-1, keepdims=True) for h in range(2)]
    o_ref[0] = (acc_scr[...] / jnp.where(first, l[0], l[1])).astype(o_ref.dtype)


def mla_attn(q, k, v, past, s_len, tq, tk):
    B, _, T, _ = q.shape
    S = k.shape[2]
    assert S % tk == 0 and T % tq == 0
    return pl.pallas_call(
        functools.partial(_mla_attn_kernel, tq, tk, past, s_len),
        grid=(B, PAIRS, T // tq),
        in_specs=[pl.BlockSpec((1, 2, tq, QK), lambda b, p, i: (b, p, i, 0)),
                  pl.BlockSpec((1, 2, S, QK), lambda b, p, i: (b, p, 0, 0)),
                  pl.BlockSpec((1, 1, S, LANES), lambda b, p, i: (b, p, 0, 0))],
        out_specs=pl.BlockSpec((1, tq, LANES), lambda b, p, i: (b, i, p)),
        out_shape=jax.ShapeDtypeStruct((B, T, WIDTH), BF16),
        scratch_shapes=[pltpu.VMEM((2, tq, LANES), F32), pltpu.VMEM((2, tq, LANES), F32),
                        pltpu.VMEM((tq, LANES), F32)],
        compiler_params=_params(("parallel", "parallel", "arbitrary")),
        name="mla_attn",
    )(q, k, v)


def _sb_kernel(tq, tk, past, q_ref, k_ref, v_ref, sum_ref, o_ref, carry_scr, acc_scr, qm_scr, z_scr):
    nk = k_ref.shape[1] // tk
    q0 = past + pl.program_id(2) * tq
    first = _iota2((1, LANES), 1) < HEAD_DIM
    carry_scr[...] = jnp.zeros(carry_scr.shape, F32)
    acc_scr[...] = jnp.zeros(acc_scr.shape, F32)
    qv = q_ref[0] * (HEAD_DIM ** -0.5 * LOG2E)
    qm_scr[0] = jnp.where(first, qv, 0.0).astype(BF16)
    qm_scr[1] = jnp.where(first, 0.0, qv).astype(BF16)

    def key_mask(kb):
        return (kb * tk + _iota2((1, tk), 1)) < (q0 + _iota2((tq, 1), 0))

    def stage_z(kb):
        kblk = k_ref[0, pl.ds(pl.multiple_of(kb * tk, tk), tk), :]
        return [_bdot_nt(qm_scr[h], kblk) for h in range(2)]

    def stage_w(z, mask):
        split, tot = [], []
        for read in z:
            x = read()
            neg_abs = lax.bitcast_convert_type(lax.bitcast_convert_type(x, jnp.uint32) | jnp.uint32(0x80000000), F32)
            w = jnp.maximum(x, 0.0) + jnp.log2(1.0 + jnp.exp2(neg_abs))
            if mask is not None:
                w = jnp.where(mask, w, 0.0)
            hi = w.astype(BF16)
            split.append(jnp.concatenate([hi, (w - hi.astype(F32)).astype(BF16)], axis=1))
            tot.append(jnp.sum(w, axis=-1, keepdims=True))
        return split, tot

    def stage_incl(split):
        return [jnp.dot(x, sum_ref[...], preferred_element_type=F32) for x in split]

    def stage_a(z, incl, c, mask):
        cb = [jnp.concatenate([x] * (tk // LANES), axis=1) for x in c]
        a = [jnp.exp2(z[h]() - incl[h] - cb[h]) for h in range(2)]
        if mask is not None:
            a = [jnp.where(mask, x, 0.0) for x in a]
        return a

    def stage_pv(a, kb):
        vblk = v_ref[0, pl.ds(pl.multiple_of(kb * tk, tk), tk), :]
        pv = [_bdot(x, vblk) for x in a]
        return jnp.where(first, pv[0], pv[1])

    def one_block(kb, masked):
        mask = key_mask(kb) if masked else None
        z = [lambda v=v: v for v in stage_z(kb)]
        split, tot = stage_w(z, mask)
        c = [carry_scr[h] for h in range(2)]
        a = stage_a(z, stage_incl(split), c, mask)
        acc_scr[...] += stage_pv(a, kb)
        for h in range(2):
            carry_scr[h] = c[h] + tot[h]

    def prefetch_z(kb):
        for j in range(2):
            z = stage_z(jnp.maximum(kb - j, 0))
            for h in range(2):
                z_scr[j, h] = z[h]

    def two_blocks(kb):
        z1, z2 = ([lambda v=z_scr[j, h]: v for h in range(2)] for j in range(2))
        split1, tot1 = stage_w(z1, None)
        incl1 = stage_incl(split1)
        split2, tot2 = stage_w(z2, None)
        incl2 = stage_incl(split2)
        prefetch_z(kb - 2)
        c1 = [carry_scr[h] for h in range(2)]
        pv1 = stage_pv(stage_a(z1, incl1, c1, None), kb)
        c2 = [c1[h] + tot1[h] for h in range(2)]
        pv2 = stage_pv(stage_a(z2, incl2, c2, None), kb - 1)
        acc_scr[...] += pv1 + pv2
        for h in range(2):
            carry_scr[h] = c2[h] + tot2[h]

    n_full = q0 // tk
    last = jnp.clip((q0 + tq - 2) // tk, 0, nk - 1)

    def loop(count, fn):
        def body(i, carry):
            fn(i)
            return carry
        lax.fori_loop(0, count, body, 0)

    loop(last - n_full + 1, lambda i: one_block(last - i, True))
    prefetch_z(n_full - 1)
    loop(n_full // 2, lambda i: two_blocks(n_full - 1 - 2 * i))
    loop(n_full % 2, lambda i: one_block(0, False))
    o_ref[0] = acc_scr[...].astype(o_ref.dtype)


def sb_attn(q, k_all, v_all, past, tq, tk):
    B, T, _ = q.shape
    S = k_all.shape[1]
    assert S % tk == 0 and T % tq == 0 and tk % LANES == 0
    incl = np.arange(tk)[:, None] >= np.arange(tk)[None, :]
    sum_mat = jnp.asarray(np.concatenate([incl, incl], axis=0), BF16)
    return pl.pallas_call(
        functools.partial(_sb_kernel, tq, tk, past),
        grid=(B, PAIRS, T // tq),
        in_specs=[pl.BlockSpec((1, tq, LANES), lambda b, p, i: (b, i, p)),
                  pl.BlockSpec((1, S, LANES), lambda b, p, i: (b, 0, p)),
                  pl.BlockSpec((1, S, LANES), lambda b, p, i: (b, 0, p)),
                  pl.BlockSpec((2 * tk, tk), lambda b, p, i: (0, 0))],
        out_specs=pl.BlockSpec((1, tq, LANES), lambda b, p, i: (b, i, p)),
        out_shape=jax.ShapeDtypeStruct((B, T, WIDTH), BF16),
        scratch_shapes=[pltpu.VMEM((2, tq, LANES), F32), pltpu.VMEM((tq, LANES), F32),
                        pltpu.VMEM((2, tq, LANES), BF16), pltpu.VMEM((2, 2, tq, tk), F32)],
        compiler_params=_params(("parallel", "parallel", "arbitrary")),
        name="sb_attn",
    )(q, k_all, v_all, sum_mat)


def _merge_kernel(x_ref, gm_ref, ya_ref, za_ref, yb_ref, zb_ref, yc_ref, zc_ref, g_ref,
                  wa_ref, wb_ref, wc_ref, wo_ref, o_ref):
    def branch(y_ref, z_ref, w_ref):
        return _bdot(y_ref[...].astype(F32) * _silu(z_ref[...].astype(F32)), w_ref[...])

    sg = _sigmoid(g_ref[...].astype(F32))
    merged = (sg[:, :D_MODEL] * branch(ya_ref, za_ref, wa_ref)
              + sg[:, D_MODEL:2 * D_MODEL] * branch(yb_ref, zb_ref, wb_ref)
              + sg[:, 2 * D_MODEL:] * branch(yc_ref, zc_ref, wc_ref))
    o_ref[...] = x_ref[...] + gm_ref[0] * _bdot(merged, wo_ref[...])


def merge_out(x2, gate_mod, ya, za, yb, zb, yc, zc, gates, wl, T, tm):
    M, D = x2.shape
    per = T // tm
    row = lambda wd: pl.BlockSpec((tm, wd), lambda i: (i, 0))
    const = lambda r, c: pl.BlockSpec((r, c), lambda i: (0, 0))
    return pl.pallas_call(
        _merge_kernel,
        grid=(M // tm,),
        in_specs=[row(D), pl.BlockSpec((1, 1, D), lambda i: (i // per, 0, 0))] + [row(WIDTH)] * 6 + [row(3 * D),
                  const(WIDTH, D), const(WIDTH, D), const(WIDTH, D), const(D, D)],
        out_specs=row(D),
        out_shape=jax.ShapeDtypeStruct((M, D), F32),
        compiler_params=_params(("parallel",)),
        name="merge_out",
    )(x2, gate_mod, ya, za, yb, zb, yc, zc, gates, wl["w_br_rwkv"], wl["w_br_mla"], wl["w_br_sb"], wl["w_out"])


def _rot_matrix(width, offset):
    half = ROPE // 2
    m = np.zeros((width, width), np.float32)
    for i in range(half):
        m[offset + half + i, offset + i] = -1.0
        m[offset + i, offset + half + i] = 1.0
    return jnp.asarray(m, BF16)


def _prep_layer(P, l):
    row = lambda a: a[l].reshape(1, -1)
    w_in = P["w_in"][l]
    offs = np.cumsum([0] + [wd for _, wd in IN_GROUPS])
    zeros_lora = jnp.zeros((LORA, WIDTH), F32)
    w_uq = P["mla_w_uq"][l].reshape(Q_LORA, HEADS, QK).transpose(1, 0, 2)
    w_ukv = P["mla_w_ukv"][l].reshape(KV_LORA, HEADS, 2 * HEAD_DIM)
    w_uk = jnp.pad(w_ukv[:, :, :NOPE].transpose(1, 0, 2), ((0, 0), (0, 0), (0, ROPE)))
    w_uv = w_ukv[:, :, NOPE:].reshape(KV_LORA, PAIRS, LANES).transpose(1, 0, 2)
    e96 = np.zeros((ROPE, QK), np.float32)
    e96[np.arange(ROPE), NOPE + np.arange(ROPE)] = 1.0
    return dict(
        norm_g=row(P["norm_g"]),
        w_in=[w_in[:, offs[i]:offs[i + 1]].astype(BF16) for i in range(len(IN_GROUPS))],
        mu=row(P["rwkv_mu"]), w0=row(P["rwkv_w0"]), a0=row(P["rwkv_a0"]), k_k=row(P["rwkv_k_k"]),
        k_a=row(P["rwkv_k_a"]), r_k=row(P["rwkv_r_k"]), gn_g=row(P["rwkv_gn_g"]), gn_b=row(P["rwkv_gn_b"]),
        w_up=jnp.concatenate([P["rwkv_w_up"][l], zeros_lora], axis=0).astype(BF16),
        a_up=jnp.concatenate([zeros_lora, P["rwkv_a_up"][l]], axis=0).astype(BF16),
        q_norm=row(P["mla_q_norm"]), w_uq=w_uq.astype(BF16),
        qn=jnp.concatenate([P["mla_qn_nope"][l], P["mla_qn_rope"][l]]).reshape(1, QK),
        kv_norm=row(P["mla_kv_norm"]), kn_rope=row(P["mla_kn_rope"]),
        kn_nope=jnp.pad(P["mla_kn_nope"][l], (0, ROPE)).reshape(1, QK),
        w_uk=w_uk.astype(BF16), w_uv=w_uv.astype(BF16),
        p96=_rot_matrix(QK, NOPE), p32=_rot_matrix(ROPE, 0), e96=jnp.asarray(e96, BF16),
        w_br_rwkv=P["w_br_rwkv"][l].astype(BF16), w_br_mla=P["w_br_mla"][l].astype(BF16),
        w_br_sb=P["w_br_sb"][l].astype(BF16), w_out=P["w_out"][l].astype(BF16),
    )


def _rope_tables(past, T):
    half = ROPE // 2
    inv = ROPE_THETA ** (-jnp.arange(half, dtype=F32) / half)
    ang = (past + jnp.arange(T, dtype=jnp.int32)).astype(F32)[:, None] * inv
    c32 = jnp.tile(jnp.cos(ang), (1, 2))
    s32 = jnp.tile(jnp.sin(ang), (1, 2))
    c96 = jnp.concatenate([jnp.ones((T, NOPE), F32), c32], axis=1)
    s96 = jnp.concatenate([jnp.zeros((T, NOPE), F32), s32], axis=1)
    return c96, s96, c32, s32


def _pad_rows(a, rows):
    return a if a.shape[1] == rows else jnp.pad(a, ((0, 0), (0, rows - a.shape[1])) + ((0, 0),) * (a.ndim - 2))


def _trunk(x, mods, pasts, layers, cfg):
    B, T, D = x.shape
    past = cfg["past"]
    S = past + T
    s_pad = -(-S // cfg["kv_mult"]) * cfg["kv_mult"]
    tabs = _rope_tables(past, T)
    x2 = x.reshape(B * T, D)
    new = []
    for l, wl in enumerate(layers):
        shift, scale, gate = (mods[l][:, None, i * D:(i + 1) * D] for i in range(3))
        outs = in_proj(x2, scale, shift, wl["norm_g"], wl["w_in"], T, cfg["tm"])
        pr, zA, cq, ckv, kr, zB, sq, sk, sv, zC, gates, sk_b, sv_b = outs
        seq = lambda a: a.reshape(B, T, a.shape[-1])
        if pasts is None:
            s0 = jnp.zeros((B, PAIRS, LANES, LANES), F32)
            shift0 = jnp.zeros((B, 1, SHIFT_W), F32)
        else:
            s0, shift0 = _state_to_bd(pasts[l][0]), pasts[l][1]
        pr3 = seq(pr)
        yA, s_fin = rwkv_branch(pr3, shift0, s0, wl, cfg["rwkv_n"])
        q, ckv_n, k_rope = mla_q(seq(cq), seq(ckv), seq(kr), tabs, wl, cfg["tm_q"])
        sk3, sv3 = seq(sk), seq(sv)
        if pasts is None:
            ckv_all, kr_all, k_all, v_all = ckv_n, k_rope, seq(sk_b), seq(sv_b)
        else:
            _, _, ckv0, kr0, k0, v0 = pasts[l]
            ckv_all = jnp.concatenate([ckv0, ckv_n], axis=1)
            kr_all = jnp.concatenate([kr0, k_rope], axis=1)
            k_all = jnp.concatenate([k0.reshape(B, past, WIDTH).astype(BF16), seq(sk_b)], axis=1)
            v_all = jnp.concatenate([v0.reshape(B, past, WIDTH).astype(BF16), seq(sv_b)], axis=1)
        ckv_all, kr_all, k_all, v_all = (_pad_rows(a, s_pad) for a in (ckv_all, kr_all, k_all, v_all))
        kf, vf = mla_kv(ckv_all, kr_all, wl, cfg["ts"])
        yB = mla_attn(q, kf, vf, past, S, cfg["tq"], cfg["tk"])
        yC = sb_attn(seq(sq), k_all, v_all, past, cfg["sb_tq"], cfg["sb_tk"])
        x2 = merge_out(x2, gate, yA.reshape(B * T, WIDTH), zA, yB.reshape(B * T, WIDTH), zB,
                       yC.reshape(B * T, WIDTH), zC, gates, wl, T, cfg["tm"])
        new.append((_bd_to_state(s_fin), pr3[:, -1:], ckv_n, k_rope,
                    sk3.reshape(B, T, HEADS, HEAD_DIM), sv3.reshape(B, T, HEADS, HEAD_DIM)))
    stacked = [jnp.stack([st[i] for st in new], axis=0) for i in range(6)]
    return x2.reshape(B, T, D), stacked


PROMPT_CFG = dict(past=0, tm=256, rwkv_n=128, tm_q=256, ts=256, tq=512, tk=1024, sb_tq=256, sb_tk=256, kv_mult=1024)
SAMPLE_CFG = dict(past=2048, tm=16, rwkv_n=16, tm_q=16, ts=2560, tq=16, tk=2560, sb_tq=16, sb_tk=512, kv_mult=512)


def kernel(x_prompt, x_sample, state_rwkv_wkv, state_rwkv_shift, cache_mla_ckv, cache_mla_krope, cache_sb_k, cache_sb_v, c_prompt, c_sample, w_ada, b_ada, norm_g, w_in, rwkv_mu, rwkv_w0, rwkv_w_up, rwkv_a0, rwkv_a_up, rwkv_k_k, rwkv_k_a, rwkv_r_k, rwkv_gn_g, rwkv_gn_b, w_br_rwkv, mla_q_norm, mla_w_uq, mla_kv_norm, mla_w_ukv, mla_qn_nope, mla_qn_rope, mla_kn_nope, mla_kn_rope, w_br_mla, w_br_sb, w_out):
    P = dict(w_in=w_in, norm_g=norm_g, rwkv_mu=rwkv_mu, rwkv_w0=rwkv_w0, rwkv_w_up=rwkv_w_up, rwkv_a0=rwkv_a0,
             rwkv_a_up=rwkv_a_up, rwkv_k_k=rwkv_k_k, rwkv_k_a=rwkv_k_a, rwkv_r_k=rwkv_r_k, rwkv_gn_g=rwkv_gn_g,
             rwkv_gn_b=rwkv_gn_b, w_br_rwkv=w_br_rwkv, mla_q_norm=mla_q_norm, mla_w_uq=mla_w_uq,
             mla_kv_norm=mla_kv_norm, mla_w_ukv=mla_w_ukv, mla_qn_nope=mla_qn_nope, mla_qn_rope=mla_qn_rope,
             mla_kn_nope=mla_kn_nope, mla_kn_rope=mla_kn_rope, w_br_mla=w_br_mla, w_br_sb=w_br_sb, w_out=w_out)
    depth = w_in.shape[0]
    bp = x_prompt.shape[0]
    layers = [_prep_layer(P, l) for l in range(depth)]
    mods = ada_mod(jnp.concatenate([c_prompt, c_sample], axis=0), w_ada, b_ada)
    y_p, st_p = _trunk(x_prompt, mods[:, :bp], None, layers, PROMPT_CFG)
    pasts = [(state_rwkv_wkv[l], state_rwkv_shift[l], cache_mla_ckv[l], cache_mla_krope[l],
              cache_sb_k[l], cache_sb_v[l]) for l in range(depth)]
    assert cache_mla_ckv.shape[2] == SAMPLE_CFG["past"]
    y_s, st_s = _trunk(x_sample, mods[:, bp:], pasts, layers, SAMPLE_CFG)
    return (y_p, y_s, *st_p, *st_s)
```

```python
import functools

import numpy as np
import jax
import jax.numpy as jnp
from jax import lax
from jax.experimental import pallas as pl
from jax.experimental.pallas import tpu as pltpu

F32 = jnp.float32
BF16 = jnp.bfloat16

D_MODEL = 1024
CHUNK = 64
RMS_EPS = 1e-6
GN_EPS = 64e-5
NEG_BIG = -1e30
ROPE_THETA = 10000.0
LOG2E = 1.4426950408889634

HEADS = 8
HEAD_DIM = 64
WIDTH = HEADS * HEAD_DIM
PAIRS = HEADS // 2
LANES = 128
LORA = 64
SHIFT_W = 3 * WIDTH + 2 * LORA
Q_LORA = 384
KV_LORA = 256
ROPE = 32
NOPE = 64
QK = NOPE + ROPE
RWKV_CHUNK = 64

IN_GROUPS = (("pr", SHIFT_W), ("zA", WIDTH), ("cq", Q_LORA), ("ckv", KV_LORA), ("kr", ROPE),
             ("zB", WIDTH), ("sq", WIDTH), ("sk", WIDTH), ("sv", WIDTH), ("zC", WIDTH),
             ("gates", 3 * D_MODEL))
BF16_COPIES = ("sk", "sv")
BF16_GROUPS = ("zA", "zB", "zC", "gates")

VMEM_LIMIT = 56 * 1024 * 1024


def _params(sem):
    return pltpu.CompilerParams(dimension_semantics=sem, vmem_limit_bytes=VMEM_LIMIT)


def _bdot(a, b):
    return jnp.dot(a.astype(BF16), b.astype(BF16), preferred_element_type=F32)


def _bdot_nt(a, b):
    return lax.dot_general(a.astype(BF16), b.astype(BF16), (((1,), (1,)), ((), ())),
                           preferred_element_type=F32)


def _bdot_tn(a, b):
    return lax.dot_general(a.astype(BF16), b.astype(BF16), (((0,), (0,)), ((), ())),
                           preferred_element_type=F32)


def _split(x, n):
    parts, r = [], x
    for _ in range(n):
        p = r.astype(BF16)
        parts.append(p)
        r = r - p.astype(F32)
    return parts


def _xdot_r(x, m01, n):
    m = m01.astype(BF16)
    return sum(jnp.dot(p, m, preferred_element_type=F32) for p in _split(x, n))


def _xdot_l(m01, x, n):
    m = m01.astype(BF16)
    return sum(jnp.dot(m, p, preferred_element_type=F32) for p in _split(x, n))


def _sigmoid(x):
    return 1.0 / (1.0 + jnp.exp(-x))


def _silu(x):
    return x * _sigmoid(x)


def _softplus(x):
    return jnp.maximum(x, 0.0) + jnp.log(1.0 + jnp.exp(-jnp.abs(x)))


def _rms(x, g):
    return x * lax.rsqrt(jnp.mean(x * x, axis=-1, keepdims=True) + RMS_EPS) * g


def _iota2(shape, dim):
    return lax.broadcasted_iota(jnp.int32, shape, dim)


def _ada_kernel(c_ref, w_ref, b_ref, o_ref):
    o_ref[0] = _bdot(_silu(c_ref[...]), w_ref[0]) + b_ref[0]


def ada_mod(c, w_ada, b_ada):
    L, D, N = w_ada.shape
    rows = c.shape[0]
    tn = 1024
    return pl.pallas_call(
        _ada_kernel,
        grid=(L, N // tn),
        in_specs=[pl.BlockSpec((rows, D), lambda l, j: (0, 0)),
                  pl.BlockSpec((1, D, tn), lambda l, j: (l, 0, j)),
                  pl.BlockSpec((1, 1, tn), lambda l, j: (l, 0, j))],
        out_specs=pl.BlockSpec((1, rows, tn), lambda l, j: (l, 0, j)),
        out_shape=jax.ShapeDtypeStruct((L, rows, N), F32),
        compiler_params=_params(("parallel", "parallel")),
        name="ada_mod",
    )(c, w_ada, b_ada.reshape(L, 1, N))


def _inproj_kernel(x_ref, sc_ref, sh_ref, g_ref, *refs):
    n = len(IN_GROUPS)
    ws, outs = refs[:n], refs[n:]
    h = _rms(x_ref[...], g_ref[...]) * (1.0 + sc_ref[0]) + sh_ref[0]
    hb = h.astype(BF16)
    extra = iter(outs[n:])
    for (name, _), w, o in zip(IN_GROUPS, ws, outs):
        res = jnp.dot(hb, w[...], preferred_element_type=F32)
        o[...] = res.astype(o.dtype)
        if name in BF16_COPIES:
            next(extra)[...] = res.astype(BF16)


def in_proj(x2, scale, shift, g, ws, T, tm):
    M, D = x2.shape
    assert T % tm == 0 and M % tm == 0
    per = T // tm
    row = lambda i: (i, 0)
    mod = lambda i: (i // per, 0, 0)
    const = lambda i: (0, 0)
    in_specs = [pl.BlockSpec((tm, D), row), pl.BlockSpec((1, 1, D), mod), pl.BlockSpec((1, 1, D), mod),
                pl.BlockSpec((1, D), const)]
    in_specs += [pl.BlockSpec((D, wd), const, pipeline_mode=pl.Buffered(1)) for _, wd in IN_GROUPS]
    out_specs = [pl.BlockSpec((tm, wd), row) for _, wd in IN_GROUPS]
    out_shape = [jax.ShapeDtypeStruct((M, wd), BF16 if name in BF16_GROUPS else F32) for name, wd in IN_GROUPS]
    out_specs += [pl.BlockSpec((tm, WIDTH), row) for _ in BF16_COPIES]
    out_shape += [jax.ShapeDtypeStruct((M, WIDTH), BF16) for _ in BF16_COPIES]
    return pl.pallas_call(
        _inproj_kernel, grid=(M // tm,), in_specs=in_specs, out_specs=out_specs, out_shape=out_shape,
        compiler_params=_params(("parallel",)), name="in_proj",
    )(x2, scale, shift, g, *ws)


def _rwkv_kernel(n, pr_ref, past_ref, s0_ref, mu_ref, w0_ref, a0_ref, kk_ref, ka_ref, rk_ref,
                 gng_ref, gnb_ref, wup_ref, aup_ref, y_ref, sfin_ref, st_scr, last_scr):
    tb = pl.program_id(1)

    @pl.when(tb == 0)
    def _():
        st_scr[...] = s0_ref[0]
        last_scr[...] = past_ref[0]

    p = pr_ref[0]
    row = _iota2((n, 1), 0)
    prev = jnp.where(row == 0, last_scr[...], pltpu.roll(p, 1, axis=0))
    last_scr[...] = p[n - 1:n, :]
    xs = p + (prev - p) * mu_ref[...]
    r, k, v = xs[:, :WIDTH], xs[:, WIDTH:2 * WIDTH], xs[:, 2 * WIDTH:3 * WIDTH]
    wa = xs[:, 3 * WIDTH:]
    w_log = -_softplus(-(w0_ref[...] + _bdot(jnp.tanh(wa), wup_ref[...]))) - 0.5
    lw = -jnp.exp(w_log)
    a = _sigmoid(a0_ref[...] + _bdot(wa, aup_ref[...]))

    ri, ci = _iota2((LANES, LANES), 0), _iota2((LANES, LANES), 1)
    seg64 = ((ri >> 6) == (ci >> 6)).astype(BF16)
    bd2 = ((ri >> 6) == (ci >> 6)).astype(F32)

    def segsum(t):
        return jnp.concatenate(
            [_xdot_r(t[:, LANES * q:LANES * (q + 1)], seg64, 2) for q in range(PAIRS)], axis=-1)

    kkr = k * kk_ref[...]
    kk = kkr * jnp.minimum(lax.rsqrt(segsum(kkr * kkr)), 1e12)
    k2 = k * (1.0 + (a - 1.0) * ka_ref[...])
    bonus = segsum(r * k2 * rk_ref[...]) * v

    chunk = min(RWKV_CHUNK, n)
    shift = chunk.bit_length() - 1
    tr, tc = _iota2((n, n), 0), _iota2((n, n), 1)
    same = (tr >> shift) == (tc >> shift)
    strict = (same & (tr > tc)).astype(F32)
    incl = (same & (tr >= tc)).astype(F32)
    eye = (tr == tc).astype(F32)
    lw_parts = _split(lw, 2)
    cum = lambda m01: sum(jnp.dot(m01.astype(BF16), part, preferred_element_type=F32) for part in lw_parts)
    g_log = cum(incl)
    g_end = cum(same)
    e_in, e_inv, e_rest = jnp.exp(g_log), jnp.exp(-g_log), jnp.exp(g_end - g_log)
    r_t = r * e_in
    kk_t = kk * jnp.exp(g_log - lw)
    kka = kk * a
    k_t, b_t = k2 * e_inv, kka * e_inv
    k_e, b_e = k2 * e_rest, kka * e_rest
    dec = jnp.exp(g_end)

    lane = _iota2((1, LANES), 1)
    head_mask = [(lane < HEAD_DIM).astype(F32), (lane >= HEAD_DIM).astype(F32)]

    sls = [slice(LANES * q, LANES * (q + 1)) for q in range(PAIRS)]
    units = [(q, hm) for q in range(PAIRS) for hm in head_mask]
    rp, kkp, vp = [r_t[:, s] for s in sls], [kk_t[:, s] for s in sls], [v[:, s] for s in sls]
    kb, bb = [k_t[:, s].astype(BF16) for s in sls], [b_t[:, s].astype(BF16) for s in sls]
    vb, kkb = [x.astype(BF16) for x in vp], [x.astype(BF16) for x in kkp]
    kkm = [(kkp[q] * hm).astype(BF16) for q, hm in units]
    rm = [(rp[q] * hm).astype(BF16) for q, hm in units]
    l_b = [_bdot_nt(kkm[u], bb[q]) * strict for u, (q, _) in enumerate(units)]
    l_k = [_bdot_nt(kkm[u], kb[q]) * strict for u, (q, _) in enumerate(units)]
    r_k = [_bdot_nt(rm[u], kb[q]) * incl for u, (q, _) in enumerate(units)]
    r_b = [_bdot_nt(rm[u], bb[q]) * incl for u, (q, _) in enumerate(units)]
    sub = min(16, chunk)
    sub_shift = sub.bit_length() - 1
    inner = ((tr >> sub_shift) == (tc >> sub_shift)).astype(F32)
    pw = [x * inner for x in l_b]
    t_inv = [eye - x for x in pw]
    for _ in range(sub_shift - 1):
        pw = [_bdot(x, x) for x in pw]
        t_inv = [t + _bdot(t, x) for t, x in zip(t_inv, pw)]
    for level in range(sub_shift, shift):
        joined = (((tr >> (level + 1)) == (tc >> (level + 1))) & ((tr >> level) != (tc >> level))).astype(F32)
        t_inv = [t - _bdot(_bdot(t, x * joined), t) for t, x in zip(t_inv, l_b)]
    lkv = [_bdot(l_k[u], vb[q]) for u, (q, _) in enumerate(units)]
    w_h = [_bdot(t_inv[u], kkb[q]) for u, (q, _) in enumerate(units)]
    u0_h = [_bdot(t, x) for t, x in zip(t_inv, lkv)]
    q_h = [rp[q] - _bdot(r_b[u], w_h[u]) for u, (q, _) in enumerate(units)]
    y0_h = [_bdot(r_k[u], vb[q]) - _bdot(r_b[u], u0_h[u]) for u, (q, _) in enumerate(units)]
    first = head_mask[0] > 0.5
    pair = lambda xs: [jnp.where(first, xs[2 * q], xs[2 * q + 1]) for q in range(PAIRS)]
    w_c, u0_c, q_c, y0_c = pair(w_h), pair(u0_h), pair(q_h), pair(y0_h)

    st = [st_scr[q] for q in range(PAIRS)]
    ys = [[] for _ in range(PAIRS)]
    rows = [slice(chunk * c, chunk * (c + 1)) for c in range(n // chunk)]
    n_c = [[(_bdot_tn(vp[q][rs], k_e[rs, sls[q]]) - _bdot_tn(u0_c[q][rs], b_e[rs, sls[q]])) * bd2
            for q in range(PAIRS)] for rs in rows]
    a_c = [[_bdot_tn(w_c[q][rs], b_e[rs, sls[q]]) * bd2 for q in range(PAIRS)] for rs in rows]
    for c, rs in enumerate(rows):
        for q in range(PAIRS):
            ys[q].append(_bdot_nt(q_c[q][rs], st[q]) + y0_c[q][rs])
        sa = [_bdot(st[q], a_c[c][q]) for q in range(PAIRS)]
        st = [st[q] * dec[chunk * c:chunk * c + 1, sls[q]] + (n_c[c][q] - sa[q]) for q in range(PAIRS)]
    for q in range(PAIRS):
        st_scr[q] = st[q]
    y = [x[0] if len(x) == 1 else jnp.concatenate(x, axis=0) for x in ys]
    mean = [_xdot_r(x, seg64, 2) * (1.0 / HEAD_DIM) for x in y]
    d = [x - m for x, m in zip(y, mean)]
    var = [_xdot_r(x * x, seg64, 2) * (1.0 / HEAD_DIM) for x in d]
    for q, s in enumerate(sls):
        yn = d[q] * lax.rsqrt(var[q] + GN_EPS) * gng_ref[:, s] + gnb_ref[:, s] + bonus[:, s]
        y_ref[0, :, s] = yn.astype(y_ref.dtype)

    @pl.when(tb == pl.num_programs(1) - 1)
    def _():
        sfin_ref[0] = st_scr[...]


def rwkv_branch(pr, past_shift, s0_bd, wl, n):
    B, T, _ = pr.shape
    assert T % n == 0 and n % min(RWKV_CHUNK, n) == 0 and n & (n - 1) == 0
    vec = lambda wd: pl.BlockSpec((1, wd), lambda b, t: (0, 0))
    in_specs = [pl.BlockSpec((1, n, SHIFT_W), lambda b, t: (b, t, 0)),
                pl.BlockSpec((1, 1, SHIFT_W), lambda b, t: (b, 0, 0)),
                pl.BlockSpec((1, PAIRS, LANES, LANES), lambda b, t: (b, 0, 0, 0)),
                vec(SHIFT_W)] + [vec(WIDTH)] * 7 + [
                pl.BlockSpec((LANES, WIDTH), lambda b, t: (0, 0)),
                pl.BlockSpec((LANES, WIDTH), lambda b, t: (0, 0))]
    return pl.pallas_call(
        functools.partial(_rwkv_kernel, n),
        grid=(B, T // n),
        in_specs=in_specs,
        out_specs=[pl.BlockSpec((1, n, WIDTH), lambda b, t: (b, t, 0)),
                   pl.BlockSpec((1, PAIRS, LANES, LANES), lambda b, t: (b, 0, 0, 0))],
        out_shape=[jax.ShapeDtypeStruct((B, T, WIDTH), BF16),
                   jax.ShapeDtypeStruct((B, PAIRS, LANES, LANES), F32)],
        scratch_shapes=[pltpu.VMEM((PAIRS, LANES, LANES), F32), pltpu.VMEM((1, SHIFT_W), F32)],
        compiler_params=_params(("parallel", "arbitrary")),
        name="rwkv7",
    )(pr, past_shift, s0_bd, wl["mu"], wl["w0"], wl["a0"], wl["k_k"], wl["k_a"], wl["r_k"],
      wl["gn_g"], wl["gn_b"], wl["w_up"], wl["a_up"])


def _state_to_bd(s):
    B = s.shape[0]
    s = s.reshape(B, PAIRS, 2, HEAD_DIM, HEAD_DIM)
    z = jnp.zeros_like(s[:, :, 0])
    top = jnp.concatenate([s[:, :, 0], z], axis=-1)
    bot = jnp.concatenate([z, s[:, :, 1]], axis=-1)
    return jnp.concatenate([top, bot], axis=-2)


def _bd_to_state(s):
    B = s.shape[0]
    h0 = s[:, :, :HEAD_DIM, :HEAD_DIM]
    h1 = s[:, :, HEAD_DIM:, HEAD_DIM:]
    return jnp.stack([h0, h1], axis=2).reshape(B, HEADS, HEAD_DIM, HEAD_DIM)


def _mla_q_kernel(cq_ref, ckv_ref, kr_ref, c96_ref, s96_ref, c32_ref, s32_ref, gq_ref, wuq_ref, gqn_ref,
                  gkv_ref, gkr_ref, p96_ref, p32_ref, q_ref, ckvn_ref, krope_ref):
    cqb = _rms(cq_ref[0], gq_ref[...]).astype(BF16)
    nope = _iota2((1, QK), 1) < NOPE
    c96, s96 = c96_ref[...], s96_ref[...]
    for h in range(HEADS):
        qh = jnp.dot(cqb, wuq_ref[h], preferred_element_type=F32)
        sq = qh * qh
        ss_n = jnp.sum(jnp.where(nope, sq, 0.0), axis=-1, keepdims=True)
        ss_r = jnp.sum(jnp.where(nope, 0.0, sq), axis=-1, keepdims=True)
        inv = jnp.where(nope, lax.rsqrt(ss_n * (1.0 / NOPE) + RMS_EPS), lax.rsqrt(ss_r * (1.0 / ROPE) + RMS_EPS))
        qn = qh * inv * gqn_ref[...]
        qo = qn * c96 + _xdot_r(qn, p96_ref[...], 3) * s96
        q_ref[0, h] = (qo * (QK ** -0.5 * LOG2E)).astype(BF16)
    ckvn_ref[0] = _rms(ckv_ref[0], gkv_ref[...])
    krn = _rms(kr_ref[0], gkr_ref[...])
    krope_ref[0] = krn * c32_ref[...] + _xdot_r(krn, p32_ref[...], 3) * s32_ref[...]


def mla_q(cq, ckv, kr, tabs, wl, tm):
    B, T, _ = cq.shape
    c96, s96, c32, s32 = tabs
    tok = lambda wd: pl.BlockSpec((1, tm, wd), lambda b, t: (b, t, 0))
    tab = lambda wd: pl.BlockSpec((tm, wd), lambda b, t: (t, 0))
    vec = lambda wd: pl.BlockSpec((1, wd), lambda b, t: (0, 0))
    return pl.pallas_call(
        _mla_q_kernel,
        grid=(B, T // tm),
        in_specs=[tok(Q_LORA), tok(KV_LORA), tok(ROPE), tab(QK), tab(QK), tab(ROPE), tab(ROPE),
                  vec(Q_LORA), pl.BlockSpec((HEADS, Q_LORA, QK), lambda b, t: (0, 0, 0)), vec(QK),
                  vec(KV_LORA), vec(ROPE),
                  pl.BlockSpec((QK, QK), lambda b, t: (0, 0)), pl.BlockSpec((ROPE, ROPE), lambda b, t: (0, 0))],
        out_specs=[pl.BlockSpec((1, HEADS, tm, QK), lambda b, t: (b, 0, t, 0)), tok(KV_LORA), tok(ROPE)],
        out_shape=[jax.ShapeDtypeStruct((B, HEADS, T, QK), BF16),
                   jax.ShapeDtypeStruct((B, T, KV_LORA), F32),
                   jax.ShapeDtypeStruct((B, T, ROPE), F32)],
        compiler_params=_params(("parallel", "parallel")),
        name="mla_q",
    )(cq, ckv, kr, c96, s96, c32, s32, wl["q_norm"], wl["w_uq"], wl["qn"], wl["kv_norm"], wl["kn_rope"],
      wl["p96"], wl["p32"])


def _mla_kv_kernel(ckv_ref, kr_ref, wuk_ref, wuv_ref, gkn_ref, e_ref, k_ref, v_ref):
    cb = ckv_ref[0].astype(BF16)
    k_rope = _xdot_r(kr_ref[0], e_ref[...], 3)
    for h in range(HEADS):
        kh = jnp.dot(cb, wuk_ref[h], preferred_element_type=F32)
        ms = jnp.sum(kh * kh, axis=-1, keepdims=True) * (1.0 / NOPE)
        k_ref[0, h] = (kh * lax.rsqrt(ms + RMS_EPS) * gkn_ref[...] + k_rope).astype(BF16)
    for q in range(PAIRS):
        v_ref[0, q] = jnp.dot(cb, wuv_ref[q], preferred_element_type=F32).astype(BF16)


def mla_kv(ckv_all, kr_all, wl, ts):
    B, S, _ = ckv_all.shape
    return pl.pallas_call(
        _mla_kv_kernel,
        grid=(B, S // ts),
        in_specs=[pl.BlockSpec((1, ts, KV_LORA), lambda b, t: (b, t, 0)),
                  pl.BlockSpec((1, ts, ROPE), lambda b, t: (b, t, 0)),
                  pl.BlockSpec((HEADS, KV_LORA, QK), lambda b, t: (0, 0, 0)),
                  pl.BlockSpec((PAIRS, KV_LORA, LANES), lambda b, t: (0, 0, 0)),
                  pl.BlockSpec((1, QK), lambda b, t: (0, 0)),
                  pl.BlockSpec((ROPE, QK), lambda b, t: (0, 0))],
        out_specs=[pl.BlockSpec((1, HEADS, ts, QK), lambda b, t: (b, 0, t, 0)),
                   pl.BlockSpec((1, PAIRS, ts, LANES), lambda b, t: (b, 0, t, 0))],
        out_shape=[jax.ShapeDtypeStruct((B, HEADS, S, QK), BF16),
                   jax.ShapeDtypeStruct((B, PAIRS, S, LANES), BF16)],
        compiler_params=_params(("parallel", "parallel")),
        name="mla_kv",
    )(ckv_all, kr_all, wl["w_uk"], wl["w_uv"], wl["kn_nope"], wl["e96"])


def _mla_attn_kernel(tq, tk, tkd, past, s_len, q_ref, k_ref, v_ref, o_ref, m_scr, l_scr, acc_scr):
    q0 = past + pl.program_id(2) * tq
    first = _iota2((1, LANES), 1) < HEAD_DIM
    m_scr[...] = jnp.full(m_scr.shape, NEG_BIG, F32)
    l_scr[...] = jnp.zeros(l_scr.shape, F32)
    acc_scr[...] = jnp.zeros(acc_scr.shape, F32)
    qs = [q_ref[0, h] for h in range(2)]

    def block(start, width, masked):
        reps = width // LANES
        lane_chunks = lambda x: [x[:, LANES * c:LANES * (c + 1)] for c in range(reps)]
        ks = pl.multiple_of(start, width)
        s = [_bdot_nt(qs[h], k_ref[0, h, pl.ds(ks, width), :]) for h in range(2)]
        if masked:
            limit = jnp.minimum((((q0 + _iota2((tq, 1), 0)) >> 6) + 1) << 6, s_len)
            visible = (ks + _iota2((1, width), 1)) < limit
            s = [jnp.where(visible, x, NEG_BIG) for x in s]
        vb = v_ref[0, 0, pl.ds(ks, width), :]
        m_old = [m_scr[h] for h in range(2)]
        m_blk = [functools.reduce(jnp.maximum, lane_chunks(x)) for x in s]
        m_new = [jnp.maximum(m_old[h], jnp.max(m_blk[h], axis=-1, keepdims=True)) for h in range(2)]
        alpha = [jnp.exp2(m_old[h] - m_new[h]) for h in range(2)]
        p = [jnp.exp2(s[h] - jnp.concatenate([m_new[h]] * reps, axis=1)) for h in range(2)]
        for h in range(2):
            l_scr[h] = functools.reduce(jnp.add, lane_chunks(p[h]), alpha[h] * l_scr[h])
            m_scr[h] = m_new[h]
        pv = [_bdot(p[h], vb) for h in range(2)]
        acc_scr[...] = acc_scr[...] * jnp.where(first, alpha[0], alpha[1]) + jnp.where(first, pv[0], pv[1])

    n_full = jnp.minimum((q0 // CHUNK + 1) * CHUNK, s_len) // tk
    seen = jnp.minimum(((q0 + tq - 1) // CHUNK + 1) * CHUNK, s_len)
    n_diag = (seen - n_full * tk + tkd - 1) // tkd

    def full_body(kb, carry):
        block(kb * tk, tk, False)
        return carry

    def diag_body(i, carry):
        block(n_full * tk + i * tkd, tkd, True)
        return carry

    lax.fori_loop(0, n_full, full_body, 0)
    lax.fori_loop(0, n_diag, diag_body, 0)
    l = [jnp.sum(l_scr[h], axis=-1, keepdims=True) for h in range(2)]
    o_ref[0] = (acc_scr[...] / jnp.where(first, l[0], l[1])).astype(o_ref.dtype)


def mla_attn(q, k, v, past, s_len, tq, tk, tkd):
    B, _, T, _ = q.shape
    S = k.shape[2]
    assert S % tk == 0 and tk % tkd == 0 and tkd % LANES == 0 and T % tq == 0
    return pl.pallas_call(
        functools.partial(_mla_attn_kernel, tq, tk, tkd, past, s_len),
        grid=(B, PAIRS, T // tq),
        in_specs=[pl.BlockSpec((1, 2, tq, QK), lambda b, p, i: (b, p, i, 0)),
                  pl.BlockSpec((1, 2, S, QK), lambda b, p, i: (b, p, 0, 0)),
                  pl.BlockSpec((1, 1, S, LANES), lambda b, p, i: (b, p, 0, 0))],
        out_specs=pl.BlockSpec((1, tq, LANES), lambda b, p, i: (b, i, p)),
        out_shape=jax.ShapeDtypeStruct((B, T, WIDTH), BF16),
        scratch_shapes=[pltpu.VMEM((2, tq, LANES), F32), pltpu.VMEM((2, tq, LANES), F32),
                        pltpu.VMEM((tq, LANES), F32)],
        compiler_params=_params(("parallel", "parallel", "arbitrary")),
        name="mla_attn",
    )(q, k, v)


def _sb_kernel(tq, tk, past, q_ref, k_ref, v_ref, sum_ref, o_ref, carry_scr, acc_scr, qm_scr, z_scr):
    nk = k_ref.shape[1] // tk
    q0 = past + pl.program_id(2) * tq
    first = _iota2((1, LANES), 1) < HEAD_DIM
    carry_scr[...] = jnp.zeros(carry_scr.shape, F32)
    acc_scr[...] = jnp.zeros(acc_scr.shape, F32)
    qv = q_ref[0] * (HEAD_DIM ** -0.5 * LOG2E)
    qm_scr[0] = jnp.where(first, qv, 0.0).astype(BF16)
    qm_scr[1] = jnp.where(first, 0.0, qv).astype(BF16)

    def key_mask(kb):
        return (kb * tk + _iota2((1, tk), 1)) < (q0 + _iota2((tq, 1), 0))

    def stage_z(kb):
        kblk = k_ref[0, pl.ds(pl.multiple_of(kb * tk, tk), tk), :]
        return [_bdot_nt(qm_scr[h], kblk) for h in range(2)]

    def stage_w(z, mask):
        split, tot = [], []
        for read in z:
            x = read()
            neg_abs = lax.bitcast_convert_type(lax.bitcast_convert_type(x, jnp.uint32) | jnp.uint32(0x80000000), F32)
            w = jnp.maximum(x, 0.0) + jnp.log2(1.0 + jnp.exp2(neg_abs))
            if mask is not None:
                w = jnp.where(mask, w, 0.0)
            hi = w.astype(BF16)
            split.append(jnp.concatenate([hi, (w - hi.astype(F32)).astype(BF16)], axis=1))
            tot.append(jnp.sum(w, axis=-1, keepdims=True))
        return split, tot

    def stage_incl(split):
        return [jnp.dot(x, sum_ref[...], preferred_element_type=F32) for x in split]

    def stage_a(z, incl, c, mask):
        cb = [jnp.concatenate([x] * (tk // LANES), axis=1) for x in c]
        a = [jnp.exp2(z[h]() - incl[h] - cb[h]) for h in range(2)]
        if mask is not None:
            a = [jnp.where(mask, x, 0.0) for x in a]
        return a

    def stage_pv(a, kb):
        vblk = v_ref[0, pl.ds(pl.multiple_of(kb * tk, tk), tk), :]
        pv = [_bdot(x, vblk) for x in a]
        return jnp.where(first, pv[0], pv[1])

    def one_block(kb, masked):
        mask = key_mask(kb) if masked else None
        z = [lambda v=v: v for v in stage_z(kb)]
        split, tot = stage_w(z, mask)
        c = [carry_scr[h] for h in range(2)]
        a = stage_a(z, stage_incl(split), c, mask)
        acc_scr[...] += stage_pv(a, kb)
        for h in range(2):
            carry_scr[h] = c[h] + tot[h]

    def prefetch_z(kb, slot):
        for j in range(2):
            z = stage_z(jnp.maximum(kb - j, 0))
            for h in range(2):
                z_scr[slot, j, h] = z[h]

    def two_blocks(kb, slot):
        z1, z2 = ([lambda j=j, h=h: z_scr[slot, j, h] for h in range(2)] for j in range(2))
        split1, tot1 = stage_w(z1, None)
        incl1 = stage_incl(split1)
        prefetch_z(kb - 2, 1 - slot)
        split2, tot2 = stage_w(z2, None)
        incl2 = stage_incl(split2)
        c1 = [carry_scr[h] for h in range(2)]
        pv1 = stage_pv(stage_a(z1, incl1, c1, None), kb)
        c2 = [c1[h] + tot1[h] for h in range(2)]
        pv2 = stage_pv(stage_a(z2, incl2, c2, None), kb - 1)
        acc_scr[...] += pv1 + pv2
        for h in range(2):
            carry_scr[h] = c2[h] + tot2[h]

    n_full = q0 // tk
    last = jnp.clip((q0 + tq - 2) // tk, 0, nk - 1)

    def loop(count, fn):
        def body(i, carry):
            fn(i)
            return carry
        lax.fori_loop(0, count, body, 0)

    loop(last - n_full + 1, lambda i: one_block(last - i, True))
    def four_blocks(i):
        two_blocks(n_full - 1 - 4 * i, 0)
        two_blocks(n_full - 3 - 4 * i, 1)

    rest = n_full % 4
    prefetch_z(n_full - 1, 0)
    loop(n_full // 4, four_blocks)
    loop(rest // 2, lambda i: two_blocks(rest - 1, 0))
    loop(rest % 2, lambda i: one_block(0, False))
    o_ref[0] = acc_scr[...].astype(o_ref.dtype)


def sb_attn(q, k_all, v_all, past, tq, tk):
    B, T, _ = q.shape
    S = k_all.shape[1]
    assert S % tk == 0 and T % tq == 0 and tk % LANES == 0
    incl = np.arange(tk)[:, None] >= np.arange(tk)[None, :]
    sum_mat = jnp.asarray(np.concatenate([incl, incl], axis=0), BF16)
    return pl.pallas_call(
        functools.partial(_sb_kernel, tq, tk, past),
        grid=(B, PAIRS, T // tq),
        in_specs=[pl.BlockSpec((1, tq, LANES), lambda b, p, i: (b, i, p)),
                  pl.BlockSpec((1, S, LANES), lambda b, p, i: (b, 0, p)),
                  pl.BlockSpec((1, S, LANES), lambda b, p, i: (b, 0, p)),
                  pl.BlockSpec((2 * tk, tk), lambda b, p, i: (0, 0))],
        out_specs=pl.BlockSpec((1, tq, LANES), lambda b, p, i: (b, i, p)),
        out_shape=jax.ShapeDtypeStruct((B, T, WIDTH), BF16),
        scratch_shapes=[pltpu.VMEM((2, tq, LANES), F32), pltpu.VMEM((tq, LANES), F32),
                        pltpu.VMEM((2, tq, LANES), BF16), pltpu.VMEM((2, 2, 2, tq, tk), F32)],
        compiler_params=_params(("parallel", "parallel", "arbitrary")),
        name="sb_attn",
    )(q, k_all, v_all, sum_mat)


def _merge_kernel(x_ref, gm_ref, ya_ref, za_ref, yb_ref, zb_ref, yc_ref, zc_ref, g_ref,
                  wa_ref, wb_ref, wc_ref, wo_ref, o_ref):
    def branch(y_ref, z_ref, w_ref):
        return _bdot(y_ref[...].astype(F32) * _silu(z_ref[...].astype(F32)), w_ref[...])

    sg = _sigmoid(g_ref[...].astype(F32))
    merged = (sg[:, :D_MODEL] * branch(ya_ref, za_ref, wa_ref)
              + sg[:, D_MODEL:2 * D_MODEL] * branch(yb_ref, zb_ref, wb_ref)
              + sg[:, 2 * D_MODEL:] * branch(yc_ref, zc_ref, wc_ref))
    o_ref[...] = x_ref[...] + gm_ref[0] * _bdot(merged, wo_ref[...])


def merge_out(x2, gate_mod, ya, za, yb, zb, yc, zc, gates, wl, T, tm):
    M, D = x2.shape
    per = T // tm
    row = lambda wd: pl.BlockSpec((tm, wd), lambda i: (i, 0))
    const = lambda r, c: pl.BlockSpec((r, c), lambda i: (0, 0))
    return pl.pallas_call(
        _merge_kernel,
        grid=(M // tm,),
        in_specs=[row(D), pl.BlockSpec((1, 1, D), lambda i: (i // per, 0, 0))] + [row(WIDTH)] * 6 + [row(3 * D),
                  const(WIDTH, D), const(WIDTH, D), const(WIDTH, D), const(D, D)],
        out_specs=row(D),
        out_shape=jax.ShapeDtypeStruct((M, D), F32),
        compiler_params=_params(("parallel",)),
        name="merge_out",
    )(x2, gate_mod, ya, za, yb, zb, yc, zc, gates, wl["w_br_rwkv"], wl["w_br_mla"], wl["w_br_sb"], wl["w_out"])


def _rot_matrix(width, offset):
    half = ROPE // 2
    m = np.zeros((width, width), np.float32)
    for i in range(half):
        m[offset + half + i, offset + i] = -1.0
        m[offset + i, offset + half + i] = 1.0
    return jnp.asarray(m, BF16)


def _prep_layer(P, l):
    row = lambda a: a[l].reshape(1, -1)
    w_in = P["w_in"][l]
    offs = np.cumsum([0] + [wd for _, wd in IN_GROUPS])
    zeros_lora = jnp.zeros((LORA, WIDTH), F32)
    w_uq = P["mla_w_uq"][l].reshape(Q_LORA, HEADS, QK).transpose(1, 0, 2)
    w_ukv = P["mla_w_ukv"][l].reshape(KV_LORA, HEADS, 2 * HEAD_DIM)
    w_uk = jnp.pad(w_ukv[:, :, :NOPE].transpose(1, 0, 2), ((0, 0), (0, 0), (0, ROPE)))
    w_uv = w_ukv[:, :, NOPE:].reshape(KV_LORA, PAIRS, LANES).transpose(1, 0, 2)
    e96 = np.zeros((ROPE, QK), np.float32)
    e96[np.arange(ROPE), NOPE + np.arange(ROPE)] = 1.0
    return dict(
        norm_g=row(P["norm_g"]),
        w_in=[w_in[:, offs[i]:offs[i + 1]].astype(BF16) for i in range(len(IN_GROUPS))],
        mu=row(P["rwkv_mu"]), w0=row(P["rwkv_w0"]), a0=row(P["rwkv_a0"]), k_k=row(P["rwkv_k_k"]),
        k_a=row(P["rwkv_k_a"]), r_k=row(P["rwkv_r_k"]), gn_g=row(P["rwkv_gn_g"]), gn_b=row(P["rwkv_gn_b"]),
        w_up=jnp.concatenate([P["rwkv_w_up"][l], zeros_lora], axis=0).astype(BF16),
        a_up=jnp.concatenate([zeros_lora, P["rwkv_a_up"][l]], axis=0).astype(BF16),
        q_norm=row(P["mla_q_norm"]), w_uq=w_uq.astype(BF16),
        qn=jnp.concatenate([P["mla_qn_nope"][l], P["mla_qn_rope"][l]]).reshape(1, QK),
        kv_norm=row(P["mla_kv_norm"]), kn_rope=row(P["mla_kn_rope"]),
        kn_nope=jnp.pad(P["mla_kn_nope"][l], (0, ROPE)).reshape(1, QK),
        w_uk=w_uk.astype(BF16), w_uv=w_uv.astype(BF16),
        p96=_rot_matrix(QK, NOPE), p32=_rot_matrix(ROPE, 0), e96=jnp.asarray(e96, BF16),
        w_br_rwkv=P["w_br_rwkv"][l].astype(BF16), w_br_mla=P["w_br_mla"][l].astype(BF16),
        w_br_sb=P["w_br_sb"][l].astype(BF16), w_out=P["w_out"][l].astype(BF16),
    )


def _rope_tables(past, T):
    half = ROPE // 2
    inv = ROPE_THETA ** (-jnp.arange(half, dtype=F32) / half)
    ang = (past + jnp.arange(T, dtype=jnp.int32)).astype(F32)[:, None] * inv
    c32 = jnp.tile(jnp.cos(ang), (1, 2))
    s32 = jnp.tile(jnp.sin(ang), (1, 2))
    c96 = jnp.concatenate([jnp.ones((T, NOPE), F32), c32], axis=1)
    s96 = jnp.concatenate([jnp.zeros((T, NOPE), F32), s32], axis=1)
    return c96, s96, c32, s32


def _pad_rows(a, rows):
    return a if a.shape[1] == rows else jnp.pad(a, ((0, 0), (0, rows - a.shape[1])) + ((0, 0),) * (a.ndim - 2))


def _trunk(x, mods, pasts, layers, cfg):
    B, T, D = x.shape
    past = cfg["past"]
    S = past + T
    s_pad = -(-S // cfg["kv_mult"]) * cfg["kv_mult"]
    tabs = _rope_tables(past, T)
    x2 = x.reshape(B * T, D)
    new = []
    for l, wl in enumerate(layers):
        shift, scale, gate = (mods[l][:, None, i * D:(i + 1) * D] for i in range(3))
        outs = in_proj(x2, scale, shift, wl["norm_g"], wl["w_in"], T, cfg["tm"])
        pr, zA, cq, ckv, kr, zB, sq, sk, sv, zC, gates, sk_b, sv_b = outs
        seq = lambda a: a.reshape(B, T, a.shape[-1])
        if pasts is None:
            s0 = jnp.zeros((B, PAIRS, LANES, LANES), F32)
            shift0 = jnp.zeros((B, 1, SHIFT_W), F32)
        else:
            s0, shift0 = _state_to_bd(pasts[l][0]), pasts[l][1]
        pr3 = seq(pr)
        yA, s_fin = rwkv_branch(pr3, shift0, s0, wl, cfg["rwkv_n"])
        q, ckv_n, k_rope = mla_q(seq(cq), seq(ckv), seq(kr), tabs, wl, cfg["tm_q"])
        sk3, sv3 = seq(sk), seq(sv)
        if pasts is None:
            ckv_all, kr_all, k_all, v_all = ckv_n, k_rope, seq(sk_b), seq(sv_b)
        else:
            _, _, ckv0, kr0, k0, v0 = pasts[l]
            ckv_all = jnp.concatenate([ckv0, ckv_n], axis=1)
            kr_all = jnp.concatenate([kr0, k_rope], axis=1)
            k_all = jnp.concatenate([k0.reshape(B, past, WIDTH).astype(BF16), seq(sk_b)], axis=1)
            v_all = jnp.concatenate([v0.reshape(B, past, WIDTH).astype(BF16), seq(sv_b)], axis=1)
        ckv_all, kr_all, k_all, v_all = (_pad_rows(a, s_pad) for a in (ckv_all, kr_all, k_all, v_all))
        kf, vf = mla_kv(ckv_all, kr_all, wl, cfg["ts"])
        yB = mla_attn(q, kf, vf, past, S, cfg["tq"], cfg["tk"], cfg["tkd"])
        yC = sb_attn(seq(sq), k_all, v_all, past, cfg["sb_tq"], cfg["sb_tk"])
        x2 = merge_out(x2, gate, yA.reshape(B * T, WIDTH), zA, yB.reshape(B * T, WIDTH), zB,
                       yC.reshape(B * T, WIDTH), zC, gates, wl, T, cfg["tm"])
        new.append((_bd_to_state(s_fin), pr3[:, -1:], ckv_n, k_rope,
                    sk3.reshape(B, T, HEADS, HEAD_DIM), sv3.reshape(B, T, HEADS, HEAD_DIM)))
    stacked = [jnp.stack([st[i] for st in new], axis=0) for i in range(6)]
    return x2.reshape(B, T, D), stacked


PROMPT_CFG = dict(past=0, tm=256, rwkv_n=128, tm_q=256, ts=256, tq=512, tk=1024, tkd=512, sb_tq=512, sb_tk=256,
                  kv_mult=1024)
SAMPLE_CFG = dict(past=2048, tm=16, rwkv_n=16, tm_q=16, ts=2560, tq=16, tk=2560, tkd=2560, sb_tq=16, sb_tk=512,
                  kv_mult=512)


def kernel(x_prompt, x_sample, state_rwkv_wkv, state_rwkv_shift, cache_mla_ckv, cache_mla_krope, cache_sb_k, cache_sb_v, c_prompt, c_sample, w_ada, b_ada, norm_g, w_in, rwkv_mu, rwkv_w0, rwkv_w_up, rwkv_a0, rwkv_a_up, rwkv_k_k, rwkv_k_a, rwkv_r_k, rwkv_gn_g, rwkv_gn_b, w_br_rwkv, mla_q_norm, mla_w_uq, mla_kv_norm, mla_w_ukv, mla_qn_nope, mla_qn_rope, mla_kn_nope, mla_kn_rope, w_br_mla, w_br_sb, w_out):
    P = dict(w_in=w_in, norm_g=norm_g, rwkv_mu=rwkv_mu, rwkv_w0=rwkv_w0, rwkv_w_up=rwkv_w_up, rwkv_a0=rwkv_a0,
             rwkv_a_up=rwkv_a_up, rwkv_k_k=rwkv_k_k, rwkv_k_a=rwkv_k_a, rwkv_r_k=rwkv_r_k, rwkv_gn_g=rwkv_gn_g,
             rwkv_gn_b=rwkv_gn_b, w_br_rwkv=w_br_rwkv, mla_q_norm=mla_q_norm, mla_w_uq=mla_w_uq,
             mla_kv_norm=mla_kv_norm, mla_w_ukv=mla_w_ukv, mla_qn_nope=mla_qn_nope, mla_qn_rope=mla_qn_rope,
             mla_kn_nope=mla_kn_nope, mla_kn_rope=mla_kn_rope, w_br_mla=w_br_mla, w_br_sb=w_br_sb, w_out=w_out)
    depth = w_in.shape[0]
    bp = x_prompt.shape[0]
    layers = [_prep_layer(P, l) for l in range(depth)]
    mods = ada_mod(jnp.concatenate([c_prompt, c_sample], axis=0), w_ada, b_ada)
    y_p, st_p = _trunk(x_prompt, mods[:, :bp], None, layers, PROMPT_CFG)
    pasts = [(state_rwkv_wkv[l], state_rwkv_shift[l], cache_mla_ckv[l], cache_mla_krope[l],
              cache_sb_k[l], cache_sb_v[l]) for l in range(depth)]
    assert cache_mla_ckv.shape[2] == SAMPLE_CFG["past"]
    y_s, st_s = _trunk(x_sample, mods[:, bp:], pasts, layers, SAMPLE_CFG)
    return (y_p, y_s, *st_p, *st_s)
```

```python
import functools

import numpy as np
import jax
import jax.numpy as jnp
from jax import lax
from jax.experimental import pallas as pl
from jax.experimental.pallas import tpu as pltpu

F32 = jnp.float32
BF16 = jnp.bfloat16

D_MODEL = 1024
CHUNK = 64
RMS_EPS = 1e-6
GN_EPS = 64e-5
NEG_BIG = -1e30
ROPE_THETA = 10000.0
LOG2E = 1.4426950408889634

HEADS = 8
HEAD_DIM = 64
WIDTH = HEADS * HEAD_DIM
PAIRS = HEADS // 2
LANES = 128
LORA = 64
SHIFT_W = 3 * WIDTH + 2 * LORA
Q_LORA = 384
KV_LORA = 256
ROPE = 32
NOPE = 64
QK = NOPE + ROPE
RWKV_CHUNK = 64

IN_GROUPS = (("pr", SHIFT_W), ("zA", WIDTH), ("cq", Q_LORA), ("ckv", KV_LORA), ("kr", ROPE),
             ("zB", WIDTH), ("sq", WIDTH), ("sk", WIDTH), ("sv", WIDTH), ("zC", WIDTH),
             ("gates", 3 * D_MODEL))
BF16_COPIES = ("sk", "sv")
BF16_GROUPS = ("zA", "zB", "zC", "gates")

VMEM_LIMIT = 56 * 1024 * 1024


def _params(sem):
    return pltpu.CompilerParams(dimension_semantics=sem, vmem_limit_bytes=VMEM_LIMIT)


def _bdot(a, b):
    return jnp.dot(a.astype(BF16), b.astype(BF16), preferred_element_type=F32)


def _bdot_nt(a, b):
    return lax.dot_general(a.astype(BF16), b.astype(BF16), (((1,), (1,)), ((), ())),
                           preferred_element_type=F32)


def _bdot_tn(a, b):
    return lax.dot_general(a.astype(BF16), b.astype(BF16), (((0,), (0,)), ((), ())),
                           preferred_element_type=F32)


def _split(x, n):
    parts, r = [], x
    for _ in range(n):
        p = r.astype(BF16)
        parts.append(p)
        r = r - p.astype(F32)
    return parts


def _xdot_r(x, m01, n):
    m = m01.astype(BF16)
    return sum(jnp.dot(p, m, preferred_element_type=F32) for p in _split(x, n))


def _xdot_l(m01, x, n):
    m = m01.astype(BF16)
    return sum(jnp.dot(m, p, preferred_element_type=F32) for p in _split(x, n))


def _sigmoid(x):
    return 1.0 / (1.0 + jnp.exp(-x))


def _silu(x):
    return x * _sigmoid(x)


def _softplus(x):
    return jnp.maximum(x, 0.0) + jnp.log(1.0 + jnp.exp(-jnp.abs(x)))


def _rms(x, g):
    return x * lax.rsqrt(jnp.mean(x * x, axis=-1, keepdims=True) + RMS_EPS) * g


def _iota2(shape, dim):
    return lax.broadcasted_iota(jnp.int32, shape, dim)


def _ada_kernel(c_ref, w_ref, b_ref, o_ref):
    o_ref[0] = _bdot(_silu(c_ref[...]), w_ref[0]) + b_ref[0]


def ada_mod(c, w_ada, b_ada):
    L, D, N = w_ada.shape
    rows = c.shape[0]
    tn = 1024
    return pl.pallas_call(
        _ada_kernel,
        grid=(L, N // tn),
        in_specs=[pl.BlockSpec((rows, D), lambda l, j: (0, 0)),
                  pl.BlockSpec((1, D, tn), lambda l, j: (l, 0, j)),
                  pl.BlockSpec((1, 1, tn), lambda l, j: (l, 0, j))],
        out_specs=pl.BlockSpec((1, rows, tn), lambda l, j: (l, 0, j)),
        out_shape=jax.ShapeDtypeStruct((L, rows, N), F32),
        compiler_params=_params(("parallel", "parallel")),
        name="ada_mod",
    )(c, w_ada, b_ada.reshape(L, 1, N))


def _in_weight_layout():
    order = [g for g in IN_GROUPS if g[0] != "kr"] + [g for g in IN_GROUPS if g[0] == "kr"]
    offs, pos = {}, 0
    for name, wd in order:
        offs[name] = pos
        pos += wd
    return order, offs


def _inproj_kernel(x_ref, sc_ref, sh_ref, g_ref, w_ref, *outs):
    n = len(IN_GROUPS)
    offs = _in_weight_layout()[1]
    h = _rms(x_ref[...], g_ref[...]) * (1.0 + sc_ref[0]) + sh_ref[0]
    hb = h.astype(BF16)
    extra = iter(outs[n:])
    for (name, wd), o in zip(IN_GROUPS, outs):
        res = jnp.dot(hb, w_ref[0, :, offs[name]:offs[name] + wd], preferred_element_type=F32)
        o[...] = res.astype(o.dtype)
        if name in BF16_COPIES:
            next(extra)[...] = res.astype(BF16)


def in_proj(x2, scale, shift, g, w_all, layer, T, tm):
    M, D = x2.shape
    assert T % tm == 0 and M % tm == 0
    per = T // tm
    row = lambda i: (i, 0)
    mod = lambda i: (i // per, 0, 0)
    const = lambda i: (0, 0)
    in_specs = [pl.BlockSpec((tm, D), row), pl.BlockSpec((1, 1, D), mod), pl.BlockSpec((1, 1, D), mod),
                pl.BlockSpec((1, D), const),
                pl.BlockSpec((1, D, w_all.shape[2]), lambda i: (layer, 0, 0), pipeline_mode=pl.Buffered(1))]
    out_specs = [pl.BlockSpec((tm, wd), row) for _, wd in IN_GROUPS]
    out_shape = [jax.ShapeDtypeStruct((M, wd), BF16 if name in BF16_GROUPS else F32) for name, wd in IN_GROUPS]
    out_specs += [pl.BlockSpec((tm, WIDTH), row) for _ in BF16_COPIES]
    out_shape += [jax.ShapeDtypeStruct((M, WIDTH), BF16) for _ in BF16_COPIES]
    return pl.pallas_call(
        _inproj_kernel, grid=(M // tm,), in_specs=in_specs, out_specs=out_specs, out_shape=out_shape,
        compiler_params=_params(("parallel",)), name="in_proj",
    )(x2, scale, shift, g, w_all)


def _rwkv_kernel(n, pr_ref, past_ref, s0_ref, mu_ref, w0_ref, a0_ref, kk_ref, ka_ref, rk_ref,
                 gng_ref, gnb_ref, wup_ref, aup_ref, y_ref, sfin_ref, st_scr, last_scr):
    tb = pl.program_id(1)

    @pl.when(tb == 0)
    def _():
        st_scr[...] = s0_ref[0]
        last_scr[...] = past_ref[0]

    p = pr_ref[0]
    row = _iota2((n, 1), 0)
    prev = jnp.where(row == 0, last_scr[...], pltpu.roll(p, 1, axis=0))
    last_scr[...] = p[n - 1:n, :]
    xs = p + (prev - p) * mu_ref[...]
    r, k, v = xs[:, :WIDTH], xs[:, WIDTH:2 * WIDTH], xs[:, 2 * WIDTH:3 * WIDTH]
    wa = xs[:, 3 * WIDTH:]
    w_log = -_softplus(-(w0_ref[...] + _bdot(jnp.tanh(wa), wup_ref[...]))) - 0.5
    lw = -jnp.exp(w_log)
    a = _sigmoid(a0_ref[...] + _bdot(wa, aup_ref[...]))

    ri, ci = _iota2((LANES, LANES), 0), _iota2((LANES, LANES), 1)
    seg64 = ((ri >> 6) == (ci >> 6)).astype(BF16)
    bd2 = ((ri >> 6) == (ci >> 6)).astype(F32)

    def segsum(t):
        return jnp.concatenate(
            [_xdot_r(t[:, LANES * q:LANES * (q + 1)], seg64, 2) for q in range(PAIRS)], axis=-1)

    kkr = k * kk_ref[...]
    kk = kkr * jnp.minimum(lax.rsqrt(segsum(kkr * kkr)), 1e12)
    k2 = k * (1.0 + (a - 1.0) * ka_ref[...])
    bonus = segsum(r * k2 * rk_ref[...]) * v

    chunk = min(RWKV_CHUNK, n)
    shift = chunk.bit_length() - 1
    tr, tc = _iota2((n, n), 0), _iota2((n, n), 1)
    same = (tr >> shift) == (tc >> shift)
    strict = (same & (tr > tc)).astype(F32)
    incl = (same & (tr >= tc)).astype(F32)
    eye = (tr == tc).astype(F32)
    lw_parts = _split(lw, 2)
    cum = lambda m01: sum(jnp.dot(m01.astype(BF16), part, preferred_element_type=F32) for part in lw_parts)
    g_log = cum(incl)
    g_end = cum(same)
    e_in, e_inv, e_rest = jnp.exp(g_log), jnp.exp(-g_log), jnp.exp(g_end - g_log)
    r_t = r * e_in
    kk_t = kk * jnp.exp(g_log - lw)
    kka = kk * a
    k_t, b_t = k2 * e_inv, kka * e_inv
    k_e, b_e = k2 * e_rest, kka * e_rest
    dec = jnp.exp(g_end)

    lane = _iota2((1, LANES), 1)
    head_mask = [(lane < HEAD_DIM).astype(F32), (lane >= HEAD_DIM).astype(F32)]

    sls = [slice(LANES * q, LANES * (q + 1)) for q in range(PAIRS)]
    units = [(q, hm) for q in range(PAIRS) for hm in head_mask]
    rp, kkp, vp = [r_t[:, s] for s in sls], [kk_t[:, s] for s in sls], [v[:, s] for s in sls]
    kb, bb = [k_t[:, s].astype(BF16) for s in sls], [b_t[:, s].astype(BF16) for s in sls]
    vb, kkb = [x.astype(BF16) for x in vp], [x.astype(BF16) for x in kkp]
    kkm = [(kkp[q] * hm).astype(BF16) for q, hm in units]
    rm = [(rp[q] * hm).astype(BF16) for q, hm in units]
    l_b = [_bdot_nt(kkm[u], bb[q]) * strict for u, (q, _) in enumerate(units)]
    l_k = [_bdot_nt(kkm[u], kb[q]) * strict for u, (q, _) in enumerate(units)]
    r_k = [_bdot_nt(rm[u], kb[q]) * incl for u, (q, _) in enumerate(units)]
    r_b = [_bdot_nt(rm[u], bb[q]) * incl for u, (q, _) in enumerate(units)]
    sub = min(16, chunk)
    sub_shift = sub.bit_length() - 1
    inner = ((tr >> sub_shift) == (tc >> sub_shift)).astype(F32)
    pw = [x * inner for x in l_b]
    t_inv = [eye - x for x in pw]
    for _ in range(sub_shift - 1):
        pw = [_bdot(x, x) for x in pw]
        t_inv = [t + _bdot(t, x) for t, x in zip(t_inv, pw)]
    for level in range(sub_shift, shift):
        joined = (((tr >> (level + 1)) == (tc >> (level + 1))) & ((tr >> level) != (tc >> level))).astype(F32)
        t_inv = [t - _bdot(_bdot(t, x * joined), t) for t, x in zip(t_inv, l_b)]
    lkv = [_bdot(l_k[u], vb[q]) for u, (q, _) in enumerate(units)]
    w_h = [_bdot(t_inv[u], kkb[q]) for u, (q, _) in enumerate(units)]
    u0_h = [_bdot(t, x) for t, x in zip(t_inv, lkv)]
    q_h = [rp[q] - _bdot(r_b[u], w_h[u]) for u, (q, _) in enumerate(units)]
    y0_h = [_bdot(r_k[u], vb[q]) - _bdot(r_b[u], u0_h[u]) for u, (q, _) in enumerate(units)]
    first = head_mask[0] > 0.5
    pair = lambda xs: [jnp.where(first, xs[2 * q], xs[2 * q + 1]) for q in range(PAIRS)]
    w_c, u0_c, q_c, y0_c = pair(w_h), pair(u0_h), pair(q_h), pair(y0_h)

    st = [st_scr[q] for q in range(PAIRS)]
    ys = [[] for _ in range(PAIRS)]
    rows = [slice(chunk * c, chunk * (c + 1)) for c in range(n // chunk)]
    n_c = [[(_bdot_tn(vp[q][rs], k_e[rs, sls[q]]) - _bdot_tn(u0_c[q][rs], b_e[rs, sls[q]])) * bd2
            for q in range(PAIRS)] for rs in rows]
    a_c = [[_bdot_tn(w_c[q][rs], b_e[rs, sls[q]]) * bd2 for q in range(PAIRS)] for rs in rows]
    for c, rs in enumerate(rows):
        for q in range(PAIRS):
            ys[q].append(_bdot_nt(q_c[q][rs], st[q]) + y0_c[q][rs])
        sa = [_bdot(st[q], a_c[c][q]) for q in range(PAIRS)]
        st = [st[q] * dec[chunk * c:chunk * c + 1, sls[q]] + (n_c[c][q] - sa[q]) for q in range(PAIRS)]
    for q in range(PAIRS):
        st_scr[q] = st[q]
    y = [x[0] if len(x) == 1 else jnp.concatenate(x, axis=0) for x in ys]
    mean = [_xdot_r(x, seg64, 2) * (1.0 / HEAD_DIM) for x in y]
    d = [x - m for x, m in zip(y, mean)]
    var = [_xdot_r(x * x, seg64, 2) * (1.0 / HEAD_DIM) for x in d]
    for q, s in enumerate(sls):
        yn = d[q] * lax.rsqrt(var[q] + GN_EPS) * gng_ref[:, s] + gnb_ref[:, s] + bonus[:, s]
        y_ref[0, :, s] = yn.astype(y_ref.dtype)

    @pl.when(tb == pl.num_programs(1) - 1)
    def _():
        sfin_ref[0] = st_scr[...]


def rwkv_branch(pr, past_shift, s0_bd, wl, n):
    B, T, _ = pr.shape
    assert T % n == 0 and n % min(RWKV_CHUNK, n) == 0 and n & (n - 1) == 0
    vec = lambda wd: pl.BlockSpec((1, wd), lambda b, t: (0, 0))
    in_specs = [pl.BlockSpec((1, n, SHIFT_W), lambda b, t: (b, t, 0)),
                pl.BlockSpec((1, 1, SHIFT_W), lambda b, t: (b, 0, 0)),
                pl.BlockSpec((1, PAIRS, LANES, LANES), lambda b, t: (b, 0, 0, 0)),
                vec(SHIFT_W)] + [vec(WIDTH)] * 7 + [
                pl.BlockSpec((LANES, WIDTH), lambda b, t: (0, 0)),
                pl.BlockSpec((LANES, WIDTH), lambda b, t: (0, 0))]
    return pl.pallas_call(
        functools.partial(_rwkv_kernel, n),
        grid=(B, T // n),
        in_specs=in_specs,
        out_specs=[pl.BlockSpec((1, n, WIDTH), lambda b, t: (b, t, 0)),
                   pl.BlockSpec((1, PAIRS, LANES, LANES), lambda b, t: (b, 0, 0, 0))],
        out_shape=[jax.ShapeDtypeStruct((B, T, WIDTH), BF16),
                   jax.ShapeDtypeStruct((B, PAIRS, LANES, LANES), F32)],
        scratch_shapes=[pltpu.VMEM((PAIRS, LANES, LANES), F32), pltpu.VMEM((1, SHIFT_W), F32)],
        compiler_params=_params(("parallel", "arbitrary")),
        name="rwkv7",
    )(pr, past_shift, s0_bd, wl["mu"], wl["w0"], wl["a0"], wl["k_k"], wl["k_a"], wl["r_k"],
      wl["gn_g"], wl["gn_b"], wl["w_up"], wl["a_up"])


def _state_to_bd(s):
    B = s.shape[0]
    s = s.reshape(B, PAIRS, 2, HEAD_DIM, HEAD_DIM)
    z = jnp.zeros_like(s[:, :, 0])
    top = jnp.concatenate([s[:, :, 0], z], axis=-1)
    bot = jnp.concatenate([z, s[:, :, 1]], axis=-1)
    return jnp.concatenate([top, bot], axis=-2)


def _bd_to_state(s):
    B = s.shape[0]
    h0 = s[:, :, :HEAD_DIM, :HEAD_DIM]
    h1 = s[:, :, HEAD_DIM:, HEAD_DIM:]
    return jnp.stack([h0, h1], axis=2).reshape(B, HEADS, HEAD_DIM, HEAD_DIM)


def _mla_q_kernel(cq_ref, ckv_ref, kr_ref, c96_ref, s96_ref, c32_ref, s32_ref, gq_ref, wuq_ref, gqn_ref,
                  gkv_ref, gkr_ref, p96_ref, p32_ref, q_ref, ckvn_ref, krope_ref):
    cqb = _rms(cq_ref[0], gq_ref[...]).astype(BF16)
    nope = _iota2((1, QK), 1) < NOPE
    c96, s96 = c96_ref[...], s96_ref[...]
    for h in range(HEADS):
        qh = jnp.dot(cqb, wuq_ref[h], preferred_element_type=F32)
        sq = qh * qh
        ss_n = jnp.sum(jnp.where(nope, sq, 0.0), axis=-1, keepdims=True)
        ss_r = jnp.sum(jnp.where(nope, 0.0, sq), axis=-1, keepdims=True)
        inv = jnp.where(nope, lax.rsqrt(ss_n * (1.0 / NOPE) + RMS_EPS), lax.rsqrt(ss_r * (1.0 / ROPE) + RMS_EPS))
        qn = qh * inv * gqn_ref[...]
        qo = qn * c96 + _xdot_r(qn, p96_ref[...], 3) * s96
        q_ref[0, h] = (qo * (QK ** -0.5 * LOG2E)).astype(BF16)
    ckvn_ref[0] = _rms(ckv_ref[0], gkv_ref[...])
    krn = _rms(kr_ref[0], gkr_ref[...])
    krope_ref[0] = krn * c32_ref[...] + _xdot_r(krn, p32_ref[...], 3) * s32_ref[...]


def mla_q(cq, ckv, kr, tabs, wl, tm):
    B, T, _ = cq.shape
    c96, s96, c32, s32 = tabs
    tok = lambda wd: pl.BlockSpec((1, tm, wd), lambda b, t: (b, t, 0))
    tab = lambda wd: pl.BlockSpec((tm, wd), lambda b, t: (t, 0))
    vec = lambda wd: pl.BlockSpec((1, wd), lambda b, t: (0, 0))
    return pl.pallas_call(
        _mla_q_kernel,
        grid=(B, T // tm),
        in_specs=[tok(Q_LORA), tok(KV_LORA), tok(ROPE), tab(QK), tab(QK), tab(ROPE), tab(ROPE),
                  vec(Q_LORA), pl.BlockSpec((HEADS, Q_LORA, QK), lambda b, t: (0, 0, 0)), vec(QK),
                  vec(KV_LORA), vec(ROPE),
                  pl.BlockSpec((QK, QK), lambda b, t: (0, 0)), pl.BlockSpec((ROPE, ROPE), lambda b, t: (0, 0))],
        out_specs=[pl.BlockSpec((1, HEADS, tm, QK), lambda b, t: (b, 0, t, 0)), tok(KV_LORA), tok(ROPE)],
        out_shape=[jax.ShapeDtypeStruct((B, HEADS, T, QK), BF16),
                   jax.ShapeDtypeStruct((B, T, KV_LORA), F32),
                   jax.ShapeDtypeStruct((B, T, ROPE), F32)],
        compiler_params=_params(("parallel", "parallel")),
        name="mla_q",
    )(cq, ckv, kr, c96, s96, c32, s32, wl["q_norm"], wl["w_uq"], wl["qn"], wl["kv_norm"], wl["kn_rope"],
      wl["p96"], wl["p32"])


def _mla_kv_kernel(ckv_ref, kr_ref, wuk_ref, wuv_ref, gkn_ref, e_ref, k_ref, v_ref):
    cb = ckv_ref[0].astype(BF16)
    k_rope = _xdot_r(kr_ref[0], e_ref[...], 3)
    for h in range(HEADS):
        kh = jnp.dot(cb, wuk_ref[h], preferred_element_type=F32)
        ms = jnp.sum(kh * kh, axis=-1, keepdims=True) * (1.0 / NOPE)
        k_ref[0, h] = (kh * lax.rsqrt(ms + RMS_EPS) * gkn_ref[...] + k_rope).astype(BF16)
    for q in range(PAIRS):
        v_ref[0, q] = jnp.dot(cb, wuv_ref[q], preferred_element_type=F32).astype(BF16)


def mla_kv(ckv_all, kr_all, wl, ts):
    B, S, _ = ckv_all.shape
    return pl.pallas_call(
        _mla_kv_kernel,
        grid=(B, S // ts),
        in_specs=[pl.BlockSpec((1, ts, KV_LORA), lambda b, t: (b, t, 0)),
                  pl.BlockSpec((1, ts, ROPE), lambda b, t: (b, t, 0)),
                  pl.BlockSpec((HEADS, KV_LORA, QK), lambda b, t: (0, 0, 0)),
                  pl.BlockSpec((PAIRS, KV_LORA, LANES), lambda b, t: (0, 0, 0)),
                  pl.BlockSpec((1, QK), lambda b, t: (0, 0)),
                  pl.BlockSpec((ROPE, QK), lambda b, t: (0, 0))],
        out_specs=[pl.BlockSpec((1, HEADS, ts, QK), lambda b, t: (b, 0, t, 0)),
                   pl.BlockSpec((1, PAIRS, ts, LANES), lambda b, t: (b, 0, t, 0))],
        out_shape=[jax.ShapeDtypeStruct((B, HEADS, S, QK), BF16),
                   jax.ShapeDtypeStruct((B, PAIRS, S, LANES), BF16)],
        compiler_params=_params(("parallel", "parallel")),
        name="mla_kv",
    )(ckv_all, kr_all, wl["w_uk"], wl["w_uv"], wl["kn_nope"], wl["e96"])


def _mla_attn_kernel(tq, tk, tkd, past, s_len, q_ref, k_ref, v_ref, o_ref, m_scr, l_scr, acc_scr):
    q0 = past + pl.program_id(2) * tq
    first = _iota2((1, LANES), 1) < HEAD_DIM
    m_scr[...] = jnp.full(m_scr.shape, NEG_BIG, F32)
    l_scr[...] = jnp.zeros(l_scr.shape, F32)
    acc_scr[...] = jnp.zeros(acc_scr.shape, F32)
    qs = [q_ref[0, h] for h in range(2)]

    def block(start, width, masked):
        reps = width // LANES
        lane_chunks = lambda x: [x[:, LANES * c:LANES * (c + 1)] for c in range(reps)]
        ks = pl.multiple_of(start, width)
        s = [_bdot_nt(qs[h], k_ref[0, h, pl.ds(ks, width), :]) for h in range(2)]
        if masked:
            limit = jnp.minimum((((q0 + _iota2((tq, 1), 0)) >> 6) + 1) << 6, s_len)
            visible = (ks + _iota2((1, width), 1)) < limit
            s = [jnp.where(visible, x, NEG_BIG) for x in s]
        vb = v_ref[0, 0, pl.ds(ks, width), :]
        m_old = [m_scr[h] for h in range(2)]
        m_blk = [functools.reduce(jnp.maximum, lane_chunks(x)) for x in s]
        m_new = [jnp.maximum(m_old[h], jnp.max(m_blk[h], axis=-1, keepdims=True)) for h in range(2)]
        alpha = [jnp.exp2(m_old[h] - m_new[h]) for h in range(2)]
        p = [jnp.exp2(s[h] - jnp.concatenate([m_new[h]] * reps, axis=1)) for h in range(2)]
        for h in range(2):
            l_scr[h] = functools.reduce(jnp.add, lane_chunks(p[h]), alpha[h] * l_scr[h])
            m_scr[h] = m_new[h]
        pv = [_bdot(p[h], vb) for h in range(2)]
        acc_scr[...] = acc_scr[...] * jnp.where(first, alpha[0], alpha[1]) + jnp.where(first, pv[0], pv[1])

    n_full = jnp.minimum((q0 // CHUNK + 1) * CHUNK, s_len) // tk
    seen = jnp.minimum(((q0 + tq - 1) // CHUNK + 1) * CHUNK, s_len)
    n_diag = (seen - n_full * tk + tkd - 1) // tkd

    def full_body(kb, carry):
        block(kb * tk, tk, False)
        return carry

    def diag_body(i, carry):
        block(n_full * tk + i * tkd, tkd, True)
        return carry

    lax.fori_loop(0, n_full, full_body, 0)
    lax.fori_loop(0, n_diag, diag_body, 0)
    l = [jnp.sum(l_scr[h], axis=-1, keepdims=True) for h in range(2)]
    o_ref[0] = (acc_scr[...] / jnp.where(first, l[0], l[1])).astype(o_ref.dtype)


def mla_attn(q, k, v, past, s_len, tq, tk, tkd):
    B, _, T, _ = q.shape
    S = k.shape[2]
    assert S % tk == 0 and tk % tkd == 0 and tkd % LANES == 0 and T % tq == 0
    return pl.pallas_call(
        functools.partial(_mla_attn_kernel, tq, tk, tkd, past, s_len),
        grid=(B, PAIRS, T // tq),
        in_specs=[pl.BlockSpec((1, 2, tq, QK), lambda b, p, i: (b, p, i, 0)),
                  pl.BlockSpec((1, 2, S, QK), lambda b, p, i: (b, p, 0, 0)),
                  pl.BlockSpec((1, 1, S, LANES), lambda b, p, i: (b, p, 0, 0))],
        out_specs=pl.BlockSpec((1, tq, LANES), lambda b, p, i: (b, i, p)),
        out_shape=jax.ShapeDtypeStruct((B, T, WIDTH), BF16),
        scratch_shapes=[pltpu.VMEM((2, tq, LANES), F32), pltpu.VMEM((2, tq, LANES), F32),
                        pltpu.VMEM((tq, LANES), F32)],
        compiler_params=_params(("parallel", "parallel", "arbitrary")),
        name="mla_attn",
    )(q, k, v)


def _sb_kernel(tq, tk, past, q_ref, k_ref, v_ref, sum_ref, o_ref, carry_scr, acc_scr, qm_scr, z_scr):
    nk = k_ref.shape[1] // tk
    q0 = past + pl.program_id(2) * tq
    first = _iota2((1, LANES), 1) < HEAD_DIM
    carry_scr[...] = jnp.zeros(carry_scr.shape, F32)
    acc_scr[...] = jnp.zeros(acc_scr.shape, F32)
    qv = q_ref[0] * (HEAD_DIM ** -0.5 * LOG2E)
    qm_scr[0] = jnp.where(first, qv, 0.0).astype(BF16)
    qm_scr[1] = jnp.where(first, 0.0, qv).astype(BF16)

    def key_mask(kb):
        return (kb * tk + _iota2((1, tk), 1)) < (q0 + _iota2((tq, 1), 0))

    def stage_z(kb):
        kblk = k_ref[0, pl.ds(pl.multiple_of(kb * tk, tk), tk), :]
        return [_bdot_nt(qm_scr[h], kblk) for h in range(2)]

    def stage_w(z, mask):
        split, tot = [], []
        for read in z:
            x = read()
            neg_abs = lax.bitcast_convert_type(lax.bitcast_convert_type(x, jnp.uint32) | jnp.uint32(0x80000000), F32)
            w = jnp.maximum(x, 0.0) + jnp.log2(1.0 + jnp.exp2(neg_abs))
            if mask is not None:
                w = jnp.where(mask, w, 0.0)
            hi = w.astype(BF16)
            split.append(jnp.concatenate([hi, (w - hi.astype(F32)).astype(BF16)], axis=1))
            tot.append(jnp.sum(w, axis=-1, keepdims=True))
        return split, tot

    def stage_incl(split):
        return [jnp.dot(x, sum_ref[...], preferred_element_type=F32) for x in split]

    def stage_a(z, incl, c, mask):
        cb = [jnp.concatenate([x] * (tk // LANES), axis=1) for x in c]
        a = [jnp.exp2(z[h]() - incl[h] - cb[h]) for h in range(2)]
        if mask is not None:
            a = [jnp.where(mask, x, 0.0) for x in a]
        return a

    def stage_pv(a, kb):
        vblk = v_ref[0, pl.ds(pl.multiple_of(kb * tk, tk), tk), :]
        pv = [_bdot(x, vblk) for x in a]
        return jnp.where(first, pv[0], pv[1])

    def one_block(kb, masked):
        mask = key_mask(kb) if masked else None
        z = [lambda v=v: v for v in stage_z(kb)]
        split, tot = stage_w(z, mask)
        c = [carry_scr[h] for h in range(2)]
        a = stage_a(z, stage_incl(split), c, mask)
        acc_scr[...] += stage_pv(a, kb)
        for h in range(2):
            carry_scr[h] = c[h] + tot[h]

    def prefetch_z(kb, slot):
        for j in range(2):
            z = stage_z(jnp.maximum(kb - j, 0))
            for h in range(2):
                z_scr[slot, j, h] = z[h]

    def two_blocks(kb, slot):
        z1, z2 = ([lambda j=j, h=h: z_scr[slot, j, h] for h in range(2)] for j in range(2))
        split1, tot1 = stage_w(z1, None)
        incl1 = stage_incl(split1)
        prefetch_z(kb - 2, 1 - slot)
        split2, tot2 = stage_w(z2, None)
        incl2 = stage_incl(split2)
        c1 = [carry_scr[h] for h in range(2)]
        pv1 = stage_pv(stage_a(z1, incl1, c1, None), kb)
        c2 = [c1[h] + tot1[h] for h in range(2)]
        pv2 = stage_pv(stage_a(z2, incl2, c2, None), kb - 1)
        acc_scr[...] += pv1 + pv2
        for h in range(2):
            carry_scr[h] = c2[h] + tot2[h]

    n_full = q0 // tk
    last = jnp.clip((q0 + tq - 2) // tk, 0, nk - 1)

    def loop(count, fn):
        def body(i, carry):
            fn(i)
            return carry
        lax.fori_loop(0, count, body, 0)

    loop(last - n_full + 1, lambda i: one_block(last - i, True))
    def four_blocks(i):
        two_blocks(n_full - 1 - 4 * i, 0)
        two_blocks(n_full - 3 - 4 * i, 1)

    rest = n_full % 4
    prefetch_z(n_full - 1, 0)
    loop(n_full // 4, four_blocks)
    loop(rest // 2, lambda i: two_blocks(rest - 1, 0))
    loop(rest % 2, lambda i: one_block(0, False))
    o_ref[0] = acc_scr[...].astype(o_ref.dtype)


def sb_attn(q, k_all, v_all, past, tq, tk):
    B, T, _ = q.shape
    S = k_all.shape[1]
    assert S % tk == 0 and T % tq == 0 and tk % LANES == 0
    incl = np.arange(tk)[:, None] >= np.arange(tk)[None, :]
    sum_mat = jnp.asarray(np.concatenate([incl, incl], axis=0), BF16)
    return pl.pallas_call(
        functools.partial(_sb_kernel, tq, tk, past),
        grid=(B, PAIRS, T // tq),
        in_specs=[pl.BlockSpec((1, tq, LANES), lambda b, p, i: (b, i, p)),
                  pl.BlockSpec((1, S, LANES), lambda b, p, i: (b, 0, p)),
                  pl.BlockSpec((1, S, LANES), lambda b, p, i: (b, 0, p)),
                  pl.BlockSpec((2 * tk, tk), lambda b, p, i: (0, 0))],
        out_specs=pl.BlockSpec((1, tq, LANES), lambda b, p, i: (b, i, p)),
        out_shape=jax.ShapeDtypeStruct((B, T, WIDTH), BF16),
        scratch_shapes=[pltpu.VMEM((2, tq, LANES), F32), pltpu.VMEM((tq, LANES), F32),
                        pltpu.VMEM((2, tq, LANES), BF16), pltpu.VMEM((2, 2, 2, tq, tk), F32)],
        compiler_params=_params(("parallel", "parallel", "arbitrary")),
        name="sb_attn",
    )(q, k_all, v_all, sum_mat)


def _merge_kernel(x_ref, gm_ref, ya_ref, za_ref, yb_ref, zb_ref, yc_ref, zc_ref, g_ref,
                  wa_ref, wb_ref, wc_ref, wo_ref, o_ref):
    def branch(y_ref, z_ref, w_ref):
        return _bdot(y_ref[...].astype(F32) * _silu(z_ref[...].astype(F32)), w_ref[...])

    sg = _sigmoid(g_ref[...].astype(F32))
    merged = (sg[:, :D_MODEL] * branch(ya_ref, za_ref, wa_ref)
              + sg[:, D_MODEL:2 * D_MODEL] * branch(yb_ref, zb_ref, wb_ref)
              + sg[:, 2 * D_MODEL:] * branch(yc_ref, zc_ref, wc_ref))
    o_ref[...] = x_ref[...] + gm_ref[0] * _bdot(merged, wo_ref[...])


def merge_out(x2, gate_mod, ya, za, yb, zb, yc, zc, gates, wl, T, tm):
    M, D = x2.shape
    per = T // tm
    row = lambda wd: pl.BlockSpec((tm, wd), lambda i: (i, 0))
    const = lambda r, c: pl.BlockSpec((r, c), lambda i: (0, 0))
    return pl.pallas_call(
        _merge_kernel,
        grid=(M // tm,),
        in_specs=[row(D), pl.BlockSpec((1, 1, D), lambda i: (i // per, 0, 0))] + [row(WIDTH)] * 6 + [row(3 * D),
                  const(WIDTH, D), const(WIDTH, D), const(WIDTH, D), const(D, D)],
        out_specs=row(D),
        out_shape=jax.ShapeDtypeStruct((M, D), F32),
        compiler_params=_params(("parallel",)),
        name="merge_out",
    )(x2, gate_mod, ya, za, yb, zb, yc, zc, gates, wl["w_br_rwkv"], wl["w_br_mla"], wl["w_br_sb"], wl["w_out"])


def _rot_matrix(width, offset):
    half = ROPE // 2
    m = np.zeros((width, width), np.float32)
    for i in range(half):
        m[offset + half + i, offset + i] = -1.0
        m[offset + i, offset + half + i] = 1.0
    return jnp.asarray(m, BF16)


def _prep_layer(P, l):
    row = lambda a: a[l].reshape(1, -1)
    zeros_lora = jnp.zeros((LORA, WIDTH), F32)
    w_uq = P["mla_w_uq"][l].reshape(Q_LORA, HEADS, QK).transpose(1, 0, 2)
    w_ukv = P["mla_w_ukv"][l].reshape(KV_LORA, HEADS, 2 * HEAD_DIM)
    w_uk = jnp.pad(w_ukv[:, :, :NOPE].transpose(1, 0, 2), ((0, 0), (0, 0), (0, ROPE)))
    w_uv = w_ukv[:, :, NOPE:].reshape(KV_LORA, PAIRS, LANES).transpose(1, 0, 2)
    e96 = np.zeros((ROPE, QK), np.float32)
    e96[np.arange(ROPE), NOPE + np.arange(ROPE)] = 1.0
    return dict(
        norm_g=row(P["norm_g"]),
        mu=row(P["rwkv_mu"]), w0=row(P["rwkv_w0"]), a0=row(P["rwkv_a0"]), k_k=row(P["rwkv_k_k"]),
        k_a=row(P["rwkv_k_a"]), r_k=row(P["rwkv_r_k"]), gn_g=row(P["rwkv_gn_g"]), gn_b=row(P["rwkv_gn_b"]),
        w_up=jnp.concatenate([P["rwkv_w_up"][l], zeros_lora], axis=0).astype(BF16),
        a_up=jnp.concatenate([zeros_lora, P["rwkv_a_up"][l]], axis=0).astype(BF16),
        q_norm=row(P["mla_q_norm"]), w_uq=w_uq.astype(BF16),
        qn=jnp.concatenate([P["mla_qn_nope"][l], P["mla_qn_rope"][l]]).reshape(1, QK),
        kv_norm=row(P["mla_kv_norm"]), kn_rope=row(P["mla_kn_rope"]),
        kn_nope=jnp.pad(P["mla_kn_nope"][l], (0, ROPE)).reshape(1, QK),
        w_uk=w_uk.astype(BF16), w_uv=w_uv.astype(BF16),
        p96=_rot_matrix(QK, NOPE), p32=_rot_matrix(ROPE, 0), e96=jnp.asarray(e96, BF16),
        w_br_rwkv=P["w_br_rwkv"][l].astype(BF16), w_br_mla=P["w_br_mla"][l].astype(BF16),
        w_br_sb=P["w_br_sb"][l].astype(BF16), w_out=P["w_out"][l].astype(BF16),
    )


def _rope_tables(past, T):
    half = ROPE // 2
    inv = ROPE_THETA ** (-jnp.arange(half, dtype=F32) / half)
    ang = (past + jnp.arange(T, dtype=jnp.int32)).astype(F32)[:, None] * inv
    c32 = jnp.tile(jnp.cos(ang), (1, 2))
    s32 = jnp.tile(jnp.sin(ang), (1, 2))
    c96 = jnp.concatenate([jnp.ones((T, NOPE), F32), c32], axis=1)
    s96 = jnp.concatenate([jnp.zeros((T, NOPE), F32), s32], axis=1)
    return c96, s96, c32, s32


def _pad_rows(a, rows):
    return a if a.shape[1] == rows else jnp.pad(a, ((0, 0), (0, rows - a.shape[1])) + ((0, 0),) * (a.ndim - 2))


def _trunk(x, mods, pasts, layers, w_in_all, cfg):
    B, T, D = x.shape
    past = cfg["past"]
    S = past + T
    s_pad = -(-S // cfg["kv_mult"]) * cfg["kv_mult"]
    tabs = _rope_tables(past, T)
    x2 = x.reshape(B * T, D)
    new = []
    for l, wl in enumerate(layers):
        shift, scale, gate = (mods[l][:, None, i * D:(i + 1) * D] for i in range(3))
        outs = in_proj(x2, scale, shift, wl["norm_g"], w_in_all, l, T, cfg["tm"])
        pr, zA, cq, ckv, kr, zB, sq, sk, sv, zC, gates, sk_b, sv_b = outs
        seq = lambda a: a.reshape(B, T, a.shape[-1])
        if pasts is None:
            s0 = jnp.zeros((B, PAIRS, LANES, LANES), F32)
            shift0 = jnp.zeros((B, 1, SHIFT_W), F32)
        else:
            s0, shift0 = _state_to_bd(pasts[l][0]), pasts[l][1]
        pr3 = seq(pr)
        yA, s_fin = rwkv_branch(pr3, shift0, s0, wl, cfg["rwkv_n"])
        q, ckv_n, k_rope = mla_q(seq(cq), seq(ckv), seq(kr), tabs, wl, cfg["tm_q"])
        sk3, sv3 = seq(sk), seq(sv)
        if pasts is None:
            ckv_all, kr_all, k_all, v_all = ckv_n, k_rope, seq(sk_b), seq(sv_b)
        else:
            _, _, ckv0, kr0, k0, v0 = pasts[l]
            ckv_all = jnp.concatenate([ckv0, ckv_n], axis=1)
            kr_all = jnp.concatenate([kr0, k_rope], axis=1)
            k_all = jnp.concatenate([k0.reshape(B, past, WIDTH).astype(BF16), seq(sk_b)], axis=1)
            v_all = jnp.concatenate([v0.reshape(B, past, WIDTH).astype(BF16), seq(sv_b)], axis=1)
        ckv_all, kr_all, k_all, v_all = (_pad_rows(a, s_pad) for a in (ckv_all, kr_all, k_all, v_all))
        kf, vf = mla_kv(ckv_all, kr_all, wl, cfg["ts"])
        yB = mla_attn(q, kf, vf, past, S, cfg["tq"], cfg["tk"], cfg["tkd"])
        yC = sb_attn(seq(sq), k_all, v_all, past, cfg["sb_tq"], cfg["sb_tk"])
        x2 = merge_out(x2, gate, yA.reshape(B * T, WIDTH), zA, yB.reshape(B * T, WIDTH), zB,
                       yC.reshape(B * T, WIDTH), zC, gates, wl, T, cfg["tm"])
        new.append((_bd_to_state(s_fin), pr3[:, -1:], ckv_n, k_rope,
                    sk3.reshape(B, T, HEADS, HEAD_DIM), sv3.reshape(B, T, HEADS, HEAD_DIM)))
    stacked = [jnp.stack([st[i] for st in new], axis=0) for i in range(6)]
    return x2.reshape(B, T, D), stacked


PROMPT_CFG = dict(past=0, tm=256, rwkv_n=128, tm_q=256, ts=256, tq=512, tk=1024, tkd=512, sb_tq=512, sb_tk=256,
                  kv_mult=1024)
SAMPLE_CFG = dict(past=2048, tm=16, rwkv_n=16, tm_q=16, ts=2560, tq=16, tk=2560, tkd=2560, sb_tq=16, sb_tk=512,
                  kv_mult=512)


def kernel(x_prompt, x_sample, state_rwkv_wkv, state_rwkv_shift, cache_mla_ckv, cache_mla_krope, cache_sb_k, cache_sb_v, c_prompt, c_sample, w_ada, b_ada, norm_g, w_in, rwkv_mu, rwkv_w0, rwkv_w_up, rwkv_a0, rwkv_a_up, rwkv_k_k, rwkv_k_a, rwkv_r_k, rwkv_gn_g, rwkv_gn_b, w_br_rwkv, mla_q_norm, mla_w_uq, mla_kv_norm, mla_w_ukv, mla_qn_nope, mla_qn_rope, mla_kn_nope, mla_kn_rope, w_br_mla, w_br_sb, w_out):
    P = dict(w_in=w_in, norm_g=norm_g, rwkv_mu=rwkv_mu, rwkv_w0=rwkv_w0, rwkv_w_up=rwkv_w_up, rwkv_a0=rwkv_a0,
             rwkv_a_up=rwkv_a_up, rwkv_k_k=rwkv_k_k, rwkv_k_a=rwkv_k_a, rwkv_r_k=rwkv_r_k, rwkv_gn_g=rwkv_gn_g,
             rwkv_gn_b=rwkv_gn_b, w_br_rwkv=w_br_rwkv, mla_q_norm=mla_q_norm, mla_w_uq=mla_w_uq,
             mla_kv_norm=mla_kv_norm, mla_w_ukv=mla_w_ukv, mla_qn_nope=mla_qn_nope, mla_qn_rope=mla_qn_rope,
             mla_kn_nope=mla_kn_nope, mla_kn_rope=mla_kn_rope, w_br_mla=w_br_mla, w_br_sb=w_br_sb, w_out=w_out)
    depth = w_in.shape[0]
    bp = x_prompt.shape[0]
    layers = [_prep_layer(P, l) for l in range(depth)]
    offs = np.cumsum([0] + [wd for _, wd in IN_GROUPS])
    src = {name: (offs[i], offs[i + 1]) for i, (name, _) in enumerate(IN_GROUPS)}
    w_in_all = jnp.concatenate([w_in[:, :, src[name][0]:src[name][1]] for name, _ in _in_weight_layout()[0]],
                               axis=-1).astype(BF16)
    mods = ada_mod(jnp.concatenate([c_prompt, c_sample], axis=0), w_ada, b_ada)
    y_p, st_p = _trunk(x_prompt, mods[:, :bp], None, layers, w_in_all, PROMPT_CFG)
    pasts = [(state_rwkv_wkv[l], state_rwkv_shift[l], cache_mla_ckv[l], cache_mla_krope[l],
              cache_sb_k[l], cache_sb_v[l]) for l in range(depth)]
    assert cache_mla_ckv.shape[2] == SAMPLE_CFG["past"]
    y_s, st_s = _trunk(x_sample, mods[:, bp:], pasts, layers, w_in_all, SAMPLE_CFG)
    return (y_p, y_s, *st_p, *st_s)
```

```python
import functools

import numpy as np
import jax
import jax.numpy as jnp
from jax import lax
from jax.experimental import pallas as pl
from jax.experimental.pallas import tpu as pltpu

F32 = jnp.float32
BF16 = jnp.bfloat16

D_MODEL = 1024
CHUNK = 64
RMS_EPS = 1e-6
GN_EPS = 64e-5
NEG_BIG = -1e30
ROPE_THETA = 10000.0
LOG2E = 1.4426950408889634

HEADS = 8
HEAD_DIM = 64
WIDTH = HEADS * HEAD_DIM
PAIRS = HEADS // 2
LANES = 128
LORA = 64
SHIFT_W = 3 * WIDTH + 2 * LORA
Q_LORA = 384
KV_LORA = 256
ROPE = 32
NOPE = 64
QK = NOPE + ROPE
RWKV_CHUNK = 64

IN_GROUPS = (("pr", SHIFT_W), ("zA", WIDTH), ("cq", Q_LORA), ("ckv", KV_LORA), ("kr", ROPE),
             ("zB", WIDTH), ("sq", WIDTH), ("sk", WIDTH), ("sv", WIDTH), ("zC", WIDTH),
             ("gates", 3 * D_MODEL))
BF16_COPIES = ("sk", "sv")
BF16_GROUPS = ("zA", "zB", "zC", "gates")

VMEM_LIMIT = 56 * 1024 * 1024


def _params(sem):
    return pltpu.CompilerParams(dimension_semantics=sem, vmem_limit_bytes=VMEM_LIMIT)


def _bdot(a, b):
    return jnp.dot(a.astype(BF16), b.astype(BF16), preferred_element_type=F32)


def _bdot_nt(a, b):
    return lax.dot_general(a.astype(BF16), b.astype(BF16), (((1,), (1,)), ((), ())),
                           preferred_element_type=F32)


def _bdot_tn(a, b):
    return lax.dot_general(a.astype(BF16), b.astype(BF16), (((0,), (0,)), ((), ())),
                           preferred_element_type=F32)


def _split(x, n):
    parts, r = [], x
    for _ in range(n):
        p = r.astype(BF16)
        parts.append(p)
        r = r - p.astype(F32)
    return parts


def _xdot_r(x, m01, n):
    m = m01.astype(BF16)
    return sum(jnp.dot(p, m, preferred_element_type=F32) for p in _split(x, n))


def _xdot_l(m01, x, n):
    m = m01.astype(BF16)
    return sum(jnp.dot(m, p, preferred_element_type=F32) for p in _split(x, n))


def _sigmoid(x):
    return 1.0 / (1.0 + jnp.exp(-x))


def _silu(x):
    return x * _sigmoid(x)


def _softplus(x):
    return jnp.maximum(x, 0.0) + jnp.log(1.0 + jnp.exp(-jnp.abs(x)))


def _rms(x, g):
    return x * lax.rsqrt(jnp.mean(x * x, axis=-1, keepdims=True) + RMS_EPS) * g


def _iota2(shape, dim):
    return lax.broadcasted_iota(jnp.int32, shape, dim)


def _ada_kernel(c_ref, w_ref, b_ref, o_ref):
    o_ref[0] = _bdot(_silu(c_ref[...]), w_ref[0]) + b_ref[0]


def ada_mod(c, w_ada, b_ada):
    L, D, N = w_ada.shape
    rows = c.shape[0]
    tn = 1024
    return pl.pallas_call(
        _ada_kernel,
        grid=(L, N // tn),
        in_specs=[pl.BlockSpec((rows, D), lambda l, j: (0, 0)),
                  pl.BlockSpec((1, D, tn), lambda l, j: (l, 0, j)),
                  pl.BlockSpec((1, 1, tn), lambda l, j: (l, 0, j))],
        out_specs=pl.BlockSpec((1, rows, tn), lambda l, j: (l, 0, j)),
        out_shape=jax.ShapeDtypeStruct((L, rows, N), F32),
        compiler_params=_params(("parallel", "parallel")),
        name="ada_mod",
    )(c, w_ada, b_ada.reshape(L, 1, N))


def _in_weight_layout():
    order = [g for g in IN_GROUPS if g[0] != "kr"] + [g for g in IN_GROUPS if g[0] == "kr"]
    offs, pos = {}, 0
    for name, wd in order:
        offs[name] = pos
        pos += wd
    return order, offs


def _inproj_kernel(x_ref, sc_ref, sh_ref, g_ref, w_ref, *outs):
    n = len(IN_GROUPS)
    offs = _in_weight_layout()[1]
    h = _rms(x_ref[...], g_ref[...]) * (1.0 + sc_ref[0]) + sh_ref[0]
    hb = h.astype(BF16)
    extra = iter(outs[n:])
    for (name, wd), o in zip(IN_GROUPS, outs):
        res = jnp.dot(hb, w_ref[0, :, offs[name]:offs[name] + wd], preferred_element_type=F32)
        o[...] = res.astype(o.dtype)
        if name in BF16_COPIES:
            next(extra)[...] = res.astype(BF16)


def in_proj(x2, scale, shift, g, w_all, layer, T, tm):
    M, D = x2.shape
    assert T % tm == 0 and M % tm == 0
    per = T // tm
    row = lambda i: (i, 0)
    mod = lambda i: (i // per, 0, 0)
    const = lambda i: (0, 0)
    in_specs = [pl.BlockSpec((tm, D), row), pl.BlockSpec((1, 1, D), mod), pl.BlockSpec((1, 1, D), mod),
                pl.BlockSpec((1, D), const),
                pl.BlockSpec((1, D, w_all.shape[2]), lambda i: (layer, 0, 0), pipeline_mode=pl.Buffered(1))]
    out_specs = [pl.BlockSpec((tm, wd), row) for _, wd in IN_GROUPS]
    out_shape = [jax.ShapeDtypeStruct((M, wd), BF16 if name in BF16_GROUPS else F32) for name, wd in IN_GROUPS]
    out_specs += [pl.BlockSpec((tm, WIDTH), row) for _ in BF16_COPIES]
    out_shape += [jax.ShapeDtypeStruct((M, WIDTH), BF16) for _ in BF16_COPIES]
    return pl.pallas_call(
        _inproj_kernel, grid=(M // tm,), in_specs=in_specs, out_specs=out_specs, out_shape=out_shape,
        compiler_params=_params(("parallel",)), name="in_proj",
    )(x2, scale, shift, g, w_all)


def _rwkv_kernel(n, pr_ref, past_ref, s0_ref, mu_ref, w0_ref, a0_ref, kk_ref, ka_ref, rk_ref,
                 gng_ref, gnb_ref, wup_ref, aup_ref, y_ref, sfin_ref, st_scr, last_scr):
    tb = pl.program_id(1)

    @pl.when(tb == 0)
    def _():
        st_scr[...] = s0_ref[0]
        last_scr[...] = past_ref[0]

    p = pr_ref[0]
    row = _iota2((n, 1), 0)
    prev = jnp.where(row == 0, last_scr[...], pltpu.roll(p, 1, axis=0))
    last_scr[...] = p[n - 1:n, :]
    xs = p + (prev - p) * mu_ref[...]
    r, k, v = xs[:, :WIDTH], xs[:, WIDTH:2 * WIDTH], xs[:, 2 * WIDTH:3 * WIDTH]
    wa = xs[:, 3 * WIDTH:]
    w_log = -_softplus(-(w0_ref[...] + _bdot(jnp.tanh(wa), wup_ref[...]))) - 0.5
    lw = -jnp.exp(w_log)
    a = _sigmoid(a0_ref[...] + _bdot(wa, aup_ref[...]))

    ri, ci = _iota2((LANES, LANES), 0), _iota2((LANES, LANES), 1)
    seg64 = ((ri >> 6) == (ci >> 6)).astype(BF16)
    bd2 = ((ri >> 6) == (ci >> 6)).astype(F32)

    def segsum(t):
        return jnp.concatenate(
            [_xdot_r(t[:, LANES * q:LANES * (q + 1)], seg64, 2) for q in range(PAIRS)], axis=-1)

    kkr = k * kk_ref[...]
    kk = kkr * jnp.minimum(lax.rsqrt(segsum(kkr * kkr)), 1e12)
    k2 = k * (1.0 + (a - 1.0) * ka_ref[...])
    bonus = segsum(r * k2 * rk_ref[...]) * v

    chunk = min(RWKV_CHUNK, n)
    shift = chunk.bit_length() - 1
    tr, tc = _iota2((n, n), 0), _iota2((n, n), 1)
    same = (tr >> shift) == (tc >> shift)
    strict = (same & (tr > tc)).astype(F32)
    incl = (same & (tr >= tc)).astype(F32)
    eye = (tr == tc).astype(F32)
    lw_parts = _split(lw, 2)
    cum = lambda m01: sum(jnp.dot(m01.astype(BF16), part, preferred_element_type=F32) for part in lw_parts)
    g_log = cum(incl)
    g_end = cum(same)
    e_in, e_inv, e_rest = jnp.exp(g_log), jnp.exp(-g_log), jnp.exp(g_end - g_log)
    r_t = r * e_in
    kk_t = kk * jnp.exp(g_log - lw)
    kka = kk * a
    k_t, b_t = k2 * e_inv, kka * e_inv
    k_e, b_e = k2 * e_rest, kka * e_rest
    dec = jnp.exp(g_end)

    lane = _iota2((1, LANES), 1)
    head_mask = [(lane < HEAD_DIM).astype(F32), (lane >= HEAD_DIM).astype(F32)]

    sls = [slice(LANES * q, LANES * (q + 1)) for q in range(PAIRS)]
    units = [(q, hm) for q in range(PAIRS) for hm in head_mask]
    rp, kkp, vp = [r_t[:, s] for s in sls], [kk_t[:, s] for s in sls], [v[:, s] for s in sls]
    kb, bb = [k_t[:, s].astype(BF16) for s in sls], [b_t[:, s].astype(BF16) for s in sls]
    vb, kkb = [x.astype(BF16) for x in vp], [x.astype(BF16) for x in kkp]
    kkm = [(kkp[q] * hm).astype(BF16) for q, hm in units]
    rm = [(rp[q] * hm).astype(BF16) for q, hm in units]
    l_b = [_bdot_nt(kkm[u], bb[q]) * strict for u, (q, _) in enumerate(units)]
    l_k = [_bdot_nt(kkm[u], kb[q]) * strict for u, (q, _) in enumerate(units)]
    r_k = [_bdot_nt(rm[u], kb[q]) * incl for u, (q, _) in enumerate(units)]
    r_b = [_bdot_nt(rm[u], bb[q]) * incl for u, (q, _) in enumerate(units)]
    sub = min(16, chunk)
    sub_shift = sub.bit_length() - 1
    inner = ((tr >> sub_shift) == (tc >> sub_shift)).astype(F32)
    pw = [x * inner for x in l_b]
    t_inv = [eye - x for x in pw]
    for _ in range(sub_shift - 1):
        pw = [_bdot(x, x) for x in pw]
        t_inv = [t + _bdot(t, x) for t, x in zip(t_inv, pw)]
    for level in range(sub_shift, shift):
        joined = (((tr >> (level + 1)) == (tc >> (level + 1))) & ((tr >> level) != (tc >> level))).astype(F32)
        t_inv = [t - _bdot(_bdot(t, x * joined), t) for t, x in zip(t_inv, l_b)]
    lkv = [_bdot(l_k[u], vb[q]) for u, (q, _) in enumerate(units)]
    w_h = [_bdot(t_inv[u], kkb[q]) for u, (q, _) in enumerate(units)]
    u0_h = [_bdot(t, x) for t, x in zip(t_inv, lkv)]
    q_h = [rp[q] - _bdot(r_b[u], w_h[u]) for u, (q, _) in enumerate(units)]
    y0_h = [_bdot(r_k[u], vb[q]) - _bdot(r_b[u], u0_h[u]) for u, (q, _) in enumerate(units)]
    first = head_mask[0] > 0.5
    pair = lambda xs: [jnp.where(first, xs[2 * q], xs[2 * q + 1]) for q in range(PAIRS)]
    w_c, u0_c, q_c, y0_c = pair(w_h), pair(u0_h), pair(q_h), pair(y0_h)

    st = [st_scr[q] for q in range(PAIRS)]
    ys = [[] for _ in range(PAIRS)]
    rows = [slice(chunk * c, chunk * (c + 1)) for c in range(n // chunk)]
    n_c = [[(_bdot_tn(vp[q][rs], k_e[rs, sls[q]]) - _bdot_tn(u0_c[q][rs], b_e[rs, sls[q]])) * bd2
            for q in range(PAIRS)] for rs in rows]
    a_c = [[_bdot_tn(w_c[q][rs], b_e[rs, sls[q]]) * bd2 for q in range(PAIRS)] for rs in rows]
    for c, rs in enumerate(rows):
        for q in range(PAIRS):
            ys[q].append(_bdot_nt(q_c[q][rs], st[q]) + y0_c[q][rs])
        sa = [_bdot(st[q], a_c[c][q]) for q in range(PAIRS)]
        st = [st[q] * dec[chunk * c:chunk * c + 1, sls[q]] + (n_c[c][q] - sa[q]) for q in range(PAIRS)]
    for q in range(PAIRS):
        st_scr[q] = st[q]
    y = [x[0] if len(x) == 1 else jnp.concatenate(x, axis=0) for x in ys]
    mean = [_xdot_r(x, seg64, 2) * (1.0 / HEAD_DIM) for x in y]
    d = [x - m for x, m in zip(y, mean)]
    var = [_xdot_r(x * x, seg64, 2) * (1.0 / HEAD_DIM) for x in d]
    for q, s in enumerate(sls):
        yn = d[q] * lax.rsqrt(var[q] + GN_EPS) * gng_ref[:, s] + gnb_ref[:, s] + bonus[:, s]
        y_ref[0, :, s] = yn.astype(y_ref.dtype)

    @pl.when(tb == pl.num_programs(1) - 1)
    def _():
        sfin_ref[0] = st_scr[...]


def rwkv_branch(pr, past_shift, s0_bd, wl, n):
    B, T, _ = pr.shape
    assert T % n == 0 and n % min(RWKV_CHUNK, n) == 0 and n & (n - 1) == 0
    vec = lambda wd: pl.BlockSpec((1, wd), lambda b, t: (0, 0))
    in_specs = [pl.BlockSpec((1, n, SHIFT_W), lambda b, t: (b, t, 0)),
                pl.BlockSpec((1, 1, SHIFT_W), lambda b, t: (b, 0, 0)),
                pl.BlockSpec((1, PAIRS, LANES, LANES), lambda b, t: (b, 0, 0, 0)),
                vec(SHIFT_W)] + [vec(WIDTH)] * 7 + [
                pl.BlockSpec((LANES, WIDTH), lambda b, t: (0, 0)),
                pl.BlockSpec((LANES, WIDTH), lambda b, t: (0, 0))]
    return pl.pallas_call(
        functools.partial(_rwkv_kernel, n),
        grid=(B, T // n),
        in_specs=in_specs,
        out_specs=[pl.BlockSpec((1, n, WIDTH), lambda b, t: (b, t, 0)),
                   pl.BlockSpec((1, PAIRS, LANES, LANES), lambda b, t: (b, 0, 0, 0))],
        out_shape=[jax.ShapeDtypeStruct((B, T, WIDTH), BF16),
                   jax.ShapeDtypeStruct((B, PAIRS, LANES, LANES), F32)],
        scratch_shapes=[pltpu.VMEM((PAIRS, LANES, LANES), F32), pltpu.VMEM((1, SHIFT_W), F32)],
        compiler_params=_params(("parallel", "arbitrary")),
        name="rwkv7",
    )(pr, past_shift, s0_bd, wl["mu"], wl["w0"], wl["a0"], wl["k_k"], wl["k_a"], wl["r_k"],
      wl["gn_g"], wl["gn_b"], wl["w_up"], wl["a_up"])


def _state_to_bd(s):
    B = s.shape[0]
    s = s.reshape(B, PAIRS, 2, HEAD_DIM, HEAD_DIM)
    z = jnp.zeros_like(s[:, :, 0])
    top = jnp.concatenate([s[:, :, 0], z], axis=-1)
    bot = jnp.concatenate([z, s[:, :, 1]], axis=-1)
    return jnp.concatenate([top, bot], axis=-2)


def _bd_to_state(s):
    B = s.shape[0]
    h0 = s[:, :, :HEAD_DIM, :HEAD_DIM]
    h1 = s[:, :, HEAD_DIM:, HEAD_DIM:]
    return jnp.stack([h0, h1], axis=2).reshape(B, HEADS, HEAD_DIM, HEAD_DIM)


def _mla_q_kernel(cq_ref, ckv_ref, kr_ref, c96_ref, s96_ref, c32_ref, s32_ref, gq_ref, wuq_ref, gqn_ref,
                  gkv_ref, gkr_ref, p96_ref, p32_ref, q_ref, ckvn_ref, krope_ref):
    cqb = _rms(cq_ref[0], gq_ref[...]).astype(BF16)
    nope = _iota2((1, QK), 1) < NOPE
    c96, s96 = c96_ref[...], s96_ref[...]
    for h in range(HEADS):
        qh = jnp.dot(cqb, wuq_ref[h], preferred_element_type=F32)
        sq = qh * qh
        ss_n = jnp.sum(jnp.where(nope, sq, 0.0), axis=-1, keepdims=True)
        ss_r = jnp.sum(jnp.where(nope, 0.0, sq), axis=-1, keepdims=True)
        inv = jnp.where(nope, lax.rsqrt(ss_n * (1.0 / NOPE) + RMS_EPS), lax.rsqrt(ss_r * (1.0 / ROPE) + RMS_EPS))
        qn = qh * inv * gqn_ref[...]
        qo = qn * c96 + _xdot_r(qn, p96_ref[...], 3) * s96
        q_ref[0, h] = (qo * (QK ** -0.5 * LOG2E)).astype(BF16)
    ckvn_ref[0] = _rms(ckv_ref[0], gkv_ref[...])
    krn = _rms(kr_ref[0], gkr_ref[...])
    krope_ref[0] = krn * c32_ref[...] + _xdot_r(krn, p32_ref[...], 3) * s32_ref[...]


def mla_q(cq, ckv, kr, tabs, wl, tm):
    B, T, _ = cq.shape
    c96, s96, c32, s32 = tabs
    tok = lambda wd: pl.BlockSpec((1, tm, wd), lambda b, t: (b, t, 0))
    tab = lambda wd: pl.BlockSpec((tm, wd), lambda b, t: (t, 0))
    vec = lambda wd: pl.BlockSpec((1, wd), lambda b, t: (0, 0))
    return pl.pallas_call(
        _mla_q_kernel,
        grid=(B, T // tm),
        in_specs=[tok(Q_LORA), tok(KV_LORA), tok(ROPE), tab(QK), tab(QK), tab(ROPE), tab(ROPE),
                  vec(Q_LORA), pl.BlockSpec((HEADS, Q_LORA, QK), lambda b, t: (0, 0, 0)), vec(QK),
                  vec(KV_LORA), vec(ROPE),
                  pl.BlockSpec((QK, QK), lambda b, t: (0, 0)), pl.BlockSpec((ROPE, ROPE), lambda b, t: (0, 0))],
        out_specs=[pl.BlockSpec((1, HEADS, tm, QK), lambda b, t: (b, 0, t, 0)), tok(KV_LORA), tok(ROPE)],
        out_shape=[jax.ShapeDtypeStruct((B, HEADS, T, QK), BF16),
                   jax.ShapeDtypeStruct((B, T, KV_LORA), F32),
                   jax.ShapeDtypeStruct((B, T, ROPE), F32)],
        compiler_params=_params(("parallel", "parallel")),
        name="mla_q",
    )(cq, ckv, kr, c96, s96, c32, s32, wl["q_norm"], wl["w_uq"], wl["qn"], wl["kv_norm"], wl["kn_rope"],
      wl["p96"], wl["p32"])


def _mla_kv_kernel(ckv_ref, kr_ref, wuk_ref, wuv_ref, gkn_ref, e_ref, k_ref, v_ref):
    cb = ckv_ref[0].astype(BF16)
    k_rope = _xdot_r(kr_ref[0], e_ref[...], 3)
    for h in range(HEADS):
        kh = jnp.dot(cb, wuk_ref[h], preferred_element_type=F32)
        ms = jnp.sum(kh * kh, axis=-1, keepdims=True) * (1.0 / NOPE)
        k_ref[0, h] = (kh * lax.rsqrt(ms + RMS_EPS) * gkn_ref[...] + k_rope).astype(BF16)
    for q in range(PAIRS):
        v_ref[0, q] = jnp.dot(cb, wuv_ref[q], preferred_element_type=F32).astype(BF16)


def mla_kv(ckv_all, kr_all, wl, ts):
    B, S, _ = ckv_all.shape
    return pl.pallas_call(
        _mla_kv_kernel,
        grid=(B, S // ts),
        in_specs=[pl.BlockSpec((1, ts, KV_LORA), lambda b, t: (b, t, 0)),
                  pl.BlockSpec((1, ts, ROPE), lambda b, t: (b, t, 0)),
                  pl.BlockSpec((HEADS, KV_LORA, QK), lambda b, t: (0, 0, 0)),
                  pl.BlockSpec((PAIRS, KV_LORA, LANES), lambda b, t: (0, 0, 0)),
                  pl.BlockSpec((1, QK), lambda b, t: (0, 0)),
                  pl.BlockSpec((ROPE, QK), lambda b, t: (0, 0))],
        out_specs=[pl.BlockSpec((1, HEADS, ts, QK), lambda b, t: (b, 0, t, 0)),
                   pl.BlockSpec((1, PAIRS, ts, LANES), lambda b, t: (b, 0, t, 0))],
        out_shape=[jax.ShapeDtypeStruct((B, HEADS, S, QK), BF16),
                   jax.ShapeDtypeStruct((B, PAIRS, S, LANES), BF16)],
        compiler_params=_params(("parallel", "parallel")),
        name="mla_kv",
    )(ckv_all, kr_all, wl["w_uk"], wl["w_uv"], wl["kn_nope"], wl["e96"])


def _mla_attn_kernel(tq, tk, tkd, past, s_len, q_ref, k_ref, v_ref, o_ref, m_scr, l_scr, acc_scr):
    q0 = past + pl.program_id(2) * tq
    first = _iota2((1, LANES), 1) < HEAD_DIM
    m_scr[...] = jnp.full(m_scr.shape, NEG_BIG, F32)
    l_scr[...] = jnp.zeros(l_scr.shape, F32)
    acc_scr[...] = jnp.zeros(acc_scr.shape, F32)
    qs = [q_ref[0, h] for h in range(2)]

    def block(start, width, masked):
        reps = width // LANES
        lane_chunks = lambda x: [x[:, LANES * c:LANES * (c + 1)] for c in range(reps)]
        ks = pl.multiple_of(start, width)
        s = [_bdot_nt(qs[h], k_ref[0, h, pl.ds(ks, width), :]) for h in range(2)]
        if masked:
            limit = jnp.minimum((((q0 + _iota2((tq, 1), 0)) >> 6) + 1) << 6, s_len)
            visible = (ks + _iota2((1, width), 1)) < limit
            s = [jnp.where(visible, x, NEG_BIG) for x in s]
        vb = v_ref[0, 0, pl.ds(ks, width), :]
        m_old = [m_scr[h] for h in range(2)]
        m_blk = [functools.reduce(jnp.maximum, lane_chunks(x)) for x in s]
        m_new = [jnp.maximum(m_old[h], jnp.max(m_blk[h], axis=-1, keepdims=True)) for h in range(2)]
        alpha = [jnp.exp2(m_old[h] - m_new[h]) for h in range(2)]
        p = [jnp.exp2(s[h] - jnp.concatenate([m_new[h]] * reps, axis=1)) for h in range(2)]
        for h in range(2):
            l_scr[h] = functools.reduce(jnp.add, lane_chunks(p[h]), alpha[h] * l_scr[h])
            m_scr[h] = m_new[h]
        pv = _bdot(jnp.concatenate(p, axis=0), vb)
        acc_scr[...] = acc_scr[...] * jnp.where(first, alpha[0], alpha[1]) + jnp.where(first, pv[:tq], pv[tq:])

    n_full = jnp.minimum((q0 // CHUNK + 1) * CHUNK, s_len) // tk
    seen = jnp.minimum(((q0 + tq - 1) // CHUNK + 1) * CHUNK, s_len)
    n_diag = (seen - n_full * tk + tkd - 1) // tkd

    def full_body(kb, carry):
        block(kb * tk, tk, False)
        return carry

    def diag_body(i, carry):
        block(n_full * tk + i * tkd, tkd, True)
        return carry

    lax.fori_loop(0, n_full, full_body, 0)
    lax.fori_loop(0, n_diag, diag_body, 0)
    l = [jnp.sum(l_scr[h], axis=-1, keepdims=True) for h in range(2)]
    o_ref[0] = (acc_scr[...] / jnp.where(first, l[0], l[1])).astype(o_ref.dtype)


def mla_attn(q, k, v, past, s_len, tq, tk, tkd):
    B, _, T, _ = q.shape
    S = k.shape[2]
    assert S % tk == 0 and tk % tkd == 0 and tkd % LANES == 0 and T % tq == 0
    return pl.pallas_call(
        functools.partial(_mla_attn_kernel, tq, tk, tkd, past, s_len),
        grid=(B, PAIRS, T // tq),
        in_specs=[pl.BlockSpec((1, 2, tq, QK), lambda b, p, i: (b, p, i, 0)),
                  pl.BlockSpec((1, 2, S, QK), lambda b, p, i: (b, p, 0, 0)),
                  pl.BlockSpec((1, 1, S, LANES), lambda b, p, i: (b, p, 0, 0))],
        out_specs=pl.BlockSpec((1, tq, LANES), lambda b, p, i: (b, i, p)),
        out_shape=jax.ShapeDtypeStruct((B, T, WIDTH), BF16),
        scratch_shapes=[pltpu.VMEM((2, tq, LANES), F32), pltpu.VMEM((2, tq, LANES), F32),
                        pltpu.VMEM((tq, LANES), F32)],
        compiler_params=_params(("parallel", "parallel", "arbitrary")),
        name="mla_attn",
    )(q, k, v)


def _sb_kernel(tq, tk, past, q_ref, k_ref, v_ref, sum_ref, o_ref, carry_scr, acc_scr, qm_scr, z_scr):
    nk = k_ref.shape[1] // tk
    q0 = past + pl.program_id(2) * tq
    first = _iota2((1, LANES), 1) < HEAD_DIM
    carry_scr[...] = jnp.zeros(carry_scr.shape, F32)
    acc_scr[...] = jnp.zeros(acc_scr.shape, F32)
    qv = q_ref[0] * (HEAD_DIM ** -0.5 * LOG2E)
    qm_scr[0] = jnp.where(first, qv, 0.0).astype(BF16)
    qm_scr[1] = jnp.where(first, 0.0, qv).astype(BF16)

    def key_mask(kb):
        return (kb * tk + _iota2((1, tk), 1)) < (q0 + _iota2((tq, 1), 0))

    def stage_z(kb):
        kblk = k_ref[0, pl.ds(pl.multiple_of(kb * tk, tk), tk), :]
        z = _bdot_nt(qm_scr[...].reshape(2 * tq, LANES), kblk)
        return [z[:tq], z[tq:]]

    def stage_w(z, mask):
        split = []
        for read in z:
            x = read()
            neg_abs = lax.bitcast_convert_type(lax.bitcast_convert_type(x, jnp.uint32) | jnp.uint32(0x80000000), F32)
            w = jnp.maximum(x, 0.0) + jnp.log2(1.0 + jnp.exp2(neg_abs))
            if mask is not None:
                w = jnp.where(mask, w, 0.0)
            hi = w.astype(BF16)
            split.append(jnp.concatenate([hi, (w - hi.astype(F32)).astype(BF16)], axis=1))
        return split

    def stage_incl(split):
        both = jnp.dot(jnp.concatenate(split, axis=0), sum_ref[...], preferred_element_type=F32)
        incl = [both[:tq], both[tq:]]
        return incl, [x[:, 0:1] for x in incl]

    def stage_a(z, incl, c, mask):
        cb = [jnp.concatenate([x] * (tk // LANES), axis=1) for x in c]
        a = [jnp.exp2(z[h]() - incl[h] - cb[h]) for h in range(2)]
        if mask is not None:
            a = [jnp.where(mask, x, 0.0) for x in a]
        return a

    def stage_pv(a, kb):
        vblk = v_ref[0, pl.ds(pl.multiple_of(kb * tk, tk), tk), :]
        pv = _bdot(jnp.concatenate(a, axis=0), vblk)
        return jnp.where(first, pv[:tq], pv[tq:])

    def one_block(kb, masked):
        mask = key_mask(kb) if masked else None
        z = [lambda v=v: v for v in stage_z(kb)]
        incl, tot = stage_incl(stage_w(z, mask))
        c = [carry_scr[h] for h in range(2)]
        a = stage_a(z, incl, c, mask)
        acc_scr[...] += stage_pv(a, kb)
        for h in range(2):
            carry_scr[h] = c[h] + tot[h]

    def prefetch_z(kb, slot):
        for j in range(2):
            z = stage_z(jnp.maximum(kb - j, 0))
            for h in range(2):
                z_scr[slot, j, h] = z[h]

    def two_blocks(kb, slot):
        z1, z2 = ([lambda j=j, h=h: z_scr[slot, j, h] for h in range(2)] for j in range(2))
        incl1, tot1 = stage_incl(stage_w(z1, None))
        prefetch_z(kb - 2, 1 - slot)
        incl2, tot2 = stage_incl(stage_w(z2, None))
        c1 = [carry_scr[h] for h in range(2)]
        pv1 = stage_pv(stage_a(z1, incl1, c1, None), kb)
        c2 = [c1[h] + tot1[h] for h in range(2)]
        pv2 = stage_pv(stage_a(z2, incl2, c2, None), kb - 1)
        acc_scr[...] += pv1 + pv2
        for h in range(2):
            carry_scr[h] = c2[h] + tot2[h]

    n_full = q0 // tk
    last = jnp.clip((q0 + tq - 2) // tk, 0, nk - 1)

    def loop(count, fn):
        def body(i, carry):
            fn(i)
            return carry
        lax.fori_loop(0, count, body, 0)

    loop(last - n_full + 1, lambda i: one_block(last - i, True))
    def four_blocks(i):
        two_blocks(n_full - 1 - 4 * i, 0)
        two_blocks(n_full - 3 - 4 * i, 1)

    rest = n_full % 4
    prefetch_z(n_full - 1, 0)
    loop(n_full // 4, four_blocks)
    loop(rest // 2, lambda i: two_blocks(rest - 1, 0))
    loop(rest % 2, lambda i: one_block(0, False))
    o_ref[0] = acc_scr[...].astype(o_ref.dtype)


def sb_attn(q, k_all, v_all, past, tq, tk):
    B, T, _ = q.shape
    S = k_all.shape[1]
    assert S % tk == 0 and T % tq == 0 and tk % LANES == 0
    incl = np.arange(tk)[:, None] >= np.arange(tk)[None, :]
    sum_mat = jnp.asarray(np.concatenate([incl, incl], axis=0), BF16)
    return pl.pallas_call(
        functools.partial(_sb_kernel, tq, tk, past),
        grid=(B, PAIRS, T // tq),
        in_specs=[pl.BlockSpec((1, tq, LANES), lambda b, p, i: (b, i, p)),
                  pl.BlockSpec((1, S, LANES), lambda b, p, i: (b, 0, p)),
                  pl.BlockSpec((1, S, LANES), lambda b, p, i: (b, 0, p)),
                  pl.BlockSpec((2 * tk, tk), lambda b, p, i: (0, 0))],
        out_specs=pl.BlockSpec((1, tq, LANES), lambda b, p, i: (b, i, p)),
        out_shape=jax.ShapeDtypeStruct((B, T, WIDTH), BF16),
        scratch_shapes=[pltpu.VMEM((2, tq, LANES), F32), pltpu.VMEM((tq, LANES), F32),
                        pltpu.VMEM((2, tq, LANES), BF16), pltpu.VMEM((2, 2, 2, tq, tk), F32)],
        compiler_params=_params(("parallel", "parallel", "arbitrary")),
        name="sb_attn",
    )(q, k_all, v_all, sum_mat)


def _merge_kernel(x_ref, gm_ref, ya_ref, za_ref, yb_ref, zb_ref, yc_ref, zc_ref, g_ref,
                  wa_ref, wb_ref, wc_ref, wo_ref, o_ref):
    def branch(y_ref, z_ref, w_ref):
        return _bdot(y_ref[...].astype(F32) * _silu(z_ref[...].astype(F32)), w_ref[...])

    sg = _sigmoid(g_ref[...].astype(F32))
    merged = (sg[:, :D_MODEL] * branch(ya_ref, za_ref, wa_ref)
              + sg[:, D_MODEL:2 * D_MODEL] * branch(yb_ref, zb_ref, wb_ref)
              + sg[:, 2 * D_MODEL:] * branch(yc_ref, zc_ref, wc_ref))
    o_ref[...] = x_ref[...] + gm_ref[0] * _bdot(merged, wo_ref[...])


def merge_out(x2, gate_mod, ya, za, yb, zb, yc, zc, gates, wl, T, tm):
    M, D = x2.shape
    per = T // tm
    row = lambda wd: pl.BlockSpec((tm, wd), lambda i: (i, 0))
    const = lambda r, c: pl.BlockSpec((r, c), lambda i: (0, 0))
    return pl.pallas_call(
        _merge_kernel,
        grid=(M // tm,),
        in_specs=[row(D), pl.BlockSpec((1, 1, D), lambda i: (i // per, 0, 0))] + [row(WIDTH)] * 6 + [row(3 * D),
                  const(WIDTH, D), const(WIDTH, D), const(WIDTH, D), const(D, D)],
        out_specs=row(D),
        out_shape=jax.ShapeDtypeStruct((M, D), F32),
        compiler_params=_params(("parallel",)),
        name="merge_out",
    )(x2, gate_mod, ya, za, yb, zb, yc, zc, gates, wl["w_br_rwkv"], wl["w_br_mla"], wl["w_br_sb"], wl["w_out"])


def _rot_matrix(width, offset):
    half = ROPE // 2
    m = np.zeros((width, width), np.float32)
    for i in range(half):
        m[offset + half + i, offset + i] = -1.0
        m[offset + i, offset + half + i] = 1.0
    return jnp.asarray(m, BF16)


def _prep_layer(P, l):
    row = lambda a: a[l].reshape(1, -1)
    zeros_lora = jnp.zeros((LORA, WIDTH), F32)
    w_uq = P["mla_w_uq"][l].reshape(Q_LORA, HEADS, QK).transpose(1, 0, 2)
    w_ukv = P["mla_w_ukv"][l].reshape(KV_LORA, HEADS, 2 * HEAD_DIM)
    w_uk = jnp.pad(w_ukv[:, :, :NOPE].transpose(1, 0, 2), ((0, 0), (0, 0), (0, ROPE)))
    w_uv = w_ukv[:, :, NOPE:].reshape(KV_LORA, PAIRS, LANES).transpose(1, 0, 2)
    e96 = np.zeros((ROPE, QK), np.float32)
    e96[np.arange(ROPE), NOPE + np.arange(ROPE)] = 1.0
    return dict(
        norm_g=row(P["norm_g"]),
        mu=row(P["rwkv_mu"]), w0=row(P["rwkv_w0"]), a0=row(P["rwkv_a0"]), k_k=row(P["rwkv_k_k"]),
        k_a=row(P["rwkv_k_a"]), r_k=row(P["rwkv_r_k"]), gn_g=row(P["rwkv_gn_g"]), gn_b=row(P["rwkv_gn_b"]),
        w_up=jnp.concatenate([P["rwkv_w_up"][l], zeros_lora], axis=0).astype(BF16),
        a_up=jnp.concatenate([zeros_lora, P["rwkv_a_up"][l]], axis=0).astype(BF16),
        q_norm=row(P["mla_q_norm"]), w_uq=w_uq.astype(BF16),
        qn=jnp.concatenate([P["mla_qn_nope"][l], P["mla_qn_rope"][l]]).reshape(1, QK),
        kv_norm=row(P["mla_kv_norm"]), kn_rope=row(P["mla_kn_rope"]),
        kn_nope=jnp.pad(P["mla_kn_nope"][l], (0, ROPE)).reshape(1, QK),
        w_uk=w_uk.astype(BF16), w_uv=w_uv.astype(BF16),
        p96=_rot_matrix(QK, NOPE), p32=_rot_matrix(ROPE, 0), e96=jnp.asarray(e96, BF16),
        w_br_rwkv=P["w_br_rwkv"][l].astype(BF16), w_br_mla=P["w_br_mla"][l].astype(BF16),
        w_br_sb=P["w_br_sb"][l].astype(BF16), w_out=P["w_out"][l].astype(BF16),
    )


def _rope_tables(past, T):
    half = ROPE // 2
    inv = ROPE_THETA ** (-jnp.arange(half, dtype=F32) / half)
    ang = (past + jnp.arange(T, dtype=jnp.int32)).astype(F32)[:, None] * inv
    c32 = jnp.tile(jnp.cos(ang), (1, 2))
    s32 = jnp.tile(jnp.sin(ang), (1, 2))
    c96 = jnp.concatenate([jnp.ones((T, NOPE), F32), c32], axis=1)
    s96 = jnp.concatenate([jnp.zeros((T, NOPE), F32), s32], axis=1)
    return c96, s96, c32, s32


def _pad_rows(a, rows):
    return a if a.shape[1] == rows else jnp.pad(a, ((0, 0), (0, rows - a.shape[1])) + ((0, 0),) * (a.ndim - 2))


def _trunk(x, mods, pasts, layers, w_in_all, cfg):
    B, T, D = x.shape
    past = cfg["past"]
    S = past + T
    s_pad = -(-S // cfg["kv_mult"]) * cfg["kv_mult"]
    tabs = _rope_tables(past, T)
    x2 = x.reshape(B * T, D)
    new = []
    for l, wl in enumerate(layers):
        shift, scale, gate = (mods[l][:, None, i * D:(i + 1) * D] for i in range(3))
        outs = in_proj(x2, scale, shift, wl["norm_g"], w_in_all, l, T, cfg["tm"])
        pr, zA, cq, ckv, kr, zB, sq, sk, sv, zC, gates, sk_b, sv_b = outs
        seq = lambda a: a.reshape(B, T, a.shape[-1])
        if pasts is None:
            s0 = jnp.zeros((B, PAIRS, LANES, LANES), F32)
            shift0 = jnp.zeros((B, 1, SHIFT_W), F32)
        else:
            s0, shift0 = _state_to_bd(pasts[l][0]), pasts[l][1]
        pr3 = seq(pr)
        yA, s_fin = rwkv_branch(pr3, shift0, s0, wl, cfg["rwkv_n"])
        q, ckv_n, k_rope = mla_q(seq(cq), seq(ckv), seq(kr), tabs, wl, cfg["tm_q"])
        sk3, sv3 = seq(sk), seq(sv)
        if pasts is None:
            ckv_all, kr_all, k_all, v_all = ckv_n, k_rope, seq(sk_b), seq(sv_b)
        else:
            _, _, ckv0, kr0, k0, v0 = pasts[l]
            ckv_all = jnp.concatenate([ckv0, ckv_n], axis=1)
            kr_all = jnp.concatenate([kr0, k_rope], axis=1)
            k_all = jnp.concatenate([k0.reshape(B, past, WIDTH).astype(BF16), seq(sk_b)], axis=1)
            v_all = jnp.concatenate([v0.reshape(B, past, WIDTH).astype(BF16), seq(sv_b)], axis=1)
        ckv_all, kr_all, k_all, v_all = (_pad_rows(a, s_pad) for a in (ckv_all, kr_all, k_all, v_all))
        kf, vf = mla_kv(ckv_all, kr_all, wl, cfg["ts"])
        yB = mla_attn(q, kf, vf, past, S, cfg["tq"], cfg["tk"], cfg["tkd"])
        yC = sb_attn(seq(sq), k_all, v_all, past, cfg["sb_tq"], cfg["sb_tk"])
        x2 = merge_out(x2, gate, yA.reshape(B * T, WIDTH), zA, yB.reshape(B * T, WIDTH), zB,
                       yC.reshape(B * T, WIDTH), zC, gates, wl, T, cfg["tm"])
        new.append((_bd_to_state(s_fin), pr3[:, -1:], ckv_n, k_rope,
                    sk3.reshape(B, T, HEADS, HEAD_DIM), sv3.reshape(B, T, HEADS, HEAD_DIM)))
    stacked = [jnp.stack([st[i] for st in new], axis=0) for i in range(6)]
    return x2.reshape(B, T, D), stacked


PROMPT_CFG = dict(past=0, tm=512, rwkv_n=128, tm_q=256, ts=256, tq=512, tk=1024, tkd=512, sb_tq=512, sb_tk=256,
                  kv_mult=1024)
SAMPLE_CFG = dict(past=2048, tm=16, rwkv_n=16, tm_q=16, ts=2560, tq=16, tk=2560, tkd=2560, sb_tq=16, sb_tk=512,
                  kv_mult=512)


def kernel(x_prompt, x_sample, state_rwkv_wkv, state_rwkv_shift, cache_mla_ckv, cache_mla_krope, cache_sb_k, cache_sb_v, c_prompt, c_sample, w_ada, b_ada, norm_g, w_in, rwkv_mu, rwkv_w0, rwkv_w_up, rwkv_a0, rwkv_a_up, rwkv_k_k, rwkv_k_a, rwkv_r_k, rwkv_gn_g, rwkv_gn_b, w_br_rwkv, mla_q_norm, mla_w_uq, mla_kv_norm, mla_w_ukv, mla_qn_nope, mla_qn_rope, mla_kn_nope, mla_kn_rope, w_br_mla, w_br_sb, w_out):
    P = dict(w_in=w_in, norm_g=norm_g, rwkv_mu=rwkv_mu, rwkv_w0=rwkv_w0, rwkv_w_up=rwkv_w_up, rwkv_a0=rwkv_a0,
             rwkv_a_up=rwkv_a_up, rwkv_k_k=rwkv_k_k, rwkv_k_a=rwkv_k_a, rwkv_r_k=rwkv_r_k, rwkv_gn_g=rwkv_gn_g,
             rwkv_gn_b=rwkv_gn_b, w_br_rwkv=w_br_rwkv, mla_q_norm=mla_q_norm, mla_w_uq=mla_w_uq,
             mla_kv_norm=mla_kv_norm, mla_w_ukv=mla_w_ukv, mla_qn_nope=mla_qn_nope, mla_qn_rope=mla_qn_rope,
             mla_kn_nope=mla_kn_nope, mla_kn_rope=mla_kn_rope, w_br_mla=w_br_mla, w_br_sb=w_br_sb, w_out=w_out)
    depth = w_in.shape[0]
    bp = x_prompt.shape[0]
    layers = [_prep_layer(P, l) for l in range(depth)]
    offs = np.cumsum([0] + [wd for _, wd in IN_GROUPS])
    src = {name: (offs[i], offs[i + 1]) for i, (name, _) in enumerate(IN_GROUPS)}
    w_in_all = jnp.concatenate([w_in[:, :, src[name][0]:src[name][1]] for name, _ in _in_weight_layout()[0]],
                               axis=-1).astype(BF16)
    mods = ada_mod(jnp.concatenate([c_prompt, c_sample], axis=0), w_ada, b_ada)
    y_p, st_p = _trunk(x_prompt, mods[:, :bp], None, layers, w_in_all, PROMPT_CFG)
    pasts = [(state_rwkv_wkv[l], state_rwkv_shift[l], cache_mla_ckv[l], cache_mla_krope[l],
              cache_sb_k[l], cache_sb_v[l]) for l in range(depth)]
    assert cache_mla_ckv.shape[2] == SAMPLE_CFG["past"]
    y_s, st_s = _trunk(x_sample, mods[:, bp:], pasts, layers, w_in_all, SAMPLE_CFG)
    return (y_p, y_s, *st_p, *st_s)
```

```python
import functools

import numpy as np
import jax
import jax.numpy as jnp
from jax import lax
from jax.experimental import pallas as pl
from jax.experimental.pallas import tpu as pltpu

F32 = jnp.float32
BF16 = jnp.bfloat16

D_MODEL = 1024
CHUNK = 64
RMS_EPS = 1e-6
GN_EPS = 64e-5
NEG_BIG = -1e30
ROPE_THETA = 10000.0
LOG2E = 1.4426950408889634

HEADS = 8
HEAD_DIM = 64
WIDTH = HEADS * HEAD_DIM
PAIRS = HEADS // 2
LANES = 128
LORA = 64
SHIFT_W = 3 * WIDTH + 2 * LORA
Q_LORA = 384
KV_LORA = 256
ROPE = 32
NOPE = 64
QK = NOPE + ROPE
RWKV_CHUNK = 64

IN_GROUPS = (("pr", SHIFT_W), ("zA", WIDTH), ("cq", Q_LORA), ("ckv", KV_LORA), ("kr", ROPE),
             ("zB", WIDTH), ("sq", WIDTH), ("sk", WIDTH), ("sv", WIDTH), ("zC", WIDTH),
             ("gates", 3 * D_MODEL))
BF16_COPIES = ("sk", "sv")
BF16_GROUPS = ("zA", "zB", "zC", "gates")

VMEM_LIMIT = 56 * 1024 * 1024


def _params(sem):
    return pltpu.CompilerParams(dimension_semantics=sem, vmem_limit_bytes=VMEM_LIMIT)


def _bdot(a, b):
    return jnp.dot(a.astype(BF16), b.astype(BF16), preferred_element_type=F32)


def _bdot_nt(a, b):
    return lax.dot_general(a.astype(BF16), b.astype(BF16), (((1,), (1,)), ((), ())),
                           preferred_element_type=F32)


def _bdot_tn(a, b):
    return lax.dot_general(a.astype(BF16), b.astype(BF16), (((0,), (0,)), ((), ())),
                           preferred_element_type=F32)


def _split(x, n):
    parts, r = [], x
    for _ in range(n):
        p = r.astype(BF16)
        parts.append(p)
        r = r - p.astype(F32)
    return parts


def _xdot_r(x, m01, n):
    m = m01.astype(BF16)
    return sum(jnp.dot(p, m, preferred_element_type=F32) for p in _split(x, n))


def _xdot_l(m01, x, n):
    m = m01.astype(BF16)
    return sum(jnp.dot(m, p, preferred_element_type=F32) for p in _split(x, n))


def _sigmoid(x):
    return 1.0 / (1.0 + jnp.exp(-x))


def _silu(x):
    return x * _sigmoid(x)


def _softplus(x):
    return jnp.maximum(x, 0.0) + jnp.log(1.0 + jnp.exp(-jnp.abs(x)))


def _rms(x, g):
    return x * lax.rsqrt(jnp.mean(x * x, axis=-1, keepdims=True) + RMS_EPS) * g


def _iota2(shape, dim):
    return lax.broadcasted_iota(jnp.int32, shape, dim)


def _ada_kernel(c_ref, w_ref, b_ref, o_ref):
    o_ref[0] = _bdot(_silu(c_ref[...]), w_ref[0]) + b_ref[0]


def ada_mod(c, w_ada, b_ada):
    L, D, N = w_ada.shape
    rows = c.shape[0]
    tn = 1024
    return pl.pallas_call(
        _ada_kernel,
        grid=(L, N // tn),
        in_specs=[pl.BlockSpec((rows, D), lambda l, j: (0, 0)),
                  pl.BlockSpec((1, D, tn), lambda l, j: (l, 0, j)),
                  pl.BlockSpec((1, 1, tn), lambda l, j: (l, 0, j))],
        out_specs=pl.BlockSpec((1, rows, tn), lambda l, j: (l, 0, j)),
        out_shape=jax.ShapeDtypeStruct((L, rows, N), F32),
        compiler_params=_params(("parallel", "parallel")),
        name="ada_mod",
    )(c, w_ada, b_ada.reshape(L, 1, N))


def _in_weight_layout():
    order = [g for g in IN_GROUPS if g[0] != "kr"] + [g for g in IN_GROUPS if g[0] == "kr"]
    offs, pos = {}, 0
    for name, wd in order:
        offs[name] = pos
        pos += wd
    return order, offs


def _inproj_kernel(x_ref, sc_ref, sh_ref, g_ref, w_ref, *outs):
    n = len(IN_GROUPS)
    offs = _in_weight_layout()[1]
    h = _rms(x_ref[...], g_ref[...]) * (1.0 + sc_ref[0]) + sh_ref[0]
    hb = h.astype(BF16)
    extra = iter(outs[n:])
    for (name, wd), o in zip(IN_GROUPS, outs):
        res = jnp.dot(hb, w_ref[0, :, offs[name]:offs[name] + wd], preferred_element_type=F32)
        o[...] = res.astype(o.dtype)
        if name in BF16_COPIES:
            next(extra)[...] = res.astype(BF16)


def in_proj(x2, scale, shift, g, w_all, layer, T, tm):
    M, D = x2.shape
    assert T % tm == 0 and M % tm == 0
    per = T // tm
    row = lambda i: (i, 0)
    mod = lambda i: (i // per, 0, 0)
    const = lambda i: (0, 0)
    in_specs = [pl.BlockSpec((tm, D), row), pl.BlockSpec((1, 1, D), mod), pl.BlockSpec((1, 1, D), mod),
                pl.BlockSpec((1, D), const),
                pl.BlockSpec((1, D, w_all.shape[2]), lambda i: (layer, 0, 0), pipeline_mode=pl.Buffered(1))]
    out_specs = [pl.BlockSpec((tm, wd), row) for _, wd in IN_GROUPS]
    out_shape = [jax.ShapeDtypeStruct((M, wd), BF16 if name in BF16_GROUPS else F32) for name, wd in IN_GROUPS]
    out_specs += [pl.BlockSpec((tm, WIDTH), row) for _ in BF16_COPIES]
    out_shape += [jax.ShapeDtypeStruct((M, WIDTH), BF16) for _ in BF16_COPIES]
    return pl.pallas_call(
        _inproj_kernel, grid=(M // tm,), in_specs=in_specs, out_specs=out_specs, out_shape=out_shape,
        compiler_params=_params(("parallel",)), name="in_proj",
    )(x2, scale, shift, g, w_all)


def _rwkv_kernel(n, pr_ref, past_ref, s0_ref, mu_ref, w0_ref, a0_ref, kk_ref, ka_ref, rk_ref,
                 gng_ref, gnb_ref, wup_ref, aup_ref, y_ref, sfin_ref, st_scr, last_scr):
    tb = pl.program_id(1)

    @pl.when(tb == 0)
    def _():
        st_scr[...] = s0_ref[0]
        last_scr[...] = past_ref[0]

    p = pr_ref[0]
    row = _iota2((n, 1), 0)
    prev = jnp.where(row == 0, last_scr[...], pltpu.roll(p, 1, axis=0))
    last_scr[...] = p[n - 1:n, :]
    xs = p + (prev - p) * mu_ref[...]
    r, k, v = xs[:, :WIDTH], xs[:, WIDTH:2 * WIDTH], xs[:, 2 * WIDTH:3 * WIDTH]
    wa = xs[:, 3 * WIDTH:]
    w_log = -_softplus(-(w0_ref[...] + _bdot(jnp.tanh(wa), wup_ref[...]))) - 0.5
    lw = -jnp.exp(w_log)
    a = _sigmoid(a0_ref[...] + _bdot(wa, aup_ref[...]))

    ri, ci = _iota2((LANES, LANES), 0), _iota2((LANES, LANES), 1)
    seg64 = ((ri >> 6) == (ci >> 6)).astype(BF16)
    bd2 = ((ri >> 6) == (ci >> 6)).astype(F32)

    def segsum(t):
        return jnp.concatenate(
            [_xdot_r(t[:, LANES * q:LANES * (q + 1)], seg64, 2) for q in range(PAIRS)], axis=-1)

    kkr = k * kk_ref[...]
    kk = kkr * jnp.minimum(lax.rsqrt(segsum(kkr * kkr)), 1e12)
    k2 = k * (1.0 + (a - 1.0) * ka_ref[...])
    bonus = segsum(r * k2 * rk_ref[...]) * v

    chunk = min(RWKV_CHUNK, n)
    shift = chunk.bit_length() - 1
    tr, tc = _iota2((n, n), 0), _iota2((n, n), 1)
    same = (tr >> shift) == (tc >> shift)
    strict = (same & (tr > tc)).astype(F32)
    incl = (same & (tr >= tc)).astype(F32)
    eye = (tr == tc).astype(F32)
    lw_parts = _split(lw, 2)
    cum = lambda m01: sum(jnp.dot(m01.astype(BF16), part, preferred_element_type=F32) for part in lw_parts)
    g_log = cum(incl)
    g_end = cum(same)
    e_in, e_inv, e_rest = jnp.exp(g_log), jnp.exp(-g_log), jnp.exp(g_end - g_log)
    r_t = r * e_in
    kk_t = kk * jnp.exp(g_log - lw)
    kka = kk * a
    k_t, b_t = k2 * e_inv, kka * e_inv
    k_e, b_e = k2 * e_rest, kka * e_rest
    dec = jnp.exp(g_end)

    lane = _iota2((1, LANES), 1)
    head_mask = [(lane < HEAD_DIM).astype(F32), (lane >= HEAD_DIM).astype(F32)]

    sls = [slice(LANES * q, LANES * (q + 1)) for q in range(PAIRS)]
    units = [(q, hm) for q in range(PAIRS) for hm in head_mask]
    rp, kkp, vp = [r_t[:, s] for s in sls], [kk_t[:, s] for s in sls], [v[:, s] for s in sls]
    kb, bb = [k_t[:, s].astype(BF16) for s in sls], [b_t[:, s].astype(BF16) for s in sls]
    vb, kkb = [x.astype(BF16) for x in vp], [x.astype(BF16) for x in kkp]
    kkm = [(kkp[q] * hm).astype(BF16) for q, hm in units]
    rm = [(rp[q] * hm).astype(BF16) for q, hm in units]
    l_b = [_bdot_nt(kkm[u], bb[q]) * strict for u, (q, _) in enumerate(units)]
    l_k = [_bdot_nt(kkm[u], kb[q]) * strict for u, (q, _) in enumerate(units)]
    r_k = [_bdot_nt(rm[u], kb[q]) * incl for u, (q, _) in enumerate(units)]
    r_b = [_bdot_nt(rm[u], bb[q]) * incl for u, (q, _) in enumerate(units)]
    sub = min(16, chunk)
    sub_shift = sub.bit_length() - 1
    inner = ((tr >> sub_shift) == (tc >> sub_shift)).astype(F32)
    pw = [x * inner for x in l_b]
    t_inv = [eye - x for x in pw]
    for _ in range(sub_shift - 1):
        pw = [_bdot(x, x) for x in pw]
        t_inv = [t + _bdot(t, x) for t, x in zip(t_inv, pw)]
    for level in range(sub_shift, shift):
        joined = (((tr >> (level + 1)) == (tc >> (level + 1))) & ((tr >> level) != (tc >> level))).astype(F32)
        t_inv = [t - _bdot(_bdot(t, x * joined), t) for t, x in zip(t_inv, l_b)]
    lkv = [_bdot(l_k[u], vb[q]) for u, (q, _) in enumerate(units)]
    w_h = [_bdot(t_inv[u], kkb[q]) for u, (q, _) in enumerate(units)]
    u0_h = [_bdot(t, x) for t, x in zip(t_inv, lkv)]
    q_h = [rp[q] - _bdot(r_b[u], w_h[u]) for u, (q, _) in enumerate(units)]
    y0_h = [_bdot(r_k[u], vb[q]) - _bdot(r_b[u], u0_h[u]) for u, (q, _) in enumerate(units)]
    first = head_mask[0] > 0.5
    pair = lambda xs: [jnp.where(first, xs[2 * q], xs[2 * q + 1]) for q in range(PAIRS)]
    w_c, u0_c, q_c, y0_c = pair(w_h), pair(u0_h), pair(q_h), pair(y0_h)

    st = [st_scr[q] for q in range(PAIRS)]
    ys = [[] for _ in range(PAIRS)]
    rows = [slice(chunk * c, chunk * (c + 1)) for c in range(n // chunk)]
    n_c = [[(_bdot_tn(vp[q][rs], k_e[rs, sls[q]]) - _bdot_tn(u0_c[q][rs], b_e[rs, sls[q]])) * bd2
            for q in range(PAIRS)] for rs in rows]
    a_c = [[_bdot_tn(w_c[q][rs], b_e[rs, sls[q]]) * bd2 for q in range(PAIRS)] for rs in rows]
    for c, rs in enumerate(rows):
        for q in range(PAIRS):
            ys[q].append(_bdot_nt(q_c[q][rs], st[q]) + y0_c[q][rs])
        sa = [_bdot(st[q], a_c[c][q]) for q in range(PAIRS)]
        st = [st[q] * dec[chunk * c:chunk * c + 1, sls[q]] + (n_c[c][q] - sa[q]) for q in range(PAIRS)]
    for q in range(PAIRS):
        st_scr[q] = st[q]
    y = [x[0] if len(x) == 1 else jnp.concatenate(x, axis=0) for x in ys]
    mean = [_xdot_r(x, seg64, 2) * (1.0 / HEAD_DIM) for x in y]
    d = [x - m for x, m in zip(y, mean)]
    var = [_xdot_r(x * x, seg64, 2) * (1.0 / HEAD_DIM) for x in d]
    for q, s in enumerate(sls):
        yn = d[q] * lax.rsqrt(var[q] + GN_EPS) * gng_ref[:, s] + gnb_ref[:, s] + bonus[:, s]
        y_ref[0, :, s] = yn.astype(y_ref.dtype)

    @pl.when(tb == pl.num_programs(1) - 1)
    def _():
        sfin_ref[0] = st_scr[...]


def rwkv_branch(pr, past_shift, s0_bd, wl, n):
    B, T, _ = pr.shape
    assert T % n == 0 and n % min(RWKV_CHUNK, n) == 0 and n & (n - 1) == 0
    vec = lambda wd: pl.BlockSpec((1, wd), lambda b, t: (0, 0))
    in_specs = [pl.BlockSpec((1, n, SHIFT_W), lambda b, t: (b, t, 0)),
                pl.BlockSpec((1, 1, SHIFT_W), lambda b, t: (b, 0, 0)),
                pl.BlockSpec((1, PAIRS, LANES, LANES), lambda b, t: (b, 0, 0, 0)),
                vec(SHIFT_W)] + [vec(WIDTH)] * 7 + [
                pl.BlockSpec((LANES, WIDTH), lambda b, t: (0, 0)),
                pl.BlockSpec((LANES, WIDTH), lambda b, t: (0, 0))]
    return pl.pallas_call(
        functools.partial(_rwkv_kernel, n),
        grid=(B, T // n),
        in_specs=in_specs,
        out_specs=[pl.BlockSpec((1, n, WIDTH), lambda b, t: (b, t, 0)),
                   pl.BlockSpec((1, PAIRS, LANES, LANES), lambda b, t: (b, 0, 0, 0))],
        out_shape=[jax.ShapeDtypeStruct((B, T, WIDTH), BF16),
                   jax.ShapeDtypeStruct((B, PAIRS, LANES, LANES), F32)],
        scratch_shapes=[pltpu.VMEM((PAIRS, LANES, LANES), F32), pltpu.VMEM((1, SHIFT_W), F32)],
        compiler_params=_params(("parallel", "arbitrary")),
        name="rwkv7",
    )(pr, past_shift, s0_bd, wl["mu"], wl["w0"], wl["a0"], wl["k_k"], wl["k_a"], wl["r_k"],
      wl["gn_g"], wl["gn_b"], wl["w_up"], wl["a_up"])


def _state_to_bd(s):
    B = s.shape[0]
    s = s.reshape(B, PAIRS, 2, HEAD_DIM, HEAD_DIM)
    z = jnp.zeros_like(s[:, :, 0])
    top = jnp.concatenate([s[:, :, 0], z], axis=-1)
    bot = jnp.concatenate([z, s[:, :, 1]], axis=-1)
    return jnp.concatenate([top, bot], axis=-2)


def _bd_to_state(s):
    B = s.shape[0]
    h0 = s[:, :, :HEAD_DIM, :HEAD_DIM]
    h1 = s[:, :, HEAD_DIM:, HEAD_DIM:]
    return jnp.stack([h0, h1], axis=2).reshape(B, HEADS, HEAD_DIM, HEAD_DIM)


def _mla_q_kernel(cq_ref, ckv_ref, kr_ref, c96_ref, s96_ref, c32_ref, s32_ref, gq_ref, wuq_ref, wrot_ref, gqn_ref,
                  grot_ref, gkv_ref, gkr_ref, p32_ref, q_ref, ckvn_ref, krope_ref):
    cqb = _rms(cq_ref[0], gq_ref[...]).astype(BF16)
    nope = _iota2((1, QK), 1) < NOPE
    gc = gqn_ref[...] * c96_ref[...]
    gs = grot_ref[...] * s96_ref[...]
    for h in range(HEADS):
        qh = jnp.dot(cqb, wuq_ref[h], preferred_element_type=F32)
        qr = jnp.dot(cqb, wrot_ref[h], preferred_element_type=F32)
        sq = qh * qh
        ss_n = jnp.sum(jnp.where(nope, sq, 0.0), axis=-1, keepdims=True)
        ss_r = jnp.sum(jnp.where(nope, 0.0, sq), axis=-1, keepdims=True)
        inv = jnp.where(nope, lax.rsqrt(ss_n * (1.0 / NOPE) + RMS_EPS), lax.rsqrt(ss_r * (1.0 / ROPE) + RMS_EPS))
        qo = inv * (qh * gc + qr * gs)
        q_ref[0, h] = (qo * (QK ** -0.5 * LOG2E)).astype(BF16)
    ckvn_ref[0] = _rms(ckv_ref[0], gkv_ref[...])
    krn = _rms(kr_ref[0], gkr_ref[...])
    krope_ref[0] = krn * c32_ref[...] + _xdot_r(krn, p32_ref[...], 3) * s32_ref[...]


def mla_q(cq, ckv, kr, tabs, wl, tm):
    B, T, _ = cq.shape
    c96, s96, c32, s32 = tabs
    tok = lambda wd: pl.BlockSpec((1, tm, wd), lambda b, t: (b, t, 0))
    tab = lambda wd: pl.BlockSpec((tm, wd), lambda b, t: (t, 0))
    vec = lambda wd: pl.BlockSpec((1, wd), lambda b, t: (0, 0))
    return pl.pallas_call(
        _mla_q_kernel,
        grid=(B, T // tm),
        in_specs=[tok(Q_LORA), tok(KV_LORA), tok(ROPE), tab(QK), tab(QK), tab(ROPE), tab(ROPE),
                  vec(Q_LORA), pl.BlockSpec((HEADS, Q_LORA, QK), lambda b, t: (0, 0, 0)),
                  pl.BlockSpec((HEADS, Q_LORA, QK), lambda b, t: (0, 0, 0)), vec(QK), vec(QK),
                  vec(KV_LORA), vec(ROPE), pl.BlockSpec((ROPE, ROPE), lambda b, t: (0, 0))],
        out_specs=[pl.BlockSpec((1, HEADS, tm, QK), lambda b, t: (b, 0, t, 0)), tok(KV_LORA), tok(ROPE)],
        out_shape=[jax.ShapeDtypeStruct((B, HEADS, T, QK), BF16),
                   jax.ShapeDtypeStruct((B, T, KV_LORA), F32),
                   jax.ShapeDtypeStruct((B, T, ROPE), F32)],
        compiler_params=_params(("parallel", "parallel")),
        name="mla_q",
    )(cq, ckv, kr, c96, s96, c32, s32, wl["q_norm"], wl["w_uq"], wl["w_uq_rot"], wl["qn"], wl["qn_rot"],
      wl["kv_norm"], wl["kn_rope"], wl["p32"])


def _mla_kv_kernel(ckv_ref, kr_ref, wuk_ref, wuv_ref, gkn_ref, e_ref, k_ref, v_ref):
    cb = ckv_ref[0].astype(BF16)
    k_rope = _xdot_r(kr_ref[0], e_ref[...], 3)
    for h in range(HEADS):
        kh = jnp.dot(cb, wuk_ref[h], preferred_element_type=F32)
        ms = jnp.sum(kh * kh, axis=-1, keepdims=True) * (1.0 / NOPE)
        k_ref[0, h] = (kh * lax.rsqrt(ms + RMS_EPS) * gkn_ref[...] + k_rope).astype(BF16)
    for q in range(PAIRS):
        v_ref[0, q] = jnp.dot(cb, wuv_ref[q], preferred_element_type=F32).astype(BF16)


def mla_kv(ckv_all, kr_all, wl, ts):
    B, S, _ = ckv_all.shape
    return pl.pallas_call(
        _mla_kv_kernel,
        grid=(B, S // ts),
        in_specs=[pl.BlockSpec((1, ts, KV_LORA), lambda b, t: (b, t, 0)),
                  pl.BlockSpec((1, ts, ROPE), lambda b, t: (b, t, 0)),
                  pl.BlockSpec((HEADS, KV_LORA, QK), lambda b, t: (0, 0, 0)),
                  pl.BlockSpec((PAIRS, KV_LORA, LANES), lambda b, t: (0, 0, 0)),
                  pl.BlockSpec((1, QK), lambda b, t: (0, 0)),
                  pl.BlockSpec((ROPE, QK), lambda b, t: (0, 0))],
        out_specs=[pl.BlockSpec((1, HEADS, ts, QK), lambda b, t: (b, 0, t, 0)),
                   pl.BlockSpec((1, PAIRS, ts, LANES), lambda b, t: (b, 0, t, 0))],
        out_shape=[jax.ShapeDtypeStruct((B, HEADS, S, QK), BF16),
                   jax.ShapeDtypeStruct((B, PAIRS, S, LANES), BF16)],
        compiler_params=_params(("parallel", "parallel")),
        name="mla_kv",
    )(ckv_all, kr_all, wl["w_uk"], wl["w_uv"], wl["kn_nope"], wl["e96"])


def _mla_attn_kernel(tq, tk, tkd, past, s_len, q_ref, k_ref, v_ref, o_ref, m_scr, l_scr, acc_scr):
    q0 = past + pl.program_id(2) * tq
    first = _iota2((1, LANES), 1) < HEAD_DIM
    m_scr[...] = jnp.full(m_scr.shape, NEG_BIG, F32)
    l_scr[...] = jnp.zeros(l_scr.shape, F32)
    acc_scr[...] = jnp.zeros(acc_scr.shape, F32)
    qs = [q_ref[0, h] for h in range(2)]

    def block(start, width, masked):
        reps = width // LANES
        lane_chunks = lambda x: [x[:, LANES * c:LANES * (c + 1)] for c in range(reps)]
        ks = pl.multiple_of(start, width)
        s = [_bdot_nt(qs[h], k_ref[0, h, pl.ds(ks, width), :]) for h in range(2)]
        if masked:
            limit = jnp.minimum((((q0 + _iota2((tq, 1), 0)) >> 6) + 1) << 6, s_len)
            visible = (ks + _iota2((1, width), 1)) < limit
            s = [jnp.where(visible, x, NEG_BIG) for x in s]
        vb = v_ref[0, 0, pl.ds(ks, width), :]
        m_old = [m_scr[h] for h in range(2)]
        m_blk = [functools.reduce(jnp.maximum, lane_chunks(x)) for x in s]
        m_new = [jnp.maximum(m_old[h], jnp.max(m_blk[h], axis=-1, keepdims=True)) for h in range(2)]
        alpha = [jnp.exp2(m_old[h] - m_new[h]) for h in range(2)]
        p = [jnp.exp2(s[h] - jnp.concatenate([m_new[h]] * reps, axis=1)) for h in range(2)]
        for h in range(2):
            l_scr[h] = functools.reduce(jnp.add, lane_chunks(p[h]), alpha[h] * l_scr[h])
            m_scr[h] = m_new[h]
        pv = _bdot(jnp.concatenate(p, axis=0), vb)
        acc_scr[...] = acc_scr[...] * jnp.where(first, alpha[0], alpha[1]) + jnp.where(first, pv[:tq], pv[tq:])

    n_full = jnp.minimum((q0 // CHUNK + 1) * CHUNK, s_len) // tk
    seen = jnp.minimum(((q0 + tq - 1) // CHUNK + 1) * CHUNK, s_len)
    n_diag = (seen - n_full * tk + tkd - 1) // tkd

    def full_body(kb, carry):
        block(kb * tk, tk, False)
        return carry

    def diag_body(i, carry):
        block(n_full * tk + i * tkd, tkd, True)
        return carry

    lax.fori_loop(0, n_full, full_body, 0)
    lax.fori_loop(0, n_diag, diag_body, 0)
    l = [jnp.sum(l_scr[h], axis=-1, keepdims=True) for h in range(2)]
    o_ref[0] = (acc_scr[...] / jnp.where(first, l[0], l[1])).astype(o_ref.dtype)


def mla_attn(q, k, v, past, s_len, tq, tk, tkd):
    B, _, T, _ = q.shape
    S = k.shape[2]
    assert S % tk == 0 and tk % tkd == 0 and tkd % LANES == 0 and T % tq == 0
    return pl.pallas_call(
        functools.partial(_mla_attn_kernel, tq, tk, tkd, past, s_len),
        grid=(B, PAIRS, T // tq),
        in_specs=[pl.BlockSpec((1, 2, tq, QK), lambda b, p, i: (b, p, i, 0)),
                  pl.BlockSpec((1, 2, S, QK), lambda b, p, i: (b, p, 0, 0)),
                  pl.BlockSpec((1, 1, S, LANES), lambda b, p, i: (b, p, 0, 0))],
        out_specs=pl.BlockSpec((1, tq, LANES), lambda b, p, i: (b, i, p)),
        out_shape=jax.ShapeDtypeStruct((B, T, WIDTH), BF16),
        scratch_shapes=[pltpu.VMEM((2, tq, LANES), F32), pltpu.VMEM((2, tq, LANES), F32),
                        pltpu.VMEM((tq, LANES), F32)],
        compiler_params=_params(("parallel", "parallel", "arbitrary")),
        name="mla_attn",
    )(q, k, v)


def _sb_kernel(tq, tk, past, q_ref, k_ref, v_ref, sum_ref, o_ref, carry_scr, acc_scr, qm_scr, z_scr):
    nk = k_ref.shape[1] // tk
    q0 = past + pl.program_id(2) * tq
    first = _iota2((1, LANES), 1) < HEAD_DIM
    carry_scr[...] = jnp.zeros(carry_scr.shape, F32)
    acc_scr[...] = jnp.zeros(acc_scr.shape, F32)
    qv = q_ref[0] * (HEAD_DIM ** -0.5 * LOG2E)
    qm_scr[0] = jnp.where(first, qv, 0.0).astype(BF16)
    qm_scr[1] = jnp.where(first, 0.0, qv).astype(BF16)

    def key_mask(kb):
        return (kb * tk + _iota2((1, tk), 1)) < (q0 + _iota2((tq, 1), 0))

    def stage_z(kb):
        kblk = k_ref[0, pl.ds(pl.multiple_of(kb * tk, tk), tk), :]
        z = _bdot_nt(qm_scr[...].reshape(2 * tq, LANES), kblk)
        return [z[:tq], z[tq:]]

    def stage_w(z, mask):
        split = []
        for read in z:
            x = read()
            neg_abs = lax.bitcast_convert_type(lax.bitcast_convert_type(x, jnp.uint32) | jnp.uint32(0x80000000), F32)
            w = jnp.maximum(x, 0.0) + jnp.log2(1.0 + jnp.exp2(neg_abs))
            if mask is not None:
                w = jnp.where(mask, w, 0.0)
            hi = w.astype(BF16)
            split.append(jnp.concatenate([hi, (w - hi.astype(F32)).astype(BF16)], axis=1))
        return split

    def stage_incl(split):
        both = jnp.dot(jnp.concatenate(split, axis=0), sum_ref[...], preferred_element_type=F32)
        incl = [both[:tq], both[tq:]]
        return incl, [x[:, 0:1] for x in incl]

    def stage_a(z, incl, c, mask):
        cb = [jnp.concatenate([x] * (tk // LANES), axis=1) for x in c]
        a = [jnp.exp2(z[h]() - incl[h] - cb[h]) for h in range(2)]
        if mask is not None:
            a = [jnp.where(mask, x, 0.0) for x in a]
        return a

    def stage_pv(a, kb):
        vblk = v_ref[0, pl.ds(pl.multiple_of(kb * tk, tk), tk), :]
        pv = _bdot(jnp.concatenate(a, axis=0), vblk)
        return jnp.where(first, pv[:tq], pv[tq:])

    def one_block(kb, masked):
        mask = key_mask(kb) if masked else None
        z = [lambda v=v: v for v in stage_z(kb)]
        incl, tot = stage_incl(stage_w(z, mask))
        c = [carry_scr[h] for h in range(2)]
        a = stage_a(z, incl, c, mask)
        acc_scr[...] += stage_pv(a, kb)
        for h in range(2):
            carry_scr[h] = c[h] + tot[h]

    def prefetch_z(kb, slot):
        for j in range(2):
            z = stage_z(jnp.maximum(kb - j, 0))
            for h in range(2):
                z_scr[slot, j, h] = z[h]

    def two_blocks(kb, slot):
        z1, z2 = ([lambda j=j, h=h: z_scr[slot, j, h] for h in range(2)] for j in range(2))
        incl1, tot1 = stage_incl(stage_w(z1, None))
        prefetch_z(kb - 2, 1 - slot)
        incl2, tot2 = stage_incl(stage_w(z2, None))
        c1 = [carry_scr[h] for h in range(2)]
        pv1 = stage_pv(stage_a(z1, incl1, c1, None), kb)
        c2 = [c1[h] + tot1[h] for h in range(2)]
        pv2 = stage_pv(stage_a(z2, incl2, c2, None), kb - 1)
        acc_scr[...] += pv1 + pv2
        for h in range(2):
            carry_scr[h] = c2[h] + tot2[h]

    n_full = q0 // tk
    last = jnp.clip((q0 + tq - 2) // tk, 0, nk - 1)

    def loop(count, fn):
        def body(i, carry):
            fn(i)
            return carry
        lax.fori_loop(0, count, body, 0)

    loop(last - n_full + 1, lambda i: one_block(last - i, True))
    def four_blocks(i):
        two_blocks(n_full - 1 - 4 * i, 0)
        two_blocks(n_full - 3 - 4 * i, 1)

    rest = n_full % 4
    prefetch_z(n_full - 1, 0)
    loop(n_full // 4, four_blocks)
    loop(rest // 2, lambda i: two_blocks(rest - 1, 0))
    loop(rest % 2, lambda i: one_block(0, False))
    o_ref[0] = acc_scr[...].astype(o_ref.dtype)


def sb_attn(q, k_all, v_all, past, tq, tk):
    B, T, _ = q.shape
    S = k_all.shape[1]
    assert S % tk == 0 and T % tq == 0 and tk % LANES == 0
    incl = np.arange(tk)[:, None] >= np.arange(tk)[None, :]
    sum_mat = jnp.asarray(np.concatenate([incl, incl], axis=0), BF16)
    return pl.pallas_call(
        functools.partial(_sb_kernel, tq, tk, past),
        grid=(B, PAIRS, T // tq),
        in_specs=[pl.BlockSpec((1, tq, LANES), lambda b, p, i: (b, i, p)),
                  pl.BlockSpec((1, S, LANES), lambda b, p, i: (b, 0, p)),
                  pl.BlockSpec((1, S, LANES), lambda b, p, i: (b, 0, p)),
                  pl.BlockSpec((2 * tk, tk), lambda b, p, i: (0, 0))],
        out_specs=pl.BlockSpec((1, tq, LANES), lambda b, p, i: (b, i, p)),
        out_shape=jax.ShapeDtypeStruct((B, T, WIDTH), BF16),
        scratch_shapes=[pltpu.VMEM((2, tq, LANES), F32), pltpu.VMEM((tq, LANES), F32),
                        pltpu.VMEM((2, tq, LANES), BF16), pltpu.VMEM((2, 2, 2, tq, tk), F32)],
        compiler_params=_params(("parallel", "parallel", "arbitrary")),
        name="sb_attn",
    )(q, k_all, v_all, sum_mat)


def _merge_kernel(x_ref, gm_ref, ya_ref, za_ref, yb_ref, zb_ref, yc_ref, zc_ref, g_ref,
                  wa_ref, wb_ref, wc_ref, wo_ref, o_ref):
    def branch(y_ref, z_ref, w_ref):
        return _bdot(y_ref[...].astype(F32) * _silu(z_ref[...].astype(F32)), w_ref[...])

    sg = _sigmoid(g_ref[...].astype(F32))
    merged = (sg[:, :D_MODEL] * branch(ya_ref, za_ref, wa_ref)
              + sg[:, D_MODEL:2 * D_MODEL] * branch(yb_ref, zb_ref, wb_ref)
              + sg[:, 2 * D_MODEL:] * branch(yc_ref, zc_ref, wc_ref))
    o_ref[...] = x_ref[...] + gm_ref[0] * _bdot(merged, wo_ref[...])


def merge_out(x2, gate_mod, ya, za, yb, zb, yc, zc, gates, wl, T, tm):
    M, D = x2.shape
    per = T // tm
    row = lambda wd: pl.BlockSpec((tm, wd), lambda i: (i, 0))
    const = lambda r, c: pl.BlockSpec((r, c), lambda i: (0, 0))
    return pl.pallas_call(
        _merge_kernel,
        grid=(M // tm,),
        in_specs=[row(D), pl.BlockSpec((1, 1, D), lambda i: (i // per, 0, 0))] + [row(WIDTH)] * 6 + [row(3 * D),
                  const(WIDTH, D), const(WIDTH, D), const(WIDTH, D), const(D, D)],
        out_specs=row(D),
        out_shape=jax.ShapeDtypeStruct((M, D), F32),
        compiler_params=_params(("parallel",)),
        name="merge_out",
    )(x2, gate_mod, ya, za, yb, zb, yc, zc, gates, wl["w_br_rwkv"], wl["w_br_mla"], wl["w_br_sb"], wl["w_out"])


def _rot_matrix(width, offset):
    half = ROPE // 2
    m = np.zeros((width, width), np.float32)
    for i in range(half):
        m[offset + half + i, offset + i] = -1.0
        m[offset + i, offset + half + i] = 1.0
    return jnp.asarray(m, BF16)


def _prep_layer(P, l):
    row = lambda a: a[l].reshape(1, -1)
    zeros_lora = jnp.zeros((LORA, WIDTH), F32)
    w_uq = P["mla_w_uq"][l].reshape(Q_LORA, HEADS, QK).transpose(1, 0, 2)
    w_ukv = P["mla_w_ukv"][l].reshape(KV_LORA, HEADS, 2 * HEAD_DIM)
    w_uk = jnp.pad(w_ukv[:, :, :NOPE].transpose(1, 0, 2), ((0, 0), (0, 0), (0, ROPE)))
    w_uv = w_ukv[:, :, NOPE:].reshape(KV_LORA, PAIRS, LANES).transpose(1, 0, 2)
    e96 = np.zeros((ROPE, QK), np.float32)
    e96[np.arange(ROPE), NOPE + np.arange(ROPE)] = 1.0
    half = ROPE // 2
    w_uq_rot = jnp.concatenate([jnp.zeros_like(w_uq[:, :, :NOPE]), -w_uq[:, :, NOPE + half:], w_uq[:, :, NOPE:NOPE + half]],
                               axis=-1)
    g_rope = P["mla_qn_rope"][l]
    qn_rot = jnp.concatenate([jnp.zeros((NOPE,), F32), g_rope[half:], g_rope[:half]]).reshape(1, QK)
    return dict(
        norm_g=row(P["norm_g"]),
        mu=row(P["rwkv_mu"]), w0=row(P["rwkv_w0"]), a0=row(P["rwkv_a0"]), k_k=row(P["rwkv_k_k"]),
        k_a=row(P["rwkv_k_a"]), r_k=row(P["rwkv_r_k"]), gn_g=row(P["rwkv_gn_g"]), gn_b=row(P["rwkv_gn_b"]),
        w_up=jnp.concatenate([P["rwkv_w_up"][l], zeros_lora], axis=0).astype(BF16),
        a_up=jnp.concatenate([zeros_lora, P["rwkv_a_up"][l]], axis=0).astype(BF16),
        q_norm=row(P["mla_q_norm"]), w_uq=w_uq.astype(BF16), w_uq_rot=w_uq_rot.astype(BF16), qn_rot=qn_rot,
        qn=jnp.concatenate([P["mla_qn_nope"][l], P["mla_qn_rope"][l]]).reshape(1, QK),
        kv_norm=row(P["mla_kv_norm"]), kn_rope=row(P["mla_kn_rope"]),
        kn_nope=jnp.pad(P["mla_kn_nope"][l], (0, ROPE)).reshape(1, QK),
        w_uk=w_uk.astype(BF16), w_uv=w_uv.astype(BF16),
        p32=_rot_matrix(ROPE, 0), e96=jnp.asarray(e96, BF16),
        w_br_rwkv=P["w_br_rwkv"][l].astype(BF16), w_br_mla=P["w_br_mla"][l].astype(BF16),
        w_br_sb=P["w_br_sb"][l].astype(BF16), w_out=P["w_out"][l].astype(BF16),
    )


def _rope_tables(past, T):
    half = ROPE // 2
    inv = ROPE_THETA ** (-jnp.arange(half, dtype=F32) / half)
    ang = (past + jnp.arange(T, dtype=jnp.int32)).astype(F32)[:, None] * inv
    c32 = jnp.tile(jnp.cos(ang), (1, 2))
    s32 = jnp.tile(jnp.sin(ang), (1, 2))
    c96 = jnp.concatenate([jnp.ones((T, NOPE), F32), c32], axis=1)
    s96 = jnp.concatenate([jnp.zeros((T, NOPE), F32), s32], axis=1)
    return c96, s96, c32, s32


def _pad_rows(a, rows):
    return a if a.shape[1] == rows else jnp.pad(a, ((0, 0), (0, rows - a.shape[1])) + ((0, 0),) * (a.ndim - 2))


def _trunk(x, mods, pasts, layers, w_in_all, cfg):
    B, T, D = x.shape
    past = cfg["past"]
    S = past + T
    s_pad = -(-S // cfg["kv_mult"]) * cfg["kv_mult"]
    tabs = _rope_tables(past, T)
    x2 = x.reshape(B * T, D)
    new = []
    for l, wl in enumerate(layers):
        shift, scale, gate = (mods[l][:, None, i * D:(i + 1) * D] for i in range(3))
        outs = in_proj(x2, scale, shift, wl["norm_g"], w_in_all, l, T, cfg["tm"])
        pr, zA, cq, ckv, kr, zB, sq, sk, sv, zC, gates, sk_b, sv_b = outs
        seq = lambda a: a.reshape(B, T, a.shape[-1])
        if pasts is None:
            s0 = jnp.zeros((B, PAIRS, LANES, LANES), F32)
            shift0 = jnp.zeros((B, 1, SHIFT_W), F32)
        else:
            s0, shift0 = _state_to_bd(pasts[l][0]), pasts[l][1]
        pr3 = seq(pr)
        yA, s_fin = rwkv_branch(pr3, shift0, s0, wl, cfg["rwkv_n"])
        q, ckv_n, k_rope = mla_q(seq(cq), seq(ckv), seq(kr), tabs, wl, cfg["tm_q"])
        sk3, sv3 = seq(sk), seq(sv)
        if pasts is None:
            ckv_all, kr_all, k_all, v_all = ckv_n, k_rope, seq(sk_b), seq(sv_b)
        else:
            _, _, ckv0, kr0, k0, v0 = pasts[l]
            ckv_all = jnp.concatenate([ckv0, ckv_n], axis=1)
            kr_all = jnp.concatenate([kr0, k_rope], axis=1)
            k_all = jnp.concatenate([k0.reshape(B, past, WIDTH).astype(BF16), seq(sk_b)], axis=1)
            v_all = jnp.concatenate([v0.reshape(B, past, WIDTH).astype(BF16), seq(sv_b)], axis=1)
        ckv_all, kr_all, k_all, v_all = (_pad_rows(a, s_pad) for a in (ckv_all, kr_all, k_all, v_all))
        kf, vf = mla_kv(ckv_all, kr_all, wl, cfg["ts"])
        yB = mla_attn(q, kf, vf, past, S, cfg["tq"], cfg["tk"], cfg["tkd"])
        yC = sb_attn(seq(sq), k_all, v_all, past, cfg["sb_tq"], cfg["sb_tk"])
        x2 = merge_out(x2, gate, yA.reshape(B * T, WIDTH), zA, yB.reshape(B * T, WIDTH), zB,
                       yC.reshape(B * T, WIDTH), zC, gates, wl, T, cfg["tm"])
        new.append((_bd_to_state(s_fin), pr3[:, -1:], ckv_n, k_rope,
                    sk3.reshape(B, T, HEADS, HEAD_DIM), sv3.reshape(B, T, HEADS, HEAD_DIM)))
    stacked = [jnp.stack([st[i] for st in new], axis=0) for i in range(6)]
    return x2.reshape(B, T, D), stacked


PROMPT_CFG = dict(past=0, tm=512, rwkv_n=128, tm_q=512, ts=512, tq=512, tk=1024, tkd=512, sb_tq=512, sb_tk=256,
                  kv_mult=1024)
SAMPLE_CFG = dict(past=2048, tm=16, rwkv_n=16, tm_q=16, ts=2560, tq=16, tk=2560, tkd=2560, sb_tq=16, sb_tk=512,
                  kv_mult=512)


def kernel(x_prompt, x_sample, state_rwkv_wkv, state_rwkv_shift, cache_mla_ckv, cache_mla_krope, cache_sb_k, cache_sb_v, c_prompt, c_sample, w_ada, b_ada, norm_g, w_in, rwkv_mu, rwkv_w0, rwkv_w_up, rwkv_a0, rwkv_a_up, rwkv_k_k, rwkv_k_a, rwkv_r_k, rwkv_gn_g, rwkv_gn_b, w_br_rwkv, mla_q_norm, mla_w_uq, mla_kv_norm, mla_w_ukv, mla_qn_nope, mla_qn_rope, mla_kn_nope, mla_kn_rope, w_br_mla, w_br_sb, w_out):
    P = dict(w_in=w_in, norm_g=norm_g, rwkv_mu=rwkv_mu, rwkv_w0=rwkv_w0, rwkv_w_up=rwkv_w_up, rwkv_a0=rwkv_a0,
             rwkv_a_up=rwkv_a_up, rwkv_k_k=rwkv_k_k, rwkv_k_a=rwkv_k_a, rwkv_r_k=rwkv_r_k, rwkv_gn_g=rwkv_gn_g,
             rwkv_gn_b=rwkv_gn_b, w_br_rwkv=w_br_rwkv, mla_q_norm=mla_q_norm, mla_w_uq=mla_w_uq,
             mla_kv_norm=mla_kv_norm, mla_w_ukv=mla_w_ukv, mla_qn_nope=mla_qn_nope, mla_qn_rope=mla_qn_rope,
             mla_kn_nope=mla_kn_nope, mla_kn_rope=mla_kn_rope, w_br_mla=w_br_mla, w_br_sb=w_br_sb, w_out=w_out)
    depth = w_in.shape[0]
    bp = x_prompt.shape[0]
    layers = [_prep_layer(P, l) for l in range(depth)]
    offs = np.cumsum([0] + [wd for _, wd in IN_GROUPS])
    src = {name: (offs[i], offs[i + 1]) for i, (name, _) in enumerate(IN_GROUPS)}
    w_in_all = jnp.concatenate([w_in[:, :, src[name][0]:src[name][1]] for name, _ in _in_weight_layout()[0]],
                               axis=-1).astype(BF16)
    mods = ada_mod(jnp.concatenate([c_prompt, c_sample], axis=0), w_ada, b_ada)
    y_p, st_p = _trunk(x_prompt, mods[:, :bp], None, layers, w_in_all, PROMPT_CFG)
    pasts = [(state_rwkv_wkv[l], state_rwkv_shift[l], cache_mla_ckv[l], cache_mla_krope[l],
              cache_sb_k[l], cache_sb_v[l]) for l in range(depth)]
    assert cache_mla_ckv.shape[2] == SAMPLE_CFG["past"]
    y_s, st_s = _trunk(x_sample, mods[:, bp:], pasts, layers, w_in_all, SAMPLE_CFG)
    return (y_p, y_s, *st_p, *st_s)
```

```python
import functools

import numpy as np
import jax
import jax.numpy as jnp
from jax import lax
from jax.experimental import pallas as pl
from jax.experimental.pallas import tpu as pltpu

F32 = jnp.float32
BF16 = jnp.bfloat16

D_MODEL = 1024
CHUNK = 64
RMS_EPS = 1e-6
GN_EPS = 64e-5
NEG_BIG = -1e30
ROPE_THETA = 10000.0
LOG2E = 1.4426950408889634

HEADS = 8
HEAD_DIM = 64
WIDTH = HEADS * HEAD_DIM
PAIRS = HEADS // 2
LANES = 128
LORA = 64
SHIFT_W = 3 * WIDTH + 2 * LORA
Q_LORA = 384
KV_LORA = 256
ROPE = 32
NOPE = 64
QK = NOPE + ROPE
RWKV_CHUNK = 64

IN_GROUPS = (("pr", SHIFT_W), ("zA", WIDTH), ("cq", Q_LORA), ("ckv", KV_LORA), ("kr", ROPE),
             ("zB", WIDTH), ("sq", WIDTH), ("sk", WIDTH), ("sv", WIDTH), ("zC", WIDTH),
             ("gates", 3 * D_MODEL))
BF16_COPIES = ("sk", "sv")
BF16_GROUPS = ("zA", "zB", "zC", "gates")

VMEM_LIMIT = 56 * 1024 * 1024


def _params(sem):
    return pltpu.CompilerParams(dimension_semantics=sem, vmem_limit_bytes=VMEM_LIMIT)


def _bdot(a, b):
    return jnp.dot(a.astype(BF16), b.astype(BF16), preferred_element_type=F32)


def _bdot_nt(a, b):
    return lax.dot_general(a.astype(BF16), b.astype(BF16), (((1,), (1,)), ((), ())),
                           preferred_element_type=F32)


def _bdot_tn(a, b):
    return lax.dot_general(a.astype(BF16), b.astype(BF16), (((0,), (0,)), ((), ())),
                           preferred_element_type=F32)


def _split(x, n):
    parts, r = [], x
    for _ in range(n):
        p = r.astype(BF16)
        parts.append(p)
        r = r - p.astype(F32)
    return parts


def _xdot_r(x, m01, n):
    m = m01.astype(BF16)
    return sum(jnp.dot(p, m, preferred_element_type=F32) for p in _split(x, n))


def _xdot_l(m01, x, n):
    m = m01.astype(BF16)
    return sum(jnp.dot(m, p, preferred_element_type=F32) for p in _split(x, n))


def _sigmoid(x):
    return 1.0 / (1.0 + jnp.exp(-x))


def _silu(x):
    return x * _sigmoid(x)


def _softplus(x):
    return jnp.maximum(x, 0.0) + jnp.log(1.0 + jnp.exp(-jnp.abs(x)))


def _rms(x, g):
    return x * lax.rsqrt(jnp.mean(x * x, axis=-1, keepdims=True) + RMS_EPS) * g


def _iota2(shape, dim):
    return lax.broadcasted_iota(jnp.int32, shape, dim)


def _ada_kernel(c_ref, w_ref, b_ref, o_ref):
    o_ref[0] = _bdot(_silu(c_ref[...]), w_ref[0]) + b_ref[0]


def ada_mod(c, w_ada, b_ada):
    L, D, N = w_ada.shape
    rows = c.shape[0]
    tn = 1024
    return pl.pallas_call(
        _ada_kernel,
        grid=(L, N // tn),
        in_specs=[pl.BlockSpec((rows, D), lambda l, j: (0, 0)),
                  pl.BlockSpec((1, D, tn), lambda l, j: (l, 0, j)),
                  pl.BlockSpec((1, 1, tn), lambda l, j: (l, 0, j))],
        out_specs=pl.BlockSpec((1, rows, tn), lambda l, j: (l, 0, j)),
        out_shape=jax.ShapeDtypeStruct((L, rows, N), F32),
        compiler_params=_params(("parallel", "parallel")),
        name="ada_mod",
    )(c, w_ada, b_ada.reshape(L, 1, N))


def _in_weight_layout():
    order = [g for g in IN_GROUPS if g[0] != "kr"] + [g for g in IN_GROUPS if g[0] == "kr"]
    offs, pos = {}, 0
    for name, wd in order:
        offs[name] = pos
        pos += wd
    return order, offs


def _inproj_kernel(x_ref, sc_ref, sh_ref, g_ref, w_ref, *outs):
    n = len(IN_GROUPS)
    offs = _in_weight_layout()[1]
    h = _rms(x_ref[...], g_ref[...]) * (1.0 + sc_ref[0]) + sh_ref[0]
    hb = h.astype(BF16)
    extra = iter(outs[n:])
    for (name, wd), o in zip(IN_GROUPS, outs):
        res = jnp.dot(hb, w_ref[0, :, offs[name]:offs[name] + wd], preferred_element_type=F32)
        o[...] = res.astype(o.dtype)
        if name in BF16_COPIES:
            next(extra)[...] = res.astype(BF16)


def in_proj(x2, scale, shift, g, w_all, layer, T, tm):
    M, D = x2.shape
    assert T % tm == 0 and M % tm == 0
    per = T // tm
    row = lambda i: (i, 0)
    mod = lambda i: (i // per, 0, 0)
    const = lambda i: (0, 0)
    in_specs = [pl.BlockSpec((tm, D), row), pl.BlockSpec((1, 1, D), mod), pl.BlockSpec((1, 1, D), mod),
                pl.BlockSpec((1, D), const),
                pl.BlockSpec((1, D, w_all.shape[2]), lambda i: (layer, 0, 0), pipeline_mode=pl.Buffered(1))]
    out_specs = [pl.BlockSpec((tm, wd), row) for _, wd in IN_GROUPS]
    out_shape = [jax.ShapeDtypeStruct((M, wd), BF16 if name in BF16_GROUPS else F32) for name, wd in IN_GROUPS]
    out_specs += [pl.BlockSpec((tm, WIDTH), row) for _ in BF16_COPIES]
    out_shape += [jax.ShapeDtypeStruct((M, WIDTH), BF16) for _ in BF16_COPIES]
    return pl.pallas_call(
        _inproj_kernel, grid=(M // tm,), in_specs=in_specs, out_specs=out_specs, out_shape=out_shape,
        compiler_params=_params(("parallel",)), name="in_proj",
    )(x2, scale, shift, g, w_all)


def _rwkv_kernel(n, pr_ref, past_ref, s0_ref, mu_ref, w0_ref, a0_ref, kk_ref, ka_ref, rk_ref,
                 gng_ref, gnb_ref, wup_ref, aup_ref, y_ref, sfin_ref, st_scr, last_scr):
    tb = pl.program_id(1)

    @pl.when(tb == 0)
    def _():
        st_scr[...] = s0_ref[0]
        last_scr[...] = past_ref[0]

    p = pr_ref[0]
    row = _iota2((n, 1), 0)
    prev = jnp.where(row == 0, last_scr[...], pltpu.roll(p, 1, axis=0))
    last_scr[...] = p[n - 1:n, :]
    xs = p + (prev - p) * mu_ref[...]
    r, k, v = xs[:, :WIDTH], xs[:, WIDTH:2 * WIDTH], xs[:, 2 * WIDTH:3 * WIDTH]
    wa = xs[:, 3 * WIDTH:]
    w_log = -_softplus(-(w0_ref[...] + _bdot(jnp.tanh(wa), wup_ref[...]))) - 0.5
    lw = -jnp.exp(w_log)
    a = _sigmoid(a0_ref[...] + _bdot(wa, aup_ref[...]))

    ri, ci = _iota2((LANES, LANES), 0), _iota2((LANES, LANES), 1)
    seg64 = ((ri >> 6) == (ci >> 6)).astype(BF16)
    bd2 = ((ri >> 6) == (ci >> 6)).astype(F32)

    def segsum(t):
        return jnp.concatenate(
            [_xdot_r(t[:, LANES * q:LANES * (q + 1)], seg64, 2) for q in range(PAIRS)], axis=-1)

    kkr = k * kk_ref[...]
    kk = kkr * jnp.minimum(lax.rsqrt(segsum(kkr * kkr)), 1e12)
    k2 = k * (1.0 + (a - 1.0) * ka_ref[...])
    bonus = segsum(r * k2 * rk_ref[...]) * v

    chunk = min(RWKV_CHUNK, n)
    shift = chunk.bit_length() - 1
    tr, tc = _iota2((n, n), 0), _iota2((n, n), 1)
    same = (tr >> shift) == (tc >> shift)
    strict = (same & (tr > tc)).astype(F32)
    incl = (same & (tr >= tc)).astype(F32)
    eye = (tr == tc).astype(F32)
    lw_parts = _split(lw, 2)
    cum = lambda m01: sum(jnp.dot(m01.astype(BF16), part, preferred_element_type=F32) for part in lw_parts)
    g_log = cum(incl)
    g_end = cum(same)
    e_in, e_inv, e_rest = jnp.exp(g_log), jnp.exp(-g_log), jnp.exp(g_end - g_log)
    r_t = r * e_in
    kk_t = kk * jnp.exp(g_log - lw)
    kka = kk * a
    k_t, b_t = k2 * e_inv, kka * e_inv
    k_e, b_e = k2 * e_rest, kka * e_rest
    dec = jnp.exp(g_end)

    lane = _iota2((1, LANES), 1)
    head_mask = [(lane < HEAD_DIM).astype(F32), (lane >= HEAD_DIM).astype(F32)]

    sls = [slice(LANES * q, LANES * (q + 1)) for q in range(PAIRS)]
    units = [(q, hm) for q in range(PAIRS) for hm in head_mask]
    rp, kkp, vp = [r_t[:, s] for s in sls], [kk_t[:, s] for s in sls], [v[:, s] for s in sls]
    kb, bb = [k_t[:, s].astype(BF16) for s in sls], [b_t[:, s].astype(BF16) for s in sls]
    vb, kkb = [x.astype(BF16) for x in vp], [x.astype(BF16) for x in kkp]
    kkm = [(kkp[q] * hm).astype(BF16) for q, hm in units]
    rm = [(rp[q] * hm).astype(BF16) for q, hm in units]
    l_b = [_bdot_nt(kkm[u], bb[q]) * strict for u, (q, _) in enumerate(units)]
    l_k = [_bdot_nt(kkm[u], kb[q]) * strict for u, (q, _) in enumerate(units)]
    r_k = [_bdot_nt(rm[u], kb[q]) * incl for u, (q, _) in enumerate(units)]
    r_b = [_bdot_nt(rm[u], bb[q]) * incl for u, (q, _) in enumerate(units)]
    sub = min(16, chunk)
    sub_shift = sub.bit_length() - 1
    inner = ((tr >> sub_shift) == (tc >> sub_shift)).astype(F32)
    pw = [x * inner for x in l_b]
    t_inv = [eye - x for x in pw]
    for _ in range(sub_shift - 1):
        pw = [_bdot(x, x) for x in pw]
        t_inv = [t + _bdot(t, x) for t, x in zip(t_inv, pw)]
    for level in range(sub_shift, shift):
        joined = (((tr >> (level + 1)) == (tc >> (level + 1))) & ((tr >> level) != (tc >> level))).astype(F32)
        t_inv = [t - _bdot(_bdot(t, x * joined), t) for t, x in zip(t_inv, l_b)]
    lkv = [_bdot(l_k[u], vb[q]) for u, (q, _) in enumerate(units)]
    w_h = [_bdot(t_inv[u], kkb[q]) for u, (q, _) in enumerate(units)]
    u0_h = [_bdot(t, x) for t, x in zip(t_inv, lkv)]
    q_h = [rp[q] - _bdot(r_b[u], w_h[u]) for u, (q, _) in enumerate(units)]
    y0_h = [_bdot(r_k[u], vb[q]) - _bdot(r_b[u], u0_h[u]) for u, (q, _) in enumerate(units)]
    first = head_mask[0] > 0.5
    pair = lambda xs: [jnp.where(first, xs[2 * q], xs[2 * q + 1]) for q in range(PAIRS)]
    w_c, u0_c, q_c, y0_c = pair(w_h), pair(u0_h), pair(q_h), pair(y0_h)

    st = [st_scr[q] for q in range(PAIRS)]
    ys = [[] for _ in range(PAIRS)]
    rows = [slice(chunk * c, chunk * (c + 1)) for c in range(n // chunk)]
    n_c = [[(_bdot_tn(vp[q][rs], k_e[rs, sls[q]]) - _bdot_tn(u0_c[q][rs], b_e[rs, sls[q]])) * bd2
            for q in range(PAIRS)] for rs in rows]
    a_c = [[_bdot_tn(w_c[q][rs], b_e[rs, sls[q]]) * bd2 for q in range(PAIRS)] for rs in rows]
    for c, rs in enumerate(rows):
        for q in range(PAIRS):
            ys[q].append(_bdot_nt(q_c[q][rs], st[q]) + y0_c[q][rs])
        sa = [_bdot(st[q], a_c[c][q]) for q in range(PAIRS)]
        st = [st[q] * dec[chunk * c:chunk * c + 1, sls[q]] + (n_c[c][q] - sa[q]) for q in range(PAIRS)]
    for q in range(PAIRS):
        st_scr[q] = st[q]
    y = [x[0] if len(x) == 1 else jnp.concatenate(x, axis=0) for x in ys]
    mean = [_xdot_r(x, seg64, 2) * (1.0 / HEAD_DIM) for x in y]
    d = [x - m for x, m in zip(y, mean)]
    var = [_xdot_r(x * x, seg64, 2) * (1.0 / HEAD_DIM) for x in d]
    for q, s in enumerate(sls):
        yn = d[q] * lax.rsqrt(var[q] + GN_EPS) * gng_ref[:, s] + gnb_ref[:, s] + bonus[:, s]
        y_ref[0, :, s] = yn.astype(y_ref.dtype)

    @pl.when(tb == pl.num_programs(1) - 1)
    def _():
        sfin_ref[0] = st_scr[...]


def rwkv_branch(pr, past_shift, s0_bd, wl, n):
    B, T, _ = pr.shape
    assert T % n == 0 and n % min(RWKV_CHUNK, n) == 0 and n & (n - 1) == 0
    vec = lambda wd: pl.BlockSpec((1, wd), lambda b, t: (0, 0))
    in_specs = [pl.BlockSpec((1, n, SHIFT_W), lambda b, t: (b, t, 0)),
                pl.BlockSpec((1, 1, SHIFT_W), lambda b, t: (b, 0, 0)),
                pl.BlockSpec((1, PAIRS, LANES, LANES), lambda b, t: (b, 0, 0, 0)),
                vec(SHIFT_W)] + [vec(WIDTH)] * 7 + [
                pl.BlockSpec((LANES, WIDTH), lambda b, t: (0, 0)),
                pl.BlockSpec((LANES, WIDTH), lambda b, t: (0, 0))]
    return pl.pallas_call(
        functools.partial(_rwkv_kernel, n),
        grid=(B, T // n),
        in_specs=in_specs,
        out_specs=[pl.BlockSpec((1, n, WIDTH), lambda b, t: (b, t, 0)),
                   pl.BlockSpec((1, PAIRS, LANES, LANES), lambda b, t: (b, 0, 0, 0))],
        out_shape=[jax.ShapeDtypeStruct((B, T, WIDTH), BF16),
                   jax.ShapeDtypeStruct((B, PAIRS, LANES, LANES), F32)],
        scratch_shapes=[pltpu.VMEM((PAIRS, LANES, LANES), F32), pltpu.VMEM((1, SHIFT_W), F32)],
        compiler_params=_params(("parallel", "arbitrary")),
        name="rwkv7",
    )(pr, past_shift, s0_bd, wl["mu"], wl["w0"], wl["a0"], wl["k_k"], wl["k_a"], wl["r_k"],
      wl["gn_g"], wl["gn_b"], wl["w_up"], wl["a_up"])


def _state_to_bd(s):
    B = s.shape[0]
    s = s.reshape(B, PAIRS, 2, HEAD_DIM, HEAD_DIM)
    z = jnp.zeros_like(s[:, :, 0])
    top = jnp.concatenate([s[:, :, 0], z], axis=-1)
    bot = jnp.concatenate([z, s[:, :, 1]], axis=-1)
    return jnp.concatenate([top, bot], axis=-2)


def _bd_to_state(s):
    B = s.shape[0]
    h0 = s[:, :, :HEAD_DIM, :HEAD_DIM]
    h1 = s[:, :, HEAD_DIM:, HEAD_DIM:]
    return jnp.stack([h0, h1], axis=2).reshape(B, HEADS, HEAD_DIM, HEAD_DIM)


def _mla_q_kernel(cq_ref, ckv_ref, kr_ref, c96_ref, s96_ref, c32_ref, s32_ref, gq_ref, wuq_ref, wrot_ref, gqn_ref,
                  grot_ref, gkv_ref, gkr_ref, p32_ref, q_ref, ckvn_ref, krope_ref):
    cqb = _rms(cq_ref[0], gq_ref[...]).astype(BF16)
    nope = _iota2((1, QK), 1) < NOPE
    gc = gqn_ref[...] * c96_ref[...]
    gs = grot_ref[...] * s96_ref[...]
    for h in range(HEADS):
        qh = jnp.dot(cqb, wuq_ref[h], preferred_element_type=F32)
        qr = jnp.dot(cqb, wrot_ref[h], preferred_element_type=F32)
        sq = qh * qh
        ss_n = jnp.sum(jnp.where(nope, sq, 0.0), axis=-1, keepdims=True)
        ss_r = jnp.sum(jnp.where(nope, 0.0, sq), axis=-1, keepdims=True)
        inv = jnp.where(nope, lax.rsqrt(ss_n * (1.0 / NOPE) + RMS_EPS), lax.rsqrt(ss_r * (1.0 / ROPE) + RMS_EPS))
        qo = inv * (qh * gc + qr * gs)
        q_ref[0, h] = (qo * (QK ** -0.5 * LOG2E)).astype(BF16)
    ckvn_ref[0] = _rms(ckv_ref[0], gkv_ref[...])
    krn = _rms(kr_ref[0], gkr_ref[...])
    krope_ref[0] = krn * c32_ref[...] + _xdot_r(krn, p32_ref[...], 3) * s32_ref[...]


def mla_q(cq, ckv, kr, tabs, wl, tm):
    B, T, _ = cq.shape
    c96, s96, c32, s32 = tabs
    tok = lambda wd: pl.BlockSpec((1, tm, wd), lambda b, t: (b, t, 0))
    tab = lambda wd: pl.BlockSpec((tm, wd), lambda b, t: (t, 0))
    vec = lambda wd: pl.BlockSpec((1, wd), lambda b, t: (0, 0))
    return pl.pallas_call(
        _mla_q_kernel,
        grid=(B, T // tm),
        in_specs=[tok(Q_LORA), tok(KV_LORA), tok(ROPE), tab(QK), tab(QK), tab(ROPE), tab(ROPE),
                  vec(Q_LORA), pl.BlockSpec((HEADS, Q_LORA, QK), lambda b, t: (0, 0, 0)),
                  pl.BlockSpec((HEADS, Q_LORA, QK), lambda b, t: (0, 0, 0)), vec(QK), vec(QK),
                  vec(KV_LORA), vec(ROPE), pl.BlockSpec((ROPE, ROPE), lambda b, t: (0, 0))],
        out_specs=[pl.BlockSpec((1, HEADS, tm, QK), lambda b, t: (b, 0, t, 0)), tok(KV_LORA), tok(ROPE)],
        out_shape=[jax.ShapeDtypeStruct((B, HEADS, T, QK), BF16),
                   jax.ShapeDtypeStruct((B, T, KV_LORA), F32),
                   jax.ShapeDtypeStruct((B, T, ROPE), F32)],
        compiler_params=_params(("parallel", "parallel")),
        name="mla_q",
    )(cq, ckv, kr, c96, s96, c32, s32, wl["q_norm"], wl["w_uq"], wl["w_uq_rot"], wl["qn"], wl["qn_rot"],
      wl["kv_norm"], wl["kn_rope"], wl["p32"])


def _mla_kv_kernel(ckv_ref, kr_ref, wuk_ref, wuv_ref, gkn_ref, e_ref, k_ref, v_ref):
    cb = ckv_ref[0].astype(BF16)
    k_rope = _xdot_r(kr_ref[0], e_ref[...], 3)
    for h in range(HEADS):
        kh = jnp.dot(cb, wuk_ref[h], preferred_element_type=F32)
        ms = jnp.sum(kh * kh, axis=-1, keepdims=True) * (1.0 / NOPE)
        k_ref[0, h] = (kh * lax.rsqrt(ms + RMS_EPS) * gkn_ref[...] + k_rope).astype(BF16)
    for q in range(PAIRS):
        v_ref[0, q] = jnp.dot(cb, wuv_ref[q], preferred_element_type=F32).astype(BF16)


def mla_kv(ckv_all, kr_all, wl, ts):
    B, S, _ = ckv_all.shape
    return pl.pallas_call(
        _mla_kv_kernel,
        grid=(B, S // ts),
        in_specs=[pl.BlockSpec((1, ts, KV_LORA), lambda b, t: (b, t, 0)),
                  pl.BlockSpec((1, ts, ROPE), lambda b, t: (b, t, 0)),
                  pl.BlockSpec((HEADS, KV_LORA, QK), lambda b, t: (0, 0, 0)),
                  pl.BlockSpec((PAIRS, KV_LORA, LANES), lambda b, t: (0, 0, 0)),
                  pl.BlockSpec((1, QK), lambda b, t: (0, 0)),
                  pl.BlockSpec((ROPE, QK), lambda b, t: (0, 0))],
        out_specs=[pl.BlockSpec((1, HEADS, ts, QK), lambda b, t: (b, 0, t, 0)),
                   pl.BlockSpec((1, PAIRS, ts, LANES), lambda b, t: (b, 0, t, 0))],
        out_shape=[jax.ShapeDtypeStruct((B, HEADS, S, QK), BF16),
                   jax.ShapeDtypeStruct((B, PAIRS, S, LANES), BF16)],
        compiler_params=_params(("parallel", "parallel")),
        name="mla_kv",
    )(ckv_all, kr_all, wl["w_uk"], wl["w_uv"], wl["kn_nope"], wl["e96"])


def _mla_attn_kernel(tq, tk, tkd, past, s_len, q_ref, k_ref, v_ref, o_ref, m_scr, l_scr, acc_scr):
    q0 = past + pl.program_id(2) * tq
    first = _iota2((1, LANES), 1) < HEAD_DIM
    m_scr[...] = jnp.full(m_scr.shape, NEG_BIG, F32)
    l_scr[...] = jnp.zeros(l_scr.shape, F32)
    acc_scr[...] = jnp.zeros(acc_scr.shape, F32)
    qs = [q_ref[0, h] for h in range(2)]

    def block(start, width, masked):
        reps = width // LANES
        lane_chunks = lambda x: [x[:, LANES * c:LANES * (c + 1)] for c in range(reps)]
        ks = pl.multiple_of(start, width)
        s = [_bdot_nt(qs[h], k_ref[0, h, pl.ds(ks, width), :]) for h in range(2)]
        if masked:
            limit = jnp.minimum((((q0 + _iota2((tq, 1), 0)) >> 6) + 1) << 6, s_len)
            visible = (ks + _iota2((1, width), 1)) < limit
            s = [jnp.where(visible, x, NEG_BIG) for x in s]
        vb = v_ref[0, 0, pl.ds(ks, width), :]
        m_old = [m_scr[h] for h in range(2)]
        m_blk = [functools.reduce(jnp.maximum, lane_chunks(x)) for x in s]
        m_new = [jnp.maximum(m_old[h], jnp.max(m_blk[h], axis=-1, keepdims=True)) for h in range(2)]
        alpha = [jnp.exp2(m_old[h] - m_new[h]) for h in range(2)]
        p = [jnp.exp2(s[h] - jnp.concatenate([m_new[h]] * reps, axis=1)) for h in range(2)]
        for h in range(2):
            l_scr[h] = functools.reduce(jnp.add, lane_chunks(p[h]), alpha[h] * l_scr[h])
            m_scr[h] = m_new[h]
        pv = _bdot(jnp.concatenate(p, axis=0), vb)
        acc_scr[...] = acc_scr[...] * jnp.where(first, alpha[0], alpha[1]) + jnp.where(first, pv[:tq], pv[tq:])

    n_full = jnp.minimum((q0 // CHUNK + 1) * CHUNK, s_len) // tk
    seen = jnp.minimum(((q0 + tq - 1) // CHUNK + 1) * CHUNK, s_len)
    n_diag = (seen - n_full * tk + tkd - 1) // tkd

    def full_body(kb, carry):
        block(kb * tk, tk, False)
        return carry

    def diag_body(i, carry):
        block(n_full * tk + i * tkd, tkd, True)
        return carry

    lax.fori_loop(0, n_full, full_body, 0)
    lax.fori_loop(0, n_diag, diag_body, 0)
    l = [jnp.sum(l_scr[h], axis=-1, keepdims=True) for h in range(2)]
    o_ref[0] = (acc_scr[...] / jnp.where(first, l[0], l[1])).astype(o_ref.dtype)


def mla_attn(q, k, v, past, s_len, tq, tk, tkd):
    B, _, T, _ = q.shape
    S = k.shape[2]
    assert S % tk == 0 and tk % tkd == 0 and tkd % LANES == 0 and T % tq == 0
    return pl.pallas_call(
        functools.partial(_mla_attn_kernel, tq, tk, tkd, past, s_len),
        grid=(B, PAIRS, T // tq),
        in_specs=[pl.BlockSpec((1, 2, tq, QK), lambda b, p, i: (b, p, i, 0)),
                  pl.BlockSpec((1, 2, S, QK), lambda b, p, i: (b, p, 0, 0)),
                  pl.BlockSpec((1, 1, S, LANES), lambda b, p, i: (b, p, 0, 0))],
        out_specs=pl.BlockSpec((1, tq, LANES), lambda b, p, i: (b, i, p)),
        out_shape=jax.ShapeDtypeStruct((B, T, WIDTH), BF16),
        scratch_shapes=[pltpu.VMEM((2, tq, LANES), F32), pltpu.VMEM((2, tq, LANES), F32),
                        pltpu.VMEM((tq, LANES), F32)],
        compiler_params=_params(("parallel", "parallel", "arbitrary")),
        name="mla_attn",
    )(q, k, v)


def _sb_kernel(tq, tk, past, q_ref, k_ref, v_ref, sum_ref, o_ref, carry_scr, acc_scr, qm_scr, z_scr):
    nk = k_ref.shape[1] // tk
    q0 = past + pl.program_id(2) * tq
    first = _iota2((1, LANES), 1) < HEAD_DIM
    carry_scr[...] = jnp.zeros(carry_scr.shape, F32)
    acc_scr[...] = jnp.zeros(acc_scr.shape, F32)
    qv = q_ref[0] * (HEAD_DIM ** -0.5 * LOG2E)
    qm_scr[0] = jnp.where(first, qv, 0.0).astype(BF16)
    qm_scr[1] = jnp.where(first, 0.0, qv).astype(BF16)

    def key_mask(kb, r0):
        return (kb * tk + _iota2((1, tk), 1)) < (q0 + r0 + _iota2((tq - r0, 1), 0))

    def stage_z(kb, r0=0):
        kblk = k_ref[0, pl.ds(pl.multiple_of(kb * tk, tk), tk), :]
        n = tq - r0
        z = _bdot_nt(qm_scr[:, r0:, :].reshape(2 * n, LANES), kblk)
        return [z[:n], z[n:]]

    def stage_w(z, mask):
        split = []
        for read in z:
            x = read()
            neg_abs = lax.bitcast_convert_type(lax.bitcast_convert_type(x, jnp.uint32) | jnp.uint32(0x80000000), F32)
            w = jnp.maximum(x, 0.0) + jnp.log2(1.0 + jnp.exp2(neg_abs))
            if mask is not None:
                w = jnp.where(mask, w, 0.0)
            hi = w.astype(BF16)
            split.append(jnp.concatenate([hi, (w - hi.astype(F32)).astype(BF16)], axis=1))
        return split

    def stage_incl(split):
        both = jnp.dot(jnp.concatenate(split, axis=0), sum_ref[...], preferred_element_type=F32)
        n = split[0].shape[0]
        incl = [both[:n], both[n:]]
        return incl, [x[:, 0:1] for x in incl]

    def stage_a(z, incl, c, mask):
        cb = [jnp.concatenate([x] * (tk // LANES), axis=1) for x in c]
        a = [jnp.exp2(z[h]() - incl[h] - cb[h]) for h in range(2)]
        if mask is not None:
            a = [jnp.where(mask, x, 0.0) for x in a]
        return a

    def stage_pv(a, kb):
        vblk = v_ref[0, pl.ds(pl.multiple_of(kb * tk, tk), tk), :]
        pv = _bdot(jnp.concatenate(a, axis=0), vblk)
        n = a[0].shape[0]
        return jnp.where(first, pv[:n], pv[n:])

    def one_block(kb, masked, r0=0):
        mask = key_mask(kb, r0) if masked else None
        z = [lambda v=v: v for v in stage_z(kb, r0)]
        incl, tot = stage_incl(stage_w(z, mask))
        c = [carry_scr[h, r0:, :] for h in range(2)]
        a = stage_a(z, incl, c, mask)
        acc_scr[r0:, :] += stage_pv(a, kb)
        for h in range(2):
            carry_scr[h, r0:, :] = c[h] + tot[h]

    def prefetch_z(kb, slot):
        for j in range(2):
            z = stage_z(jnp.maximum(kb - j, 0))
            for h in range(2):
                z_scr[slot, j, h] = z[h]

    def two_blocks(kb, slot):
        z1, z2 = ([lambda j=j, h=h: z_scr[slot, j, h] for h in range(2)] for j in range(2))
        incl1, tot1 = stage_incl(stage_w(z1, None))
        prefetch_z(kb - 2, 1 - slot)
        incl2, tot2 = stage_incl(stage_w(z2, None))
        c1 = [carry_scr[h] for h in range(2)]
        pv1 = stage_pv(stage_a(z1, incl1, c1, None), kb)
        c2 = [c1[h] + tot1[h] for h in range(2)]
        pv2 = stage_pv(stage_a(z2, incl2, c2, None), kb - 1)
        acc_scr[...] += pv1 + pv2
        for h in range(2):
            carry_scr[h] = c2[h] + tot2[h]

    n_full = q0 // tk
    last = jnp.clip((q0 + tq - 2) // tk, 0, nk - 1)

    def loop(count, fn):
        def body(i, carry):
            fn(i)
            return carry
        lax.fori_loop(0, count, body, 0)

    if past % tk == 0 and tq % tk == 0:
        for j in reversed(range(tq // tk)):
            one_block(n_full + j, True, j * tk)
    else:
        loop(last - n_full + 1, lambda i: one_block(last - i, True))

    def four_blocks(i):
        two_blocks(n_full - 1 - 4 * i, 0)
        two_blocks(n_full - 3 - 4 * i, 1)

    rest = n_full % 4
    prefetch_z(n_full - 1, 0)
    loop(n_full // 4, four_blocks)
    loop(rest // 2, lambda i: two_blocks(rest - 1, 0))
    loop(rest % 2, lambda i: one_block(0, False))
    o_ref[0] = acc_scr[...].astype(o_ref.dtype)


def sb_attn(q, k_all, v_all, past, tq, tk):
    B, T, _ = q.shape
    S = k_all.shape[1]
    assert S % tk == 0 and T % tq == 0 and tk % LANES == 0
    incl = np.arange(tk)[:, None] >= np.arange(tk)[None, :]
    sum_mat = jnp.asarray(np.concatenate([incl, incl], axis=0), BF16)
    return pl.pallas_call(
        functools.partial(_sb_kernel, tq, tk, past),
        grid=(B, PAIRS, T // tq),
        in_specs=[pl.BlockSpec((1, tq, LANES), lambda b, p, i: (b, i, p)),
                  pl.BlockSpec((1, S, LANES), lambda b, p, i: (b, 0, p)),
                  pl.BlockSpec((1, S, LANES), lambda b, p, i: (b, 0, p)),
                  pl.BlockSpec((2 * tk, tk), lambda b, p, i: (0, 0))],
        out_specs=pl.BlockSpec((1, tq, LANES), lambda b, p, i: (b, i, p)),
        out_shape=jax.ShapeDtypeStruct((B, T, WIDTH), BF16),
        scratch_shapes=[pltpu.VMEM((2, tq, LANES), F32), pltpu.VMEM((tq, LANES), F32),
                        pltpu.VMEM((2, tq, LANES), BF16), pltpu.VMEM((2, 2, 2, tq, tk), F32)],
        compiler_params=_params(("parallel", "parallel", "arbitrary")),
        name="sb_attn",
    )(q, k_all, v_all, sum_mat)


def _merge_kernel(x_ref, gm_ref, ya_ref, za_ref, yb_ref, zb_ref, yc_ref, zc_ref, g_ref,
                  wa_ref, wb_ref, wc_ref, wo_ref, o_ref):
    def branch(y_ref, z_ref, w_ref):
        return _bdot(y_ref[...].astype(F32) * _silu(z_ref[...].astype(F32)), w_ref[...])

    sg = _sigmoid(g_ref[...].astype(F32))
    merged = (sg[:, :D_MODEL] * branch(ya_ref, za_ref, wa_ref)
              + sg[:, D_MODEL:2 * D_MODEL] * branch(yb_ref, zb_ref, wb_ref)
              + sg[:, 2 * D_MODEL:] * branch(yc_ref, zc_ref, wc_ref))
    o_ref[...] = x_ref[...] + gm_ref[0] * _bdot(merged, wo_ref[...])


def merge_out(x2, gate_mod, ya, za, yb, zb, yc, zc, gates, wl, T, tm):
    M, D = x2.shape
    per = T // tm
    row = lambda wd: pl.BlockSpec((tm, wd), lambda i: (i, 0))
    const = lambda r, c: pl.BlockSpec((r, c), lambda i: (0, 0))
    return pl.pallas_call(
        _merge_kernel,
        grid=(M // tm,),
        in_specs=[row(D), pl.BlockSpec((1, 1, D), lambda i: (i // per, 0, 0))] + [row(WIDTH)] * 6 + [row(3 * D),
                  const(WIDTH, D), const(WIDTH, D), const(WIDTH, D), const(D, D)],
        out_specs=row(D),
        out_shape=jax.ShapeDtypeStruct((M, D), F32),
        compiler_params=_params(("parallel",)),
        name="merge_out",
    )(x2, gate_mod, ya, za, yb, zb, yc, zc, gates, wl["w_br_rwkv"], wl["w_br_mla"], wl["w_br_sb"], wl["w_out"])


def _rot_matrix(width, offset):
    half = ROPE // 2
    m = np.zeros((width, width), np.float32)
    for i in range(half):
        m[offset + half + i, offset + i] = -1.0
        m[offset + i, offset + half + i] = 1.0
    return jnp.asarray(m, BF16)


def _prep_layer(P, l):
    row = lambda a: a[l].reshape(1, -1)
    zeros_lora = jnp.zeros((LORA, WIDTH), F32)
    w_uq = P["mla_w_uq"][l].reshape(Q_LORA, HEADS, QK).transpose(1, 0, 2)
    w_ukv = P["mla_w_ukv"][l].reshape(KV_LORA, HEADS, 2 * HEAD_DIM)
    w_uk = jnp.pad(w_ukv[:, :, :NOPE].transpose(1, 0, 2), ((0, 0), (0, 0), (0, ROPE)))
    w_uv = w_ukv[:, :, NOPE:].reshape(KV_LORA, PAIRS, LANES).transpose(1, 0, 2)
    e96 = np.zeros((ROPE, QK), np.float32)
    e96[np.arange(ROPE), NOPE + np.arange(ROPE)] = 1.0
    half = ROPE // 2
    w_uq_rot = jnp.concatenate([jnp.zeros_like(w_uq[:, :, :NOPE]), -w_uq[:, :, NOPE + half:], w_uq[:, :, NOPE:NOPE + half]],
                               axis=-1)
    g_rope = P["mla_qn_rope"][l]
    qn_rot = jnp.concatenate([jnp.zeros((NOPE,), F32), g_rope[half:], g_rope[:half]]).reshape(1, QK)
    return dict(
        norm_g=row(P["norm_g"]),
        mu=row(P["rwkv_mu"]), w0=row(P["rwkv_w0"]), a0=row(P["rwkv_a0"]), k_k=row(P["rwkv_k_k"]),
        k_a=row(P["rwkv_k_a"]), r_k=row(P["rwkv_r_k"]), gn_g=row(P["rwkv_gn_g"]), gn_b=row(P["rwkv_gn_b"]),
        w_up=jnp.concatenate([P["rwkv_w_up"][l], zeros_lora], axis=0).astype(BF16),
        a_up=jnp.concatenate([zeros_lora, P["rwkv_a_up"][l]], axis=0).astype(BF16),
        q_norm=row(P["mla_q_norm"]), w_uq=w_uq.astype(BF16), w_uq_rot=w_uq_rot.astype(BF16), qn_rot=qn_rot,
        qn=jnp.concatenate([P["mla_qn_nope"][l], P["mla_qn_rope"][l]]).reshape(1, QK),
        kv_norm=row(P["mla_kv_norm"]), kn_rope=row(P["mla_kn_rope"]),
        kn_nope=jnp.pad(P["mla_kn_nope"][l], (0, ROPE)).reshape(1, QK),
        w_uk=w_uk.astype(BF16), w_uv=w_uv.astype(BF16),
        p32=_rot_matrix(ROPE, 0), e96=jnp.asarray(e96, BF16),
        w_br_rwkv=P["w_br_rwkv"][l].astype(BF16), w_br_mla=P["w_br_mla"][l].astype(BF16),
        w_br_sb=P["w_br_sb"][l].astype(BF16), w_out=P["w_out"][l].astype(BF16),
    )


def _rope_tables(past, T):
    half = ROPE // 2
    inv = ROPE_THETA ** (-jnp.arange(half, dtype=F32) / half)
    ang = (past + jnp.arange(T, dtype=jnp.int32)).astype(F32)[:, None] * inv
    c32 = jnp.tile(jnp.cos(ang), (1, 2))
    s32 = jnp.tile(jnp.sin(ang), (1, 2))
    c96 = jnp.concatenate([jnp.ones((T, NOPE), F32), c32], axis=1)
    s96 = jnp.concatenate([jnp.zeros((T, NOPE), F32), s32], axis=1)
    return c96, s96, c32, s32


def _pad_rows(a, rows):
    return a if a.shape[1] == rows else jnp.pad(a, ((0, 0), (0, rows - a.shape[1])) + ((0, 0),) * (a.ndim - 2))


def _trunk(x, mods, pasts, layers, w_in_all, cfg):
    B, T, D = x.shape
    past = cfg["past"]
    S = past + T
    s_pad = -(-S // cfg["kv_mult"]) * cfg["kv_mult"]
    tabs = _rope_tables(past, T)
    x2 = x.reshape(B * T, D)
    new = []
    for l, wl in enumerate(layers):
        shift, scale, gate = (mods[l][:, None, i * D:(i + 1) * D] for i in range(3))
        outs = in_proj(x2, scale, shift, wl["norm_g"], w_in_all, l, T, cfg["tm"])
        pr, zA, cq, ckv, kr, zB, sq, sk, sv, zC, gates, sk_b, sv_b = outs
        seq = lambda a: a.reshape(B, T, a.shape[-1])
        if pasts is None:
            s0 = jnp.zeros((B, PAIRS, LANES, LANES), F32)
            shift0 = jnp.zeros((B, 1, SHIFT_W), F32)
        else:
            s0, shift0 = _state_to_bd(pasts[l][0]), pasts[l][1]
        pr3 = seq(pr)
        yA, s_fin = rwkv_branch(pr3, shift0, s0, wl, cfg["rwkv_n"])
        q, ckv_n, k_rope = mla_q(seq(cq), seq(ckv), seq(kr), tabs, wl, cfg["tm_q"])
        sk3, sv3 = seq(sk), seq(sv)
        if pasts is None:
            ckv_all, kr_all, k_all, v_all = ckv_n, k_rope, seq(sk_b), seq(sv_b)
        else:
            _, _, ckv0, kr0, k0, v0 = pasts[l]
            ckv_all = jnp.concatenate([ckv0, ckv_n], axis=1)
            kr_all = jnp.concatenate([kr0, k_rope], axis=1)
            k_all = jnp.concatenate([k0.reshape(B, past, WIDTH).astype(BF16), seq(sk_b)], axis=1)
            v_all = jnp.concatenate([v0.reshape(B, past, WIDTH).astype(BF16), seq(sv_b)], axis=1)
        ckv_all, kr_all, k_all, v_all = (_pad_rows(a, s_pad) for a in (ckv_all, kr_all, k_all, v_all))
        kf, vf = mla_kv(ckv_all, kr_all, wl, cfg["ts"])
        yB = mla_attn(q, kf, vf, past, S, cfg["tq"], cfg["tk"], cfg["tkd"])
        yC = sb_attn(seq(sq), k_all, v_all, past, cfg["sb_tq"], cfg["sb_tk"])
        x2 = merge_out(x2, gate, yA.reshape(B * T, WIDTH), zA, yB.reshape(B * T, WIDTH), zB,
                       yC.reshape(B * T, WIDTH), zC, gates, wl, T, cfg["tm"])
        new.append((_bd_to_state(s_fin), pr3[:, -1:], ckv_n, k_rope,
                    sk3.reshape(B, T, HEADS, HEAD_DIM), sv3.reshape(B, T, HEADS, HEAD_DIM)))
    stacked = [jnp.stack([st[i] for st in new], axis=0) for i in range(6)]
    return x2.reshape(B, T, D), stacked


PROMPT_CFG = dict(past=0, tm=512, rwkv_n=128, tm_q=512, ts=512, tq=512, tk=1024, tkd=512, sb_tq=512, sb_tk=256,
                  kv_mult=1024)
SAMPLE_CFG = dict(past=2048, tm=16, rwkv_n=16, tm_q=16, ts=2560, tq=16, tk=2560, tkd=2560, sb_tq=16, sb_tk=512,
                  kv_mult=512)


def kernel(x_prompt, x_sample, state_rwkv_wkv, state_rwkv_shift, cache_mla_ckv, cache_mla_krope, cache_sb_k, cache_sb_v, c_prompt, c_sample, w_ada, b_ada, norm_g, w_in, rwkv_mu, rwkv_w0, rwkv_w_up, rwkv_a0, rwkv_a_up, rwkv_k_k, rwkv_k_a, rwkv_r_k, rwkv_gn_g, rwkv_gn_b, w_br_rwkv, mla_q_norm, mla_w_uq, mla_kv_norm, mla_w_ukv, mla_qn_nope, mla_qn_rope, mla_kn_nope, mla_kn_rope, w_br_mla, w_br_sb, w_out):
    P = dict(w_in=w_in, norm_g=norm_g, rwkv_mu=rwkv_mu, rwkv_w0=rwkv_w0, rwkv_w_up=rwkv_w_up, rwkv_a0=rwkv_a0,
             rwkv_a_up=rwkv_a_up, rwkv_k_k=rwkv_k_k, rwkv_k_a=rwkv_k_a, rwkv_r_k=rwkv_r_k, rwkv_gn_g=rwkv_gn_g,
             rwkv_gn_b=rwkv_gn_b, w_br_rwkv=w_br_rwkv, mla_q_norm=mla_q_norm, mla_w_uq=mla_w_uq,
             mla_kv_norm=mla_kv_norm, mla_w_ukv=mla_w_ukv, mla_qn_nope=mla_qn_nope, mla_qn_rope=mla_qn_rope,
             mla_kn_nope=mla_kn_nope, mla_kn_rope=mla_kn_rope, w_br_mla=w_br_mla, w_br_sb=w_br_sb, w_out=w_out)
    depth = w_in.shape[0]
    bp = x_prompt.shape[0]
    layers = [_prep_layer(P, l) for l in range(depth)]
    offs = np.cumsum([0] + [wd for _, wd in IN_GROUPS])
    src = {name: (offs[i], offs[i + 1]) for i, (name, _) in enumerate(IN_GROUPS)}
    w_in_all = jnp.concatenate([w_in[:, :, src[name][0]:src[name][1]] for name, _ in _in_weight_layout()[0]],
                               axis=-1).astype(BF16)
    mods = ada_mod(jnp.concatenate([c_prompt, c_sample], axis=0), w_ada, b_ada)
    y_p, st_p = _trunk(x_prompt, mods[:, :bp], None, layers, w_in_all, PROMPT_CFG)
    pasts = [(state_rwkv_wkv[l], state_rwkv_shift[l], cache_mla_ckv[l], cache_mla_krope[l],
              cache_sb_k[l], cache_sb_v[l]) for l in range(depth)]
    assert cache_mla_ckv.shape[2] == SAMPLE_CFG["past"]
    y_s, st_s = _trunk(x_sample, mods[:, bp:], pasts, layers, w_in_all, SAMPLE_CFG)
    return (y_p, y_s, *st_p, *st_s)
```

```python
import functools

import numpy as np
import jax
import jax.numpy as jnp
from jax import lax
from jax.experimental import pallas as pl
from jax.experimental.pallas import tpu as pltpu

F32 = jnp.float32
BF16 = jnp.bfloat16

D_MODEL = 1024
CHUNK = 64
RMS_EPS = 1e-6
GN_EPS = 64e-5
NEG_BIG = -1e30
ROPE_THETA = 10000.0
LOG2E = 1.4426950408889634

HEADS = 8
HEAD_DIM = 64
WIDTH = HEADS * HEAD_DIM
PAIRS = HEADS // 2
LANES = 128
HEAD_SHIFT = HEAD_DIM.bit_length() - 1
CHUNK_SHIFT = CHUNK.bit_length() - 1
LORA = 64
SHIFT_W = 3 * WIDTH + 2 * LORA
Q_LORA = 384
KV_LORA = 256
ROPE = 32
NOPE = 64
QK = NOPE + ROPE
RWKV_CHUNK = 64

IN_GROUPS = (("pr", SHIFT_W), ("zA", WIDTH), ("cq", Q_LORA), ("ckv", KV_LORA), ("kr", ROPE),
             ("zB", WIDTH), ("sq", WIDTH), ("sk", WIDTH), ("sv", WIDTH), ("zC", WIDTH),
             ("gates", 3 * D_MODEL))
BF16_COPIES = ("sk", "sv")
BF16_GROUPS = ("zA", "zB", "zC", "gates")

VMEM_LIMIT = 56 * 1024 * 1024


def _params(sem):
    return pltpu.CompilerParams(dimension_semantics=sem, vmem_limit_bytes=VMEM_LIMIT)


def _bdot(a, b):
    return jnp.dot(a.astype(BF16), b.astype(BF16), preferred_element_type=F32)


def _bdot_nt(a, b):
    return lax.dot_general(a.astype(BF16), b.astype(BF16), (((1,), (1,)), ((), ())),
                           preferred_element_type=F32)


def _bdot_tn(a, b):
    return lax.dot_general(a.astype(BF16), b.astype(BF16), (((0,), (0,)), ((), ())),
                           preferred_element_type=F32)


def _split(x, n):
    parts, r = [], x
    for _ in range(n):
        p = r.astype(BF16)
        parts.append(p)
        r = r - p.astype(F32)
    return parts


def _xdot_r(x, m01, n):
    m = m01.astype(BF16)
    return sum(jnp.dot(p, m, preferred_element_type=F32) for p in _split(x, n))


def _sigmoid(x):
    return 1.0 / (1.0 + jnp.exp(-x))


def _silu(x):
    return x * _sigmoid(x)


def _softplus(x):
    return jnp.maximum(x, 0.0) + jnp.log(1.0 + jnp.exp(-jnp.abs(x)))


def _rms(x, g):
    return x * lax.rsqrt(jnp.mean(x * x, axis=-1, keepdims=True) + RMS_EPS) * g


def _iota2(shape, dim):
    return lax.broadcasted_iota(jnp.int32, shape, dim)


def _ada_kernel(c_ref, w_ref, b_ref, o_ref):
    o_ref[0] = _bdot(_silu(c_ref[...]), w_ref[0]) + b_ref[0]


def ada_mod(c, w_ada, b_ada):
    L, D, N = w_ada.shape
    rows = c.shape[0]
    tn = 1024
    return pl.pallas_call(
        _ada_kernel,
        grid=(L, N // tn),
        in_specs=[pl.BlockSpec((rows, D), lambda l, j: (0, 0)),
                  pl.BlockSpec((1, D, tn), lambda l, j: (l, 0, j)),
                  pl.BlockSpec((1, 1, tn), lambda l, j: (l, 0, j))],
        out_specs=pl.BlockSpec((1, rows, tn), lambda l, j: (l, 0, j)),
        out_shape=jax.ShapeDtypeStruct((L, rows, N), F32),
        compiler_params=_params(("parallel", "parallel")),
        name="ada_mod",
    )(c, w_ada, b_ada.reshape(L, 1, N))


def _in_weight_layout():
    order = [g for g in IN_GROUPS if g[0] != "kr"] + [g for g in IN_GROUPS if g[0] == "kr"]
    offs, pos = {}, 0
    for name, wd in order:
        offs[name] = pos
        pos += wd
    return order, offs


def _inproj_kernel(x_ref, sc_ref, sh_ref, g_ref, w_ref, *outs):
    n = len(IN_GROUPS)
    offs = _in_weight_layout()[1]
    h = _rms(x_ref[...], g_ref[...]) * (1.0 + sc_ref[0]) + sh_ref[0]
    hb = h.astype(BF16)
    extra = iter(outs[n:])
    for (name, wd), o in zip(IN_GROUPS, outs):
        res = jnp.dot(hb, w_ref[0, :, offs[name]:offs[name] + wd], preferred_element_type=F32)
        o[...] = res.astype(o.dtype)
        if name in BF16_COPIES:
            next(extra)[...] = res.astype(BF16)


def in_proj(x2, scale, shift, g, w_all, layer, T, tm):
    M, D = x2.shape
    assert T % tm == 0 and M % tm == 0
    per = T // tm
    row = lambda i: (i, 0)
    mod = lambda i: (i // per, 0, 0)
    const = lambda i: (0, 0)
    in_specs = [pl.BlockSpec((tm, D), row), pl.BlockSpec((1, 1, D), mod), pl.BlockSpec((1, 1, D), mod),
                pl.BlockSpec((1, D), const),
                pl.BlockSpec((1, D, w_all.shape[2]), lambda i: (layer, 0, 0), pipeline_mode=pl.Buffered(1))]
    out_specs = [pl.BlockSpec((tm, wd), row) for _, wd in IN_GROUPS]
    out_shape = [jax.ShapeDtypeStruct((M, wd), BF16 if name in BF16_GROUPS else F32) for name, wd in IN_GROUPS]
    out_specs += [pl.BlockSpec((tm, WIDTH), row) for _ in BF16_COPIES]
    out_shape += [jax.ShapeDtypeStruct((M, WIDTH), BF16) for _ in BF16_COPIES]
    return pl.pallas_call(
        _inproj_kernel, grid=(M // tm,), in_specs=in_specs, out_specs=out_specs, out_shape=out_shape,
        compiler_params=_params(("parallel",)), name="in_proj",
    )(x2, scale, shift, g, w_all)


def _rwkv_kernel(n, pr_ref, past_ref, s0_ref, mu_ref, w0_ref, a0_ref, kk_ref, ka_ref, rk_ref,
                 gng_ref, gnb_ref, wup_ref, aup_ref, y_ref, sfin_ref, st_scr, last_scr):
    tb = pl.program_id(1)

    @pl.when(tb == 0)
    def _():
        st_scr[...] = s0_ref[0]
        last_scr[...] = past_ref[0]

    p = pr_ref[0]
    row = _iota2((n, 1), 0)
    prev = jnp.where(row == 0, last_scr[...], pltpu.roll(p, 1, axis=0))
    last_scr[...] = p[n - 1:n, :]
    xs = p + (prev - p) * mu_ref[...]
    r, k, v = xs[:, :WIDTH], xs[:, WIDTH:2 * WIDTH], xs[:, 2 * WIDTH:3 * WIDTH]
    wa = xs[:, 3 * WIDTH:]
    w_log = -_softplus(-(w0_ref[...] + _bdot(jnp.tanh(wa), wup_ref[...]))) - 0.5
    lw = -jnp.exp(w_log)
    a = _sigmoid(a0_ref[...] + _bdot(wa, aup_ref[...]))

    ri, ci = _iota2((LANES, LANES), 0), _iota2((LANES, LANES), 1)
    same_head = (ri >> HEAD_SHIFT) == (ci >> HEAD_SHIFT)
    seg64 = same_head.astype(BF16)
    bd2 = same_head.astype(F32)

    def segsum(t):
        return jnp.concatenate(
            [_xdot_r(t[:, LANES * q:LANES * (q + 1)], seg64, 2) for q in range(PAIRS)], axis=-1)

    kkr = k * kk_ref[...]
    kk = kkr * jnp.minimum(lax.rsqrt(segsum(kkr * kkr)), 1e12)
    k2 = k * (1.0 + (a - 1.0) * ka_ref[...])
    bonus = segsum(r * k2 * rk_ref[...]) * v

    chunk = min(RWKV_CHUNK, n)
    shift = chunk.bit_length() - 1
    tr, tc = _iota2((n, n), 0), _iota2((n, n), 1)
    same = (tr >> shift) == (tc >> shift)
    strict = (same & (tr > tc)).astype(F32)
    incl = (same & (tr >= tc)).astype(F32)
    eye = (tr == tc).astype(F32)
    lw_parts = _split(lw, 2)
    cum = lambda m01: sum(jnp.dot(m01.astype(BF16), part, preferred_element_type=F32) for part in lw_parts)
    g_log = cum(incl)
    g_end = cum(same)
    e_in, e_inv, e_rest = jnp.exp(g_log), jnp.exp(-g_log), jnp.exp(g_end - g_log)
    r_t = r * e_in
    kk_t = kk * jnp.exp(g_log - lw)
    kka = kk * a
    k_t, b_t = k2 * e_inv, kka * e_inv
    k_e, b_e = k2 * e_rest, kka * e_rest
    dec = jnp.exp(g_end)

    lane = _iota2((1, LANES), 1)
    head_mask = [(lane < HEAD_DIM).astype(F32), (lane >= HEAD_DIM).astype(F32)]

    sls = [slice(LANES * q, LANES * (q + 1)) for q in range(PAIRS)]
    units = [(q, hm) for q in range(PAIRS) for hm in head_mask]
    rp, kkp, vp = [r_t[:, s] for s in sls], [kk_t[:, s] for s in sls], [v[:, s] for s in sls]
    kb, bb = [k_t[:, s].astype(BF16) for s in sls], [b_t[:, s].astype(BF16) for s in sls]
    vb, kkb = [x.astype(BF16) for x in vp], [x.astype(BF16) for x in kkp]
    kkm = [(kkp[q] * hm).astype(BF16) for q, hm in units]
    rm = [(rp[q] * hm).astype(BF16) for q, hm in units]
    l_b = [_bdot_nt(kkm[u], bb[q]) * strict for u, (q, _) in enumerate(units)]
    l_k = [_bdot_nt(kkm[u], kb[q]) * strict for u, (q, _) in enumerate(units)]
    r_k = [_bdot_nt(rm[u], kb[q]) * incl for u, (q, _) in enumerate(units)]
    r_b = [_bdot_nt(rm[u], bb[q]) * incl for u, (q, _) in enumerate(units)]
    sub = min(16, chunk)
    sub_shift = sub.bit_length() - 1
    inner = ((tr >> sub_shift) == (tc >> sub_shift)).astype(F32)
    pw = [x * inner for x in l_b]
    t_inv = [eye - x for x in pw]
    for _ in range(sub_shift - 1):
        pw = [_bdot(x, x) for x in pw]
        t_inv = [t + _bdot(t, x) for t, x in zip(t_inv, pw)]
    for level in range(sub_shift, shift):
        joined = (((tr >> (level + 1)) == (tc >> (level + 1))) & ((tr >> level) != (tc >> level))).astype(F32)
        t_inv = [t - _bdot(_bdot(t, x * joined), t) for t, x in zip(t_inv, l_b)]
    lkv = [_bdot(l_k[u], vb[q]) for u, (q, _) in enumerate(units)]
    w_h = [_bdot(t_inv[u], kkb[q]) for u, (q, _) in enumerate(units)]
    u0_h = [_bdot(t, x) for t, x in zip(t_inv, lkv)]
    q_h = [rp[q] - _bdot(r_b[u], w_h[u]) for u, (q, _) in enumerate(units)]
    y0_h = [_bdot(r_k[u], vb[q]) - _bdot(r_b[u], u0_h[u]) for u, (q, _) in enumerate(units)]
    first = head_mask[0] > 0.5
    pair = lambda xs: [jnp.where(first, xs[2 * q], xs[2 * q + 1]) for q in range(PAIRS)]
    w_c, u0_c, q_c, y0_c = pair(w_h), pair(u0_h), pair(q_h), pair(y0_h)

    st = [st_scr[q] for q in range(PAIRS)]
    ys = [[] for _ in range(PAIRS)]
    rows = [slice(chunk * c, chunk * (c + 1)) for c in range(n // chunk)]
    n_c = [[(_bdot_tn(vp[q][rs], k_e[rs, sls[q]]) - _bdot_tn(u0_c[q][rs], b_e[rs, sls[q]])) * bd2
            for q in range(PAIRS)] for rs in rows]
    a_c = [[_bdot_tn(w_c[q][rs], b_e[rs, sls[q]]) * bd2 for q in range(PAIRS)] for rs in rows]
    for c, rs in enumerate(rows):
        for q in range(PAIRS):
            ys[q].append(_bdot_nt(q_c[q][rs], st[q]) + y0_c[q][rs])
        sa = [_bdot(st[q], a_c[c][q]) for q in range(PAIRS)]
        st = [st[q] * dec[chunk * c:chunk * c + 1, sls[q]] + (n_c[c][q] - sa[q]) for q in range(PAIRS)]
    for q in range(PAIRS):
        st_scr[q] = st[q]
    y = [x[0] if len(x) == 1 else jnp.concatenate(x, axis=0) for x in ys]
    mean = [_xdot_r(x, seg64, 2) * (1.0 / HEAD_DIM) for x in y]
    d = [x - m for x, m in zip(y, mean)]
    var = [_xdot_r(x * x, seg64, 2) * (1.0 / HEAD_DIM) for x in d]
    for q, s in enumerate(sls):
        yn = d[q] * lax.rsqrt(var[q] + GN_EPS) * gng_ref[:, s] + gnb_ref[:, s] + bonus[:, s]
        y_ref[0, :, s] = yn.astype(y_ref.dtype)

    @pl.when(tb == pl.num_programs(1) - 1)
    def _():
        sfin_ref[0] = st_scr[...]


def rwkv_branch(pr, past_shift, s0_bd, wl, n):
    B, T, _ = pr.shape
    assert T % n == 0 and n % min(RWKV_CHUNK, n) == 0 and n & (n - 1) == 0
    vec = lambda wd: pl.BlockSpec((1, wd), lambda b, t: (0, 0))
    in_specs = [pl.BlockSpec((1, n, SHIFT_W), lambda b, t: (b, t, 0)),
                pl.BlockSpec((1, 1, SHIFT_W), lambda b, t: (b, 0, 0)),
                pl.BlockSpec((1, PAIRS, LANES, LANES), lambda b, t: (b, 0, 0, 0)),
                vec(SHIFT_W)] + [vec(WIDTH)] * 7 + [
                pl.BlockSpec((LANES, WIDTH), lambda b, t: (0, 0)),
                pl.BlockSpec((LANES, WIDTH), lambda b, t: (0, 0))]
    return pl.pallas_call(
        functools.partial(_rwkv_kernel, n),
        grid=(B, T // n),
        in_specs=in_specs,
        out_specs=[pl.BlockSpec((1, n, WIDTH), lambda b, t: (b, t, 0)),
                   pl.BlockSpec((1, PAIRS, LANES, LANES), lambda b, t: (b, 0, 0, 0))],
        out_shape=[jax.ShapeDtypeStruct((B, T, WIDTH), BF16),
                   jax.ShapeDtypeStruct((B, PAIRS, LANES, LANES), F32)],
        scratch_shapes=[pltpu.VMEM((PAIRS, LANES, LANES), F32), pltpu.VMEM((1, SHIFT_W), F32)],
        compiler_params=_params(("parallel", "arbitrary")),
        name="rwkv7",
    )(pr, past_shift, s0_bd, wl["mu"], wl["w0"], wl["a0"], wl["k_k"], wl["k_a"], wl["r_k"],
      wl["gn_g"], wl["gn_b"], wl["w_up"], wl["a_up"])


def _state_to_bd(s):
    B = s.shape[0]
    s = s.reshape(B, PAIRS, 2, HEAD_DIM, HEAD_DIM)
    z = jnp.zeros_like(s[:, :, 0])
    top = jnp.concatenate([s[:, :, 0], z], axis=-1)
    bot = jnp.concatenate([z, s[:, :, 1]], axis=-1)
    return jnp.concatenate([top, bot], axis=-2)


def _bd_to_state(s):
    B = s.shape[0]
    h0 = s[:, :, :HEAD_DIM, :HEAD_DIM]
    h1 = s[:, :, HEAD_DIM:, HEAD_DIM:]
    return jnp.stack([h0, h1], axis=2).reshape(B, HEADS, HEAD_DIM, HEAD_DIM)


def _mla_q_kernel(cq_ref, ckv_ref, kr_ref, c96_ref, s96_ref, c32_ref, s32_ref, gq_ref, wuq_ref, wrot_ref, gqn_ref,
                  grot_ref, gkv_ref, gkr_ref, p32_ref, q_ref, ckvn_ref, krope_ref):
    cqb = _rms(cq_ref[0], gq_ref[...]).astype(BF16)
    nope = _iota2((1, QK), 1) < NOPE
    gc = gqn_ref[...] * c96_ref[...]
    gs = grot_ref[...] * s96_ref[...]
    for h in range(HEADS):
        qh = jnp.dot(cqb, wuq_ref[h], preferred_element_type=F32)
        qr = jnp.dot(cqb, wrot_ref[h], preferred_element_type=F32)
        sq = qh * qh
        ss_n = jnp.sum(jnp.where(nope, sq, 0.0), axis=-1, keepdims=True)
        ss_r = jnp.sum(jnp.where(nope, 0.0, sq), axis=-1, keepdims=True)
        inv = jnp.where(nope, lax.rsqrt(ss_n * (1.0 / NOPE) + RMS_EPS), lax.rsqrt(ss_r * (1.0 / ROPE) + RMS_EPS))
        qo = inv * (qh * gc + qr * gs)
        q_ref[0, h] = (qo * (QK ** -0.5 * LOG2E)).astype(BF16)
    ckvn_ref[0] = _rms(ckv_ref[0], gkv_ref[...])
    krn = _rms(kr_ref[0], gkr_ref[...])
    krope_ref[0] = krn * c32_ref[...] + _xdot_r(krn, p32_ref[...], 3) * s32_ref[...]


def mla_q(cq, ckv, kr, tabs, wl, tm):
    B, T, _ = cq.shape
    c96, s96, c32, s32 = tabs
    tok = lambda wd: pl.BlockSpec((1, tm, wd), lambda b, t: (b, t, 0))
    tab = lambda wd: pl.BlockSpec((tm, wd), lambda b, t: (t, 0))
    vec = lambda wd: pl.BlockSpec((1, wd), lambda b, t: (0, 0))
    return pl.pallas_call(
        _mla_q_kernel,
        grid=(B, T // tm),
        in_specs=[tok(Q_LORA), tok(KV_LORA), tok(ROPE), tab(QK), tab(QK), tab(ROPE), tab(ROPE),
                  vec(Q_LORA), pl.BlockSpec((HEADS, Q_LORA, QK), lambda b, t: (0, 0, 0)),
                  pl.BlockSpec((HEADS, Q_LORA, QK), lambda b, t: (0, 0, 0)), vec(QK), vec(QK),
                  vec(KV_LORA), vec(ROPE), pl.BlockSpec((ROPE, ROPE), lambda b, t: (0, 0))],
        out_specs=[pl.BlockSpec((1, HEADS, tm, QK), lambda b, t: (b, 0, t, 0)), tok(KV_LORA), tok(ROPE)],
        out_shape=[jax.ShapeDtypeStruct((B, HEADS, T, QK), BF16),
                   jax.ShapeDtypeStruct((B, T, KV_LORA), F32),
                   jax.ShapeDtypeStruct((B, T, ROPE), F32)],
        compiler_params=_params(("parallel", "parallel")),
        name="mla_q",
    )(cq, ckv, kr, c96, s96, c32, s32, wl["q_norm"], wl["w_uq"], wl["w_uq_rot"], wl["qn"], wl["qn_rot"],
      wl["kv_norm"], wl["kn_rope"], wl["p32"])


def _mla_kv_kernel(ckv_ref, kr_ref, wuk_ref, wuv_ref, gkn_ref, e_ref, k_ref, v_ref):
    cb = ckv_ref[0].astype(BF16)
    k_rope = _xdot_r(kr_ref[0], e_ref[...], 3)
    for h in range(HEADS):
        kh = jnp.dot(cb, wuk_ref[h], preferred_element_type=F32)
        ms = jnp.sum(kh * kh, axis=-1, keepdims=True) * (1.0 / NOPE)
        k_ref[0, h] = (kh * lax.rsqrt(ms + RMS_EPS) * gkn_ref[...] + k_rope).astype(BF16)
    for q in range(PAIRS):
        v_ref[0, q] = jnp.dot(cb, wuv_ref[q], preferred_element_type=F32).astype(BF16)


def mla_kv(ckv_all, kr_all, wl, ts):
    B, S, _ = ckv_all.shape
    return pl.pallas_call(
        _mla_kv_kernel,
        grid=(B, S // ts),
        in_specs=[pl.BlockSpec((1, ts, KV_LORA), lambda b, t: (b, t, 0)),
                  pl.BlockSpec((1, ts, ROPE), lambda b, t: (b, t, 0)),
                  pl.BlockSpec((HEADS, KV_LORA, QK), lambda b, t: (0, 0, 0)),
                  pl.BlockSpec((PAIRS, KV_LORA, LANES), lambda b, t: (0, 0, 0)),
                  pl.BlockSpec((1, QK), lambda b, t: (0, 0)),
                  pl.BlockSpec((ROPE, QK), lambda b, t: (0, 0))],
        out_specs=[pl.BlockSpec((1, HEADS, ts, QK), lambda b, t: (b, 0, t, 0)),
                   pl.BlockSpec((1, PAIRS, ts, LANES), lambda b, t: (b, 0, t, 0))],
        out_shape=[jax.ShapeDtypeStruct((B, HEADS, S, QK), BF16),
                   jax.ShapeDtypeStruct((B, PAIRS, S, LANES), BF16)],
        compiler_params=_params(("parallel", "parallel")),
        name="mla_kv",
    )(ckv_all, kr_all, wl["w_uk"], wl["w_uv"], wl["kn_nope"], wl["e96"])


def _mla_attn_kernel(tq, tk, tkd, past, s_len, q_ref, k_ref, v_ref, o_ref, m_scr, l_scr, acc_scr):
    q0 = past + pl.program_id(2) * tq
    first = _iota2((1, LANES), 1) < HEAD_DIM
    m_scr[...] = jnp.full(m_scr.shape, NEG_BIG, F32)
    l_scr[...] = jnp.zeros(l_scr.shape, F32)
    acc_scr[...] = jnp.zeros(acc_scr.shape, F32)
    qs = [q_ref[0, h] for h in range(2)]

    def block(start, width, masked):
        reps = width // LANES
        lane_chunks = lambda x: [x[:, LANES * c:LANES * (c + 1)] for c in range(reps)]
        ks = pl.multiple_of(start, width)
        s = [_bdot_nt(qs[h], k_ref[0, h, pl.ds(ks, width), :]) for h in range(2)]
        if masked:
            limit = jnp.minimum((((q0 + _iota2((tq, 1), 0)) >> CHUNK_SHIFT) + 1) << CHUNK_SHIFT, s_len)
            visible = (ks + _iota2((1, width), 1)) < limit
            s = [jnp.where(visible, x, NEG_BIG) for x in s]
        vb = v_ref[0, 0, pl.ds(ks, width), :]
        m_old = [m_scr[h] for h in range(2)]
        m_blk = [functools.reduce(jnp.maximum, lane_chunks(x)) for x in s]
        m_new = [jnp.maximum(m_old[h], jnp.max(m_blk[h], axis=-1, keepdims=True)) for h in range(2)]
        alpha = [jnp.exp2(m_old[h] - m_new[h]) for h in range(2)]
        p = [jnp.exp2(s[h] - jnp.concatenate([m_new[h]] * reps, axis=1)) for h in range(2)]
        for h in range(2):
            l_scr[h] = functools.reduce(jnp.add, lane_chunks(p[h]), alpha[h] * l_scr[h])
            m_scr[h] = m_new[h]
        pv = _bdot(jnp.concatenate(p, axis=0), vb)
        acc_scr[...] = acc_scr[...] * jnp.where(first, alpha[0], alpha[1]) + jnp.where(first, pv[:tq], pv[tq:])

    open_end = jnp.minimum((q0 // CHUNK + 1) * CHUNK, s_len)
    seen = jnp.minimum(((q0 + tq - 1) // CHUNK + 1) * CHUNK, s_len)
    n_full = open_end // tk
    n_open = open_end // tkd - n_full * (tk // tkd)
    diag0 = (open_end // tkd) * tkd
    n_diag = (seen - diag0 + tkd - 1) // tkd

    def full_body(kb, carry):
        block(kb * tk, tk, False)
        return carry

    def open_body(i, carry):
        block(n_full * tk + i * tkd, tkd, False)
        return carry

    def diag_body(i, carry):
        block(diag0 + i * tkd, tkd, True)
        return carry

    lax.fori_loop(0, n_full, full_body, 0)
    lax.fori_loop(0, n_open, open_body, 0)
    lax.fori_loop(0, n_diag, diag_body, 0)
    l = [jnp.sum(l_scr[h], axis=-1, keepdims=True) for h in range(2)]
    o_ref[0] = (acc_scr[...] / jnp.where(first, l[0], l[1])).astype(o_ref.dtype)


def mla_attn(q, k, v, past, s_len, tq, tk, tkd):
    B, _, T, _ = q.shape
    S = k.shape[2]
    assert S % tk == 0 and tk % tkd == 0 and tkd % LANES == 0 and T % tq == 0
    return pl.pallas_call(
        functools.partial(_mla_attn_kernel, tq, tk, tkd, past, s_len),
        grid=(B, PAIRS, T // tq),
        in_specs=[pl.BlockSpec((1, 2, tq, QK), lambda b, p, i: (b, p, i, 0)),
                  pl.BlockSpec((1, 2, S, QK), lambda b, p, i: (b, p, 0, 0)),
                  pl.BlockSpec((1, 1, S, LANES), lambda b, p, i: (b, p, 0, 0))],
        out_specs=pl.BlockSpec((1, tq, LANES), lambda b, p, i: (b, i, p)),
        out_shape=jax.ShapeDtypeStruct((B, T, WIDTH), BF16),
        scratch_shapes=[pltpu.VMEM((2, tq, LANES), F32), pltpu.VMEM((2, tq, LANES), F32),
                        pltpu.VMEM((tq, LANES), F32)],
        compiler_params=_params(("parallel", "parallel", "arbitrary")),
        name="mla_attn",
    )(q, k, v)


def _sb_kernel(tq, tk, past, q_ref, k_ref, v_ref, sum_ref, o_ref, carry_scr, acc_scr, qm_scr, z_scr):
    nk = k_ref.shape[1] // tk
    q0 = past + pl.program_id(2) * tq
    first = _iota2((1, LANES), 1) < HEAD_DIM
    carry_scr[...] = jnp.zeros(carry_scr.shape, F32)
    acc_scr[...] = jnp.zeros(acc_scr.shape, F32)
    qv = q_ref[0] * (HEAD_DIM ** -0.5 * LOG2E)
    qm_scr[0] = jnp.where(first, qv, 0.0).astype(BF16)
    qm_scr[1] = jnp.where(first, 0.0, qv).astype(BF16)

    def key_mask(kb, r0):
        return (kb * tk + _iota2((1, tk), 1)) < (q0 + r0 + _iota2((tq - r0, 1), 0))

    def stage_z(kb, r0=0):
        kblk = k_ref[0, pl.ds(pl.multiple_of(kb * tk, tk), tk), :]
        n = tq - r0
        z = _bdot_nt(qm_scr[:, r0:, :].reshape(2 * n, LANES), kblk)
        return [z[:n], z[n:]]

    def stage_w(z, mask):
        split = []
        for read in z:
            x = read()
            neg_abs = lax.bitcast_convert_type(lax.bitcast_convert_type(x, jnp.uint32) | jnp.uint32(0x80000000), F32)
            w = jnp.maximum(x, 0.0) + jnp.log2(1.0 + jnp.exp2(neg_abs))
            if mask is not None:
                w = jnp.where(mask, w, 0.0)
            hi = w.astype(BF16)
            split.append(jnp.concatenate([hi, (w - hi.astype(F32)).astype(BF16)], axis=1))
        return split

    def stage_incl(split):
        both = jnp.dot(jnp.concatenate(split, axis=0), sum_ref[...], preferred_element_type=F32)
        n = split[0].shape[0]
        incl = [both[:n], both[n:]]
        return incl, [x[:, 0:1] for x in incl]

    def stage_a(z, incl, c, mask):
        cb = [jnp.concatenate([x] * (tk // LANES), axis=1) for x in c]
        a = [jnp.exp2(z[h]() - incl[h] - cb[h]) for h in range(2)]
        if mask is not None:
            a = [jnp.where(mask, x, 0.0) for x in a]
        return a

    def stage_pv(a, kb):
        vblk = v_ref[0, pl.ds(pl.multiple_of(kb * tk, tk), tk), :]
        pv = _bdot(jnp.concatenate(a, axis=0), vblk)
        n = a[0].shape[0]
        return jnp.where(first, pv[:n], pv[n:])

    def one_block(kb, masked, r0=0):
        mask = key_mask(kb, r0) if masked else None
        z = [lambda v=v: v for v in stage_z(kb, r0)]
        incl, tot = stage_incl(stage_w(z, mask))
        c = [carry_scr[h, r0:, :] for h in range(2)]
        a = stage_a(z, incl, c, mask)
        acc_scr[r0:, :] += stage_pv(a, kb)
        for h in range(2):
            carry_scr[h, r0:, :] = c[h] + tot[h]

    def prefetch_z(kb, slot):
        for j in range(2):
            z = stage_z(jnp.maximum(kb - j, 0))
            for h in range(2):
                z_scr[slot, j, h] = z[h]

    def two_blocks(kb, slot):
        z1, z2 = ([lambda j=j, h=h: z_scr[slot, j, h] for h in range(2)] for j in range(2))
        incl1, tot1 = stage_incl(stage_w(z1, None))
        prefetch_z(kb - 2, 1 - slot)
        incl2, tot2 = stage_incl(stage_w(z2, None))
        c1 = [carry_scr[h] for h in range(2)]
        pv1 = stage_pv(stage_a(z1, incl1, c1, None), kb)
        c2 = [c1[h] + tot1[h] for h in range(2)]
        pv2 = stage_pv(stage_a(z2, incl2, c2, None), kb - 1)
        acc_scr[...] += pv1 + pv2
        for h in range(2):
            carry_scr[h] = c2[h] + tot2[h]

    n_full = q0 // tk
    last = jnp.clip((q0 + tq - 2) // tk, 0, nk - 1)

    def loop(count, fn):
        def body(i, carry):
            fn(i)
            return carry
        lax.fori_loop(0, count, body, 0)

    if past % tk == 0 and tq % tk == 0:
        for j in reversed(range(tq // tk)):
            one_block(n_full + j, True, j * tk)
    else:
        loop(last - n_full + 1, lambda i: one_block(last - i, True))

    def four_blocks(i):
        two_blocks(n_full - 1 - 4 * i, 0)
        two_blocks(n_full - 3 - 4 * i, 1)

    rest = n_full % 4
    prefetch_z(n_full - 1, 0)
    loop(n_full // 4, four_blocks)
    loop(rest // 2, lambda i: two_blocks(rest - 1, 0))
    loop(rest % 2, lambda i: one_block(0, False))
    o_ref[0] = acc_scr[...].astype(o_ref.dtype)


def sb_attn(q, k_all, v_all, past, tq, tk):
    B, T, _ = q.shape
    S = k_all.shape[1]
    assert S % tk == 0 and T % tq == 0 and tk % LANES == 0
    incl = np.arange(tk)[:, None] >= np.arange(tk)[None, :]
    sum_mat = jnp.asarray(np.concatenate([incl, incl], axis=0), BF16)
    return pl.pallas_call(
        functools.partial(_sb_kernel, tq, tk, past),
        grid=(B, PAIRS, T // tq),
        in_specs=[pl.BlockSpec((1, tq, LANES), lambda b, p, i: (b, i, p)),
                  pl.BlockSpec((1, S, LANES), lambda b, p, i: (b, 0, p)),
                  pl.BlockSpec((1, S, LANES), lambda b, p, i: (b, 0, p)),
                  pl.BlockSpec((2 * tk, tk), lambda b, p, i: (0, 0))],
        out_specs=pl.BlockSpec((1, tq, LANES), lambda b, p, i: (b, i, p)),
        out_shape=jax.ShapeDtypeStruct((B, T, WIDTH), BF16),
        scratch_shapes=[pltpu.VMEM((2, tq, LANES), F32), pltpu.VMEM((tq, LANES), F32),
                        pltpu.VMEM((2, tq, LANES), BF16), pltpu.VMEM((2, 2, 2, tq, tk), F32)],
        compiler_params=_params(("parallel", "parallel", "arbitrary")),
        name="sb_attn",
    )(q, k_all, v_all, sum_mat)


def _merge_kernel(x_ref, gm_ref, ya_ref, za_ref, yb_ref, zb_ref, yc_ref, zc_ref, g_ref,
                  wa_ref, wb_ref, wc_ref, wo_ref, o_ref):
    def branch(y_ref, z_ref, w_ref):
        return _bdot(y_ref[...].astype(F32) * _silu(z_ref[...].astype(F32)), w_ref[...])

    sg = _sigmoid(g_ref[...].astype(F32))
    merged = (sg[:, :D_MODEL] * branch(ya_ref, za_ref, wa_ref)
              + sg[:, D_MODEL:2 * D_MODEL] * branch(yb_ref, zb_ref, wb_ref)
              + sg[:, 2 * D_MODEL:] * branch(yc_ref, zc_ref, wc_ref))
    o_ref[...] = x_ref[...] + gm_ref[0] * _bdot(merged, wo_ref[...])


def merge_out(x2, gate_mod, ya, za, yb, zb, yc, zc, gates, wl, T, tm):
    M, D = x2.shape
    per = T // tm
    row = lambda wd: pl.BlockSpec((tm, wd), lambda i: (i, 0))
    const = lambda r, c: pl.BlockSpec((r, c), lambda i: (0, 0))
    return pl.pallas_call(
        _merge_kernel,
        grid=(M // tm,),
        in_specs=[row(D), pl.BlockSpec((1, 1, D), lambda i: (i // per, 0, 0))] + [row(WIDTH)] * 6 + [row(3 * D),
                  const(WIDTH, D), const(WIDTH, D), const(WIDTH, D), const(D, D)],
        out_specs=row(D),
        out_shape=jax.ShapeDtypeStruct((M, D), F32),
        compiler_params=_params(("parallel",)),
        name="merge_out",
    )(x2, gate_mod, ya, za, yb, zb, yc, zc, gates, wl["w_br_rwkv"], wl["w_br_mla"], wl["w_br_sb"], wl["w_out"])


def _rot_matrix(width, offset):
    half = ROPE // 2
    m = np.zeros((width, width), np.float32)
    for i in range(half):
        m[offset + half + i, offset + i] = -1.0
        m[offset + i, offset + half + i] = 1.0
    return jnp.asarray(m, BF16)


def _prep_layer(P, l):
    row = lambda a: a[l].reshape(1, -1)
    zeros_lora = jnp.zeros((LORA, WIDTH), F32)
    w_uq = P["mla_w_uq"][l].reshape(Q_LORA, HEADS, QK).transpose(1, 0, 2)
    w_ukv = P["mla_w_ukv"][l].reshape(KV_LORA, HEADS, 2 * HEAD_DIM)
    w_uk = jnp.pad(w_ukv[:, :, :NOPE].transpose(1, 0, 2), ((0, 0), (0, 0), (0, ROPE)))
    w_uv = w_ukv[:, :, NOPE:].reshape(KV_LORA, PAIRS, LANES).transpose(1, 0, 2)
    e96 = np.zeros((ROPE, QK), np.float32)
    e96[np.arange(ROPE), NOPE + np.arange(ROPE)] = 1.0
    half = ROPE // 2
    w_uq_rot = jnp.concatenate([jnp.zeros_like(w_uq[:, :, :NOPE]), -w_uq[:, :, NOPE + half:], w_uq[:, :, NOPE:NOPE + half]],
                               axis=-1)
    g_rope = P["mla_qn_rope"][l]
    qn_rot = jnp.concatenate([jnp.zeros((NOPE,), F32), g_rope[half:], g_rope[:half]]).reshape(1, QK)
    return dict(
        norm_g=row(P["norm_g"]),
        mu=row(P["rwkv_mu"]), w0=row(P["rwkv_w0"]), a0=row(P["rwkv_a0"]), k_k=row(P["rwkv_k_k"]),
        k_a=row(P["rwkv_k_a"]), r_k=row(P["rwkv_r_k"]), gn_g=row(P["rwkv_gn_g"]), gn_b=row(P["rwkv_gn_b"]),
        w_up=jnp.concatenate([P["rwkv_w_up"][l], zeros_lora], axis=0).astype(BF16),
        a_up=jnp.concatenate([zeros_lora, P["rwkv_a_up"][l]], axis=0).astype(BF16),
        q_norm=row(P["mla_q_norm"]), w_uq=w_uq.astype(BF16), w_uq_rot=w_uq_rot.astype(BF16), qn_rot=qn_rot,
        qn=jnp.concatenate([P["mla_qn_nope"][l], P["mla_qn_rope"][l]]).reshape(1, QK),
        kv_norm=row(P["mla_kv_norm"]), kn_rope=row(P["mla_kn_rope"]),
        kn_nope=jnp.pad(P["mla_kn_nope"][l], (0, ROPE)).reshape(1, QK),
        w_uk=w_uk.astype(BF16), w_uv=w_uv.astype(BF16),
        p32=_rot_matrix(ROPE, 0), e96=jnp.asarray(e96, BF16),
        w_br_rwkv=P["w_br_rwkv"][l].astype(BF16), w_br_mla=P["w_br_mla"][l].astype(BF16),
        w_br_sb=P["w_br_sb"][l].astype(BF16), w_out=P["w_out"][l].astype(BF16),
    )


def _rope_tables(past, T):
    half = ROPE // 2
    inv = ROPE_THETA ** (-jnp.arange(half, dtype=F32) / half)
    ang = (past + jnp.arange(T, dtype=jnp.int32)).astype(F32)[:, None] * inv
    c32 = jnp.tile(jnp.cos(ang), (1, 2))
    s32 = jnp.tile(jnp.sin(ang), (1, 2))
    c96 = jnp.concatenate([jnp.ones((T, NOPE), F32), c32], axis=1)
    s96 = jnp.concatenate([jnp.zeros((T, NOPE), F32), s32], axis=1)
    return c96, s96, c32, s32


def _pad_rows(a, rows):
    return a if a.shape[1] == rows else jnp.pad(a, ((0, 0), (0, rows - a.shape[1])) + ((0, 0),) * (a.ndim - 2))


def _trunk(x, mods, pasts, layers, w_in_all, cfg):
    B, T, D = x.shape
    past = cfg["past"]
    S = past + T
    s_pad = -(-S // cfg["kv_mult"]) * cfg["kv_mult"]
    tabs = _rope_tables(past, T)
    x2 = x.reshape(B * T, D)
    new = []
    for l, wl in enumerate(layers):
        shift, scale, gate = (mods[l][:, None, i * D:(i + 1) * D] for i in range(3))
        outs = in_proj(x2, scale, shift, wl["norm_g"], w_in_all, l, T, cfg["tm"])
        pr, zA, cq, ckv, kr, zB, sq, sk, sv, zC, gates, sk_b, sv_b = outs
        seq = lambda a: a.reshape(B, T, a.shape[-1])
        if pasts is None:
            s0 = jnp.zeros((B, PAIRS, LANES, LANES), F32)
            shift0 = jnp.zeros((B, 1, SHIFT_W), F32)
        else:
            s0, shift0 = _state_to_bd(pasts[l][0]), pasts[l][1]
        pr3 = seq(pr)
        yA, s_fin = rwkv_branch(pr3, shift0, s0, wl, cfg["rwkv_n"])
        q, ckv_n, k_rope = mla_q(seq(cq), seq(ckv), seq(kr), tabs, wl, cfg["tm_q"])
        sk3, sv3 = seq(sk), seq(sv)
        if pasts is None:
            ckv_all, kr_all, k_all, v_all = ckv_n, k_rope, seq(sk_b), seq(sv_b)
        else:
            _, _, ckv0, kr0, k0, v0 = pasts[l]
            ckv_all = jnp.concatenate([ckv0, ckv_n], axis=1)
            kr_all = jnp.concatenate([kr0, k_rope], axis=1)
            k_all = jnp.concatenate([k0.reshape(B, past, WIDTH).astype(BF16), seq(sk_b)], axis=1)
            v_all = jnp.concatenate([v0.reshape(B, past, WIDTH).astype(BF16), seq(sv_b)], axis=1)
        ckv_all, kr_all, k_all, v_all = (_pad_rows(a, s_pad) for a in (ckv_all, kr_all, k_all, v_all))
        kf, vf = mla_kv(ckv_all, kr_all, wl, cfg["ts"])
        yB = mla_attn(q, kf, vf, past, S, cfg["tq"], cfg["tk"], cfg["tkd"])
        yC = sb_attn(seq(sq), k_all, v_all, past, cfg["sb_tq"], cfg["sb_tk"])
        x2 = merge_out(x2, gate, yA.reshape(B * T, WIDTH), zA, yB.reshape(B * T, WIDTH), zB,
                       yC.reshape(B * T, WIDTH), zC, gates, wl, T, cfg["tm"])
        new.append((_bd_to_state(s_fin), pr3[:, -1:], ckv_n, k_rope,
                    sk3.reshape(B, T, HEADS, HEAD_DIM), sv3.reshape(B, T, HEADS, HEAD_DIM)))
    stacked = [jnp.stack([st[i] for st in new], axis=0) for i in range(6)]
    return x2.reshape(B, T, D), stacked


PROMPT_CFG = dict(past=0, tm=512, rwkv_n=128, tm_q=512, ts=512, tq=512, tk=1024, tkd=512, sb_tq=512, sb_tk=256,
                  kv_mult=1024)
SAMPLE_CFG = dict(past=2048, tm=16, rwkv_n=16, tm_q=16, ts=2560, tq=16, tk=2560, tkd=2560, sb_tq=16, sb_tk=512,
                  kv_mult=512)


def kernel(x_prompt, x_sample, state_rwkv_wkv, state_rwkv_shift, cache_mla_ckv, cache_mla_krope, cache_sb_k, cache_sb_v, c_prompt, c_sample, w_ada, b_ada, norm_g, w_in, rwkv_mu, rwkv_w0, rwkv_w_up, rwkv_a0, rwkv_a_up, rwkv_k_k, rwkv_k_a, rwkv_r_k, rwkv_gn_g, rwkv_gn_b, w_br_rwkv, mla_q_norm, mla_w_uq, mla_kv_norm, mla_w_ukv, mla_qn_nope, mla_qn_rope, mla_kn_nope, mla_kn_rope, w_br_mla, w_br_sb, w_out):
    P = dict(w_in=w_in, norm_g=norm_g, rwkv_mu=rwkv_mu, rwkv_w0=rwkv_w0, rwkv_w_up=rwkv_w_up, rwkv_a0=rwkv_a0,
             rwkv_a_up=rwkv_a_up, rwkv_k_k=rwkv_k_k, rwkv_k_a=rwkv_k_a, rwkv_r_k=rwkv_r_k, rwkv_gn_g=rwkv_gn_g,
             rwkv_gn_b=rwkv_gn_b, w_br_rwkv=w_br_rwkv, mla_q_norm=mla_q_norm, mla_w_uq=mla_w_uq,
             mla_kv_norm=mla_kv_norm, mla_w_ukv=mla_w_ukv, mla_qn_nope=mla_qn_nope, mla_qn_rope=mla_qn_rope,
             mla_kn_nope=mla_kn_nope, mla_kn_rope=mla_kn_rope, w_br_mla=w_br_mla, w_br_sb=w_br_sb, w_out=w_out)
    depth = w_in.shape[0]
    bp = x_prompt.shape[0]
    layers = [_prep_layer(P, l) for l in range(depth)]
    offs = np.cumsum([0] + [wd for _, wd in IN_GROUPS])
    src = {name: (offs[i], offs[i + 1]) for i, (name, _) in enumerate(IN_GROUPS)}
    w_in_all = jnp.concatenate([w_in[:, :, src[name][0]:src[name][1]] for name, _ in _in_weight_layout()[0]],
                               axis=-1).astype(BF16)
    mods = ada_mod(jnp.concatenate([c_prompt, c_sample], axis=0), w_ada, b_ada)
    y_p, st_p = _trunk(x_prompt, mods[:, :bp], None, layers, w_in_all, PROMPT_CFG)
    pasts = [(state_rwkv_wkv[l], state_rwkv_shift[l], cache_mla_ckv[l], cache_mla_krope[l],
              cache_sb_k[l], cache_sb_v[l]) for l in range(depth)]
    assert cache_mla_ckv.shape[2] == SAMPLE_CFG["past"]
    y_s, st_s = _trunk(x_sample, mods[:, bp:], pasts, layers, w_in_all, SAMPLE_CFG)
    return (y_p, y_s, *st_p, *st_s)
```

```python
import functools

import numpy as np
import jax
import jax.numpy as jnp
from jax import lax
from jax.experimental import pallas as pl
from jax.experimental.pallas import tpu as pltpu

F32 = jnp.float32
BF16 = jnp.bfloat16

D_MODEL = 1024
CHUNK = 64
RMS_EPS = 1e-6
GN_EPS = 64e-5
NEG_BIG = -1e30
ROPE_THETA = 10000.0
LOG2E = 1.4426950408889634

HEADS = 8
HEAD_DIM = 64
WIDTH = HEADS * HEAD_DIM
PAIRS = HEADS // 2
LANES = 128
HEAD_SHIFT = HEAD_DIM.bit_length() - 1
CHUNK_SHIFT = CHUNK.bit_length() - 1
LORA = 64
SHIFT_W = 3 * WIDTH + 2 * LORA
Q_LORA = 384
KV_LORA = 256
ROPE = 32
NOPE = 64
QK = NOPE + ROPE
RWKV_CHUNK = 64

IN_GROUPS = (("pr", SHIFT_W), ("zA", WIDTH), ("cq", Q_LORA), ("ckv", KV_LORA), ("kr", ROPE),
             ("zB", WIDTH), ("sq", WIDTH), ("sk", WIDTH), ("sv", WIDTH), ("zC", WIDTH),
             ("gates", 3 * D_MODEL))
BF16_COPIES = ("sk", "sv")
BF16_GROUPS = ("zA", "zB", "zC", "gates")

VMEM_LIMIT = 56 * 1024 * 1024


def _params(sem):
    return pltpu.CompilerParams(dimension_semantics=sem, vmem_limit_bytes=VMEM_LIMIT)


def _bdot(a, b):
    return jnp.dot(a.astype(BF16), b.astype(BF16), preferred_element_type=F32)


def _bdot_nt(a, b):
    return lax.dot_general(a.astype(BF16), b.astype(BF16), (((1,), (1,)), ((), ())),
                           preferred_element_type=F32)


def _bdot_tn(a, b):
    return lax.dot_general(a.astype(BF16), b.astype(BF16), (((0,), (0,)), ((), ())),
                           preferred_element_type=F32)


def _split(x, n):
    parts, r = [], x
    for _ in range(n):
        p = r.astype(BF16)
        parts.append(p)
        r = r - p.astype(F32)
    return parts


def _xdot_r(x, m01, n):
    m = m01.astype(BF16)
    return sum(jnp.dot(p, m, preferred_element_type=F32) for p in _split(x, n))


def _sigmoid(x):
    return 1.0 / (1.0 + jnp.exp(-x))


def _silu(x):
    return x * _sigmoid(x)


def _softplus(x):
    return jnp.maximum(x, 0.0) + jnp.log(1.0 + jnp.exp(-jnp.abs(x)))


def _rms(x, g):
    return x * lax.rsqrt(jnp.mean(x * x, axis=-1, keepdims=True) + RMS_EPS) * g


def _iota2(shape, dim):
    return lax.broadcasted_iota(jnp.int32, shape, dim)


def _ada_kernel(c_ref, w_ref, b_ref, o_ref):
    o_ref[0] = _bdot(_silu(c_ref[...]), w_ref[0]) + b_ref[0]


def ada_mod(c, w_ada, b_ada):
    L, D, N = w_ada.shape
    rows = c.shape[0]
    tn = 1024
    return pl.pallas_call(
        _ada_kernel,
        grid=(L, N // tn),
        in_specs=[pl.BlockSpec((rows, D), lambda l, j: (0, 0)),
                  pl.BlockSpec((1, D, tn), lambda l, j: (l, 0, j)),
                  pl.BlockSpec((1, 1, tn), lambda l, j: (l, 0, j))],
        out_specs=pl.BlockSpec((1, rows, tn), lambda l, j: (l, 0, j)),
        out_shape=jax.ShapeDtypeStruct((L, rows, N), F32),
        compiler_params=_params(("parallel", "parallel")),
        name="ada_mod",
    )(c, w_ada, b_ada.reshape(L, 1, N))


def _in_weight_layout():
    order = [g for g in IN_GROUPS if g[0] != "kr"] + [g for g in IN_GROUPS if g[0] == "kr"]
    offs, pos = {}, 0
    for name, wd in order:
        offs[name] = pos
        pos += wd
    return order, offs


def _inproj_kernel(x_ref, sc_ref, sh_ref, g_ref, w_ref, *outs):
    n = len(IN_GROUPS)
    offs = _in_weight_layout()[1]
    h = _rms(x_ref[...], g_ref[...]) * (1.0 + sc_ref[0]) + sh_ref[0]
    hb = h.astype(BF16)
    extra = iter(outs[n:])
    for (name, wd), o in zip(IN_GROUPS, outs):
        res = jnp.dot(hb, w_ref[0, :, offs[name]:offs[name] + wd], preferred_element_type=F32)
        o[...] = res.astype(o.dtype)
        if name in BF16_COPIES:
            next(extra)[...] = res.astype(BF16)


def in_proj(x2, scale, shift, g, w_all, layer, T, tm):
    M, D = x2.shape
    assert T % tm == 0 and M % tm == 0
    per = T // tm
    row = lambda i: (i, 0)
    mod = lambda i: (i // per, 0, 0)
    const = lambda i: (0, 0)
    in_specs = [pl.BlockSpec((tm, D), row), pl.BlockSpec((1, 1, D), mod), pl.BlockSpec((1, 1, D), mod),
                pl.BlockSpec((1, D), const),
                pl.BlockSpec((1, D, w_all.shape[2]), lambda i: (layer, 0, 0), pipeline_mode=pl.Buffered(1))]
    out_specs = [pl.BlockSpec((tm, wd), row) for _, wd in IN_GROUPS]
    out_shape = [jax.ShapeDtypeStruct((M, wd), BF16 if name in BF16_GROUPS else F32) for name, wd in IN_GROUPS]
    out_specs += [pl.BlockSpec((tm, WIDTH), row) for _ in BF16_COPIES]
    out_shape += [jax.ShapeDtypeStruct((M, WIDTH), BF16) for _ in BF16_COPIES]
    return pl.pallas_call(
        _inproj_kernel, grid=(M // tm,), in_specs=in_specs, out_specs=out_specs, out_shape=out_shape,
        compiler_params=_params(("parallel",)), name="in_proj",
    )(x2, scale, shift, g, w_all)


def _rwkv_kernel(n, pr_ref, past_ref, s0_ref, mu_ref, w0_ref, a0_ref, kk_ref, ka_ref, rk_ref,
                 gng_ref, gnb_ref, wup_ref, aup_ref, y_ref, sfin_ref, st_scr, last_scr):
    tb = pl.program_id(1)

    @pl.when(tb == 0)
    def _():
        st_scr[...] = s0_ref[0]
        last_scr[...] = past_ref[0]

    p = pr_ref[0]
    row = _iota2((n, 1), 0)
    prev = jnp.where(row == 0, last_scr[...], pltpu.roll(p, 1, axis=0))
    last_scr[...] = p[n - 1:n, :]
    xs = p + (prev - p) * mu_ref[...]
    r, k, v = xs[:, :WIDTH], xs[:, WIDTH:2 * WIDTH], xs[:, 2 * WIDTH:3 * WIDTH]
    wa = xs[:, 3 * WIDTH:]
    w_log = -_softplus(-(w0_ref[...] + _bdot(jnp.tanh(wa), wup_ref[...]))) - 0.5
    lw = -jnp.exp(w_log)
    a = _sigmoid(a0_ref[...] + _bdot(wa, aup_ref[...]))

    ri, ci = _iota2((LANES, LANES), 0), _iota2((LANES, LANES), 1)
    same_head = (ri >> HEAD_SHIFT) == (ci >> HEAD_SHIFT)
    seg64 = same_head.astype(BF16)
    bd2 = same_head.astype(F32)

    def segsum(t):
        return jnp.concatenate(
            [_xdot_r(t[:, LANES * q:LANES * (q + 1)], seg64, 2) for q in range(PAIRS)], axis=-1)

    kkr = k * kk_ref[...]
    kk = kkr * jnp.minimum(lax.rsqrt(segsum(kkr * kkr)), 1e12)
    k2 = k * (1.0 + (a - 1.0) * ka_ref[...])
    bonus = segsum(r * k2 * rk_ref[...]) * v

    chunk = min(RWKV_CHUNK, n)
    shift = chunk.bit_length() - 1
    tr, tc = _iota2((n, n), 0), _iota2((n, n), 1)
    same = (tr >> shift) == (tc >> shift)
    strict = (same & (tr > tc)).astype(F32)
    incl = (same & (tr >= tc)).astype(F32)
    eye = (tr == tc).astype(F32)
    lw_parts = _split(lw, 2)
    cum = lambda m01: sum(jnp.dot(m01.astype(BF16), part, preferred_element_type=F32) for part in lw_parts)
    g_log = cum(incl)
    g_end = cum(same)
    e_in, e_inv, e_rest = jnp.exp(g_log), jnp.exp(-g_log), jnp.exp(g_end - g_log)
    r_t = r * e_in
    kk_t = kk * jnp.exp(g_log - lw)
    kka = kk * a
    k_t, b_t = k2 * e_inv, kka * e_inv
    k_e, b_e = k2 * e_rest, kka * e_rest
    dec = jnp.exp(g_end)

    lane = _iota2((1, LANES), 1)
    head_mask = [(lane < HEAD_DIM).astype(F32), (lane >= HEAD_DIM).astype(F32)]

    sls = [slice(LANES * q, LANES * (q + 1)) for q in range(PAIRS)]
    units = [(q, hm) for q in range(PAIRS) for hm in head_mask]
    rp, kkp, vp = [r_t[:, s] for s in sls], [kk_t[:, s] for s in sls], [v[:, s] for s in sls]
    kb, bb = [k_t[:, s].astype(BF16) for s in sls], [b_t[:, s].astype(BF16) for s in sls]
    vb, kkb = [x.astype(BF16) for x in vp], [x.astype(BF16) for x in kkp]
    kkm = [(kkp[q] * hm).astype(BF16) for q, hm in units]
    rm = [(rp[q] * hm).astype(BF16) for q, hm in units]
    l_b = [_bdot_nt(kkm[u], bb[q]) * strict for u, (q, _) in enumerate(units)]
    l_k = [_bdot_nt(kkm[u], kb[q]) * strict for u, (q, _) in enumerate(units)]
    r_k = [_bdot_nt(rm[u], kb[q]) * incl for u, (q, _) in enumerate(units)]
    r_b = [_bdot_nt(rm[u], bb[q]) * incl for u, (q, _) in enumerate(units)]
    sub = min(16, chunk)
    sub_shift = sub.bit_length() - 1
    inner = ((tr >> sub_shift) == (tc >> sub_shift)).astype(F32)
    pw = [x * inner for x in l_b]
    t_inv = [eye - x for x in pw]
    for _ in range(sub_shift - 1):
        pw = [_bdot(x, x) for x in pw]
        t_inv = [t + _bdot(t, x) for t, x in zip(t_inv, pw)]
    for level in range(sub_shift, shift):
        joined = (((tr >> (level + 1)) == (tc >> (level + 1))) & ((tr >> level) != (tc >> level))).astype(F32)
        t_inv = [t - _bdot(_bdot(t, x * joined), t) for t, x in zip(t_inv, l_b)]
    lkv = [_bdot(l_k[u], vb[q]) for u, (q, _) in enumerate(units)]
    w_h = [_bdot(t_inv[u], kkb[q]) for u, (q, _) in enumerate(units)]
    u0_h = [_bdot(t, x) for t, x in zip(t_inv, lkv)]
    q_h = [rp[q] - _bdot(r_b[u], w_h[u]) for u, (q, _) in enumerate(units)]
    y0_h = [_bdot(r_k[u], vb[q]) - _bdot(r_b[u], u0_h[u]) for u, (q, _) in enumerate(units)]
    first = head_mask[0] > 0.5
    pair = lambda xs: [jnp.where(first, xs[2 * q], xs[2 * q + 1]) for q in range(PAIRS)]
    w_c, u0_c, q_c, y0_c = pair(w_h), pair(u0_h), pair(q_h), pair(y0_h)

    st = [st_scr[q] for q in range(PAIRS)]
    ys = [[] for _ in range(PAIRS)]
    rows = [slice(chunk * c, chunk * (c + 1)) for c in range(n // chunk)]
    n_c = [[(_bdot_tn(vp[q][rs], k_e[rs, sls[q]]) - _bdot_tn(u0_c[q][rs], b_e[rs, sls[q]])) * bd2
            for q in range(PAIRS)] for rs in rows]
    a_c = [[_bdot_tn(w_c[q][rs], b_e[rs, sls[q]]) * bd2 for q in range(PAIRS)] for rs in rows]
    for c, rs in enumerate(rows):
        for q in range(PAIRS):
            ys[q].append(_bdot_nt(q_c[q][rs], st[q]) + y0_c[q][rs])
        sa = [_bdot(st[q], a_c[c][q]) for q in range(PAIRS)]
        st = [st[q] * dec[chunk * c:chunk * c + 1, sls[q]] + (n_c[c][q] - sa[q]) for q in range(PAIRS)]
    for q in range(PAIRS):
        st_scr[q] = st[q]
    y = [x[0] if len(x) == 1 else jnp.concatenate(x, axis=0) for x in ys]
    mean = [_xdot_r(x, seg64, 2) * (1.0 / HEAD_DIM) for x in y]
    d = [x - m for x, m in zip(y, mean)]
    var = [_xdot_r(x * x, seg64, 2) * (1.0 / HEAD_DIM) for x in d]
    for q, s in enumerate(sls):
        yn = d[q] * lax.rsqrt(var[q] + GN_EPS) * gng_ref[:, s] + gnb_ref[:, s] + bonus[:, s]
        y_ref[0, :, s] = yn.astype(y_ref.dtype)

    @pl.when(tb == pl.num_programs(1) - 1)
    def _():
        sfin_ref[0] = st_scr[...]


def rwkv_branch(pr, past_shift, s0_bd, wl, n):
    B, T, _ = pr.shape
    assert T % n == 0 and n % min(RWKV_CHUNK, n) == 0 and n & (n - 1) == 0
    vec = lambda wd: pl.BlockSpec((1, wd), lambda b, t: (0, 0))
    in_specs = [pl.BlockSpec((1, n, SHIFT_W), lambda b, t: (b, t, 0)),
                pl.BlockSpec((1, 1, SHIFT_W), lambda b, t: (b, 0, 0)),
                pl.BlockSpec((1, PAIRS, LANES, LANES), lambda b, t: (b, 0, 0, 0)),
                vec(SHIFT_W)] + [vec(WIDTH)] * 7 + [
                pl.BlockSpec((LANES, WIDTH), lambda b, t: (0, 0)),
                pl.BlockSpec((LANES, WIDTH), lambda b, t: (0, 0))]
    return pl.pallas_call(
        functools.partial(_rwkv_kernel, n),
        grid=(B, T // n),
        in_specs=in_specs,
        out_specs=[pl.BlockSpec((1, n, WIDTH), lambda b, t: (b, t, 0)),
                   pl.BlockSpec((1, PAIRS, LANES, LANES), lambda b, t: (b, 0, 0, 0))],
        out_shape=[jax.ShapeDtypeStruct((B, T, WIDTH), BF16),
                   jax.ShapeDtypeStruct((B, PAIRS, LANES, LANES), F32)],
        scratch_shapes=[pltpu.VMEM((PAIRS, LANES, LANES), F32), pltpu.VMEM((1, SHIFT_W), F32)],
        compiler_params=_params(("parallel", "arbitrary")),
        name="rwkv7",
    )(pr, past_shift, s0_bd, wl["mu"], wl["w0"], wl["a0"], wl["k_k"], wl["k_a"], wl["r_k"],
      wl["gn_g"], wl["gn_b"], wl["w_up"], wl["a_up"])


def _state_to_bd(s):
    B = s.shape[0]
    s = s.reshape(B, PAIRS, 2, HEAD_DIM, HEAD_DIM)
    z = jnp.zeros_like(s[:, :, 0])
    top = jnp.concatenate([s[:, :, 0], z], axis=-1)
    bot = jnp.concatenate([z, s[:, :, 1]], axis=-1)
    return jnp.concatenate([top, bot], axis=-2)


def _bd_to_state(s):
    B = s.shape[0]
    h0 = s[:, :, :HEAD_DIM, :HEAD_DIM]
    h1 = s[:, :, HEAD_DIM:, HEAD_DIM:]
    return jnp.stack([h0, h1], axis=2).reshape(B, HEADS, HEAD_DIM, HEAD_DIM)


def _mla_q_kernel(cq_ref, ckv_ref, kr_ref, c96_ref, s96_ref, c32_ref, s32_ref, gq_ref, wuq_ref, wrot_ref, gqn_ref,
                  grot_ref, gkv_ref, gkr_ref, p32_ref, q_ref, ckvn_ref, krope_ref):
    cqb = _rms(cq_ref[0], gq_ref[...]).astype(BF16)
    nope = _iota2((1, QK), 1) < NOPE
    gc = gqn_ref[...] * c96_ref[...]
    gs = grot_ref[...] * s96_ref[...]
    for h in range(HEADS):
        qh = jnp.dot(cqb, wuq_ref[h], preferred_element_type=F32)
        qr = jnp.dot(cqb, wrot_ref[h], preferred_element_type=F32)
        sq = qh * qh
        ss_n = jnp.sum(jnp.where(nope, sq, 0.0), axis=-1, keepdims=True)
        ss_r = jnp.sum(jnp.where(nope, 0.0, sq), axis=-1, keepdims=True)
        inv = jnp.where(nope, lax.rsqrt(ss_n * (1.0 / NOPE) + RMS_EPS), lax.rsqrt(ss_r * (1.0 / ROPE) + RMS_EPS))
        qo = inv * (qh * gc + qr * gs)
        q_ref[0, h] = (qo * (QK ** -0.5 * LOG2E)).astype(BF16)
    ckvn_ref[0] = _rms(ckv_ref[0], gkv_ref[...])
    krn = _rms(kr_ref[0], gkr_ref[...])
    krope_ref[0] = krn * c32_ref[...] + _xdot_r(krn, p32_ref[...], 3) * s32_ref[...]


def mla_q(cq, ckv, kr, tabs, wl, tm):
    B, T, _ = cq.shape
    c96, s96, c32, s32 = tabs
    tok = lambda wd: pl.BlockSpec((1, tm, wd), lambda b, t: (b, t, 0))
    tab = lambda wd: pl.BlockSpec((tm, wd), lambda b, t: (t, 0))
    vec = lambda wd: pl.BlockSpec((1, wd), lambda b, t: (0, 0))
    return pl.pallas_call(
        _mla_q_kernel,
        grid=(B, T // tm),
        in_specs=[tok(Q_LORA), tok(KV_LORA), tok(ROPE), tab(QK), tab(QK), tab(ROPE), tab(ROPE),
                  vec(Q_LORA), pl.BlockSpec((HEADS, Q_LORA, QK), lambda b, t: (0, 0, 0)),
                  pl.BlockSpec((HEADS, Q_LORA, QK), lambda b, t: (0, 0, 0)), vec(QK), vec(QK),
                  vec(KV_LORA), vec(ROPE), pl.BlockSpec((ROPE, ROPE), lambda b, t: (0, 0))],
        out_specs=[pl.BlockSpec((1, HEADS, tm, QK), lambda b, t: (b, 0, t, 0)), tok(KV_LORA), tok(ROPE)],
        out_shape=[jax.ShapeDtypeStruct((B, HEADS, T, QK), BF16),
                   jax.ShapeDtypeStruct((B, T, KV_LORA), F32),
                   jax.ShapeDtypeStruct((B, T, ROPE), F32)],
        compiler_params=_params(("parallel", "parallel")),
        name="mla_q",
    )(cq, ckv, kr, c96, s96, c32, s32, wl["q_norm"], wl["w_uq"], wl["w_uq_rot"], wl["qn"], wl["qn_rot"],
      wl["kv_norm"], wl["kn_rope"], wl["p32"])


def _mla_kv_kernel(ckv_ref, kr_ref, wuk_ref, wuv_ref, gkn_ref, e_ref, k_ref, v_ref):
    cb = ckv_ref[0].astype(BF16)
    k_rope = _xdot_r(kr_ref[0], e_ref[...], 3)
    for h in range(HEADS):
        kh = jnp.dot(cb, wuk_ref[h], preferred_element_type=F32)
        ms = jnp.sum(kh * kh, axis=-1, keepdims=True) * (1.0 / NOPE)
        k_ref[0, h] = (kh * lax.rsqrt(ms + RMS_EPS) * gkn_ref[...] + k_rope).astype(BF16)
    for q in range(PAIRS):
        v_ref[0, q] = jnp.dot(cb, wuv_ref[q], preferred_element_type=F32).astype(BF16)


def mla_kv(ckv_all, kr_all, wl, ts):
    B, S, _ = ckv_all.shape
    return pl.pallas_call(
        _mla_kv_kernel,
        grid=(B, S // ts),
        in_specs=[pl.BlockSpec((1, ts, KV_LORA), lambda b, t: (b, t, 0)),
                  pl.BlockSpec((1, ts, ROPE), lambda b, t: (b, t, 0)),
                  pl.BlockSpec((HEADS, KV_LORA, QK), lambda b, t: (0, 0, 0)),
                  pl.BlockSpec((PAIRS, KV_LORA, LANES), lambda b, t: (0, 0, 0)),
                  pl.BlockSpec((1, QK), lambda b, t: (0, 0)),
                  pl.BlockSpec((ROPE, QK), lambda b, t: (0, 0))],
        out_specs=[pl.BlockSpec((1, HEADS, ts, QK), lambda b, t: (b, 0, t, 0)),
                   pl.BlockSpec((1, PAIRS, ts, LANES), lambda b, t: (b, 0, t, 0))],
        out_shape=[jax.ShapeDtypeStruct((B, HEADS, S, QK), BF16),
                   jax.ShapeDtypeStruct((B, PAIRS, S, LANES), BF16)],
        compiler_params=_params(("parallel", "parallel")),
        name="mla_kv",
    )(ckv_all, kr_all, wl["w_uk"], wl["w_uv"], wl["kn_nope"], wl["e96"])


def _mla_attn_kernel(tq, tk, tkd, past, s_len, q_ref, k_ref, v_ref, o_ref, m_scr, l_scr, acc_scr):
    q0 = past + pl.program_id(2) * tq
    first = _iota2((1, LANES), 1) < HEAD_DIM
    m_scr[...] = jnp.full(m_scr.shape, NEG_BIG, F32)
    l_scr[...] = jnp.zeros(l_scr.shape, F32)
    acc_scr[...] = jnp.zeros(acc_scr.shape, F32)
    qs = [q_ref[0, h] for h in range(2)]

    def block(start, width, masked):
        reps = width // LANES
        lane_chunks = lambda x: [x[:, LANES * c:LANES * (c + 1)] for c in range(reps)]
        ks = pl.multiple_of(start, width)
        s = [_bdot_nt(qs[h], k_ref[0, h, pl.ds(ks, width), :]) for h in range(2)]
        if masked:
            limit = jnp.minimum((((q0 + _iota2((tq, 1), 0)) >> CHUNK_SHIFT) + 1) << CHUNK_SHIFT, s_len)
            visible = (ks + _iota2((1, width), 1)) < limit
            s = [jnp.where(visible, x, NEG_BIG) for x in s]
        vb = v_ref[0, 0, pl.ds(ks, width), :]
        m_old = [m_scr[h] for h in range(2)]
        m_blk = [functools.reduce(jnp.maximum, lane_chunks(x)) for x in s]
        m_new = [jnp.maximum(m_old[h], jnp.max(m_blk[h], axis=-1, keepdims=True)) for h in range(2)]
        alpha = [jnp.exp2(m_old[h] - m_new[h]) for h in range(2)]
        p = [jnp.exp2(s[h] - jnp.concatenate([m_new[h]] * reps, axis=1)) for h in range(2)]
        for h in range(2):
            l_scr[h] = functools.reduce(jnp.add, lane_chunks(p[h]), alpha[h] * l_scr[h])
            m_scr[h] = m_new[h]
        pv = _bdot(jnp.concatenate(p, axis=0), vb)
        acc_scr[...] = acc_scr[...] * jnp.where(first, alpha[0], alpha[1]) + jnp.where(first, pv[:tq], pv[tq:])

    open_end = jnp.minimum((q0 // CHUNK + 1) * CHUNK, s_len)
    seen = jnp.minimum(((q0 + tq - 1) // CHUNK + 1) * CHUNK, s_len)
    n_full = open_end // tk
    n_open = open_end // tkd - n_full * (tk // tkd)
    diag0 = (open_end // tkd) * tkd
    n_diag = (seen - diag0 + tkd - 1) // tkd

    def full_body(kb, carry):
        block(kb * tk, tk, False)
        return carry

    def open_body(i, carry):
        block(n_full * tk + i * tkd, tkd, False)
        return carry

    def diag_body(i, carry):
        block(diag0 + i * tkd, tkd, True)
        return carry

    lax.fori_loop(0, n_full, full_body, 0)
    lax.fori_loop(0, n_open, open_body, 0)
    lax.fori_loop(0, n_diag, diag_body, 0)
    l = [jnp.sum(l_scr[h], axis=-1, keepdims=True) for h in range(2)]
    o_ref[0] = (acc_scr[...] / jnp.where(first, l[0], l[1])).astype(o_ref.dtype)


def mla_attn(q, k, v, past, s_len, tq, tk, tkd):
    B, _, T, _ = q.shape
    S = k.shape[2]
    assert S % tk == 0 and tk % tkd == 0 and tkd % LANES == 0 and T % tq == 0
    return pl.pallas_call(
        functools.partial(_mla_attn_kernel, tq, tk, tkd, past, s_len),
        grid=(B, PAIRS, T // tq),
        in_specs=[pl.BlockSpec((1, 2, tq, QK), lambda b, p, i: (b, p, i, 0)),
                  pl.BlockSpec((1, 2, S, QK), lambda b, p, i: (b, p, 0, 0)),
                  pl.BlockSpec((1, 1, S, LANES), lambda b, p, i: (b, p, 0, 0))],
        out_specs=pl.BlockSpec((1, tq, LANES), lambda b, p, i: (b, i, p)),
        out_shape=jax.ShapeDtypeStruct((B, T, WIDTH), BF16),
        scratch_shapes=[pltpu.VMEM((2, tq, LANES), F32), pltpu.VMEM((2, tq, LANES), F32),
                        pltpu.VMEM((tq, LANES), F32)],
        compiler_params=_params(("parallel", "parallel", "arbitrary")),
        name="mla_attn",
    )(q, k, v)


def _sb_kernel(tq, tk, past, q_ref, k_ref, v_ref, sum_ref, o_ref, carry_scr, acc_scr, qm_scr, z_scr):
    nk = k_ref.shape[1] // tk
    q0 = past + pl.program_id(2) * tq
    first = _iota2((1, LANES), 1) < HEAD_DIM
    carry_scr[...] = jnp.zeros(carry_scr.shape, F32)
    acc_scr[...] = jnp.zeros(acc_scr.shape, F32)
    qv = q_ref[0] * (HEAD_DIM ** -0.5 * LOG2E)
    qm_scr[0] = jnp.where(first, qv, 0.0).astype(BF16)
    qm_scr[1] = jnp.where(first, 0.0, qv).astype(BF16)

    def key_mask(kb, r0):
        return (kb * tk + _iota2((1, tk), 1)) < (q0 + r0 + _iota2((tq - r0, 1), 0))

    def stage_z(kb, r0=0):
        kblk = k_ref[0, pl.ds(pl.multiple_of(kb * tk, tk), tk), :]
        n = tq - r0
        z = _bdot_nt(qm_scr[:, r0:, :].reshape(2 * n, LANES), kblk)
        return [z[:n], z[n:]]

    def stage_w(z, mask):
        split = []
        for read in z:
            x = read()
            neg_abs = lax.bitcast_convert_type(lax.bitcast_convert_type(x, jnp.uint32) | jnp.uint32(0x80000000), F32)
            w = jnp.maximum(x, 0.0) + jnp.log2(1.0 + jnp.exp2(neg_abs))
            if mask is not None:
                w = jnp.where(mask, w, 0.0)
            hi = w.astype(BF16)
            split.append(jnp.concatenate([hi, (w - hi.astype(F32)).astype(BF16)], axis=1))
        return split

    def stage_incl(split):
        both = jnp.dot(jnp.concatenate(split, axis=0), sum_ref[...], preferred_element_type=F32)
        n = split[0].shape[0]
        incl = [both[:n], both[n:]]
        return incl, [x[:, 0:1] for x in incl]

    def stage_a(z, incl, c, mask):
        cb = [jnp.concatenate([x] * (tk // LANES), axis=1) for x in c]
        a = [jnp.exp2(z[h]() - incl[h] - cb[h]) for h in range(2)]
        if mask is not None:
            a = [jnp.where(mask, x, 0.0) for x in a]
        return a

    def stage_pv(a, kb):
        vblk = v_ref[0, pl.ds(pl.multiple_of(kb * tk, tk), tk), :]
        pv = _bdot(jnp.concatenate(a, axis=0), vblk)
        n = a[0].shape[0]
        return jnp.where(first, pv[:n], pv[n:])

    def one_block(kb, masked, r0=0):
        mask = key_mask(kb, r0) if masked else None
        z = [lambda v=v: v for v in stage_z(kb, r0)]
        incl, tot = stage_incl(stage_w(z, mask))
        c = [carry_scr[h, r0:, :] for h in range(2)]
        a = stage_a(z, incl, c, mask)
        acc_scr[r0:, :] += stage_pv(a, kb)
        for h in range(2):
            carry_scr[h, r0:, :] = c[h] + tot[h]

    def prefetch_z(kb, slot):
        for j in range(2):
            z = stage_z(jnp.maximum(kb - j, 0))
            for h in range(2):
                z_scr[slot, j, h] = z[h]

    def two_blocks(kb, slot):
        z1, z2 = ([lambda j=j, h=h: z_scr[slot, j, h] for h in range(2)] for j in range(2))
        incl1, tot1 = stage_incl(stage_w(z1, None))
        prefetch_z(kb - 2, 1 - slot)
        incl2, tot2 = stage_incl(stage_w(z2, None))
        c1 = [carry_scr[h] for h in range(2)]
        pv1 = stage_pv(stage_a(z1, incl1, c1, None), kb)
        c2 = [c1[h] + tot1[h] for h in range(2)]
        pv2 = stage_pv(stage_a(z2, incl2, c2, None), kb - 1)
        acc_scr[...] += pv1 + pv2
        for h in range(2):
            carry_scr[h] = c2[h] + tot2[h]

    n_full = q0 // tk
    last = jnp.clip((q0 + tq - 2) // tk, 0, nk - 1)

    def loop(count, fn):
        def body(i, carry):
            fn(i)
            return carry
        lax.fori_loop(0, count, body, 0)

    if past % tk == 0 and tq % tk == 0:
        for j in reversed(range(tq // tk)):
            one_block(n_full + j, True, j * tk)
    else:
        loop(last - n_full + 1, lambda i: one_block(last - i, True))

    def four_blocks(i):
        two_blocks(n_full - 1 - 4 * i, 0)
        two_blocks(n_full - 3 - 4 * i, 1)

    rest = n_full % 4
    prefetch_z(n_full - 1, 0)
    loop(n_full // 4, four_blocks)
    loop(rest // 2, lambda i: two_blocks(rest - 1, 0))
    loop(rest % 2, lambda i: one_block(0, False))
    o_ref[0] = acc_scr[...].astype(o_ref.dtype)


def sb_attn(q, k_all, v_all, past, tq, tk):
    B, T, _ = q.shape
    S = k_all.shape[1]
    assert S % tk == 0 and T % tq == 0 and tk % LANES == 0
    incl = np.arange(tk)[:, None] >= np.arange(tk)[None, :]
    sum_mat = jnp.asarray(np.concatenate([incl, incl], axis=0), BF16)
    return pl.pallas_call(
        functools.partial(_sb_kernel, tq, tk, past),
        grid=(B, PAIRS, T // tq),
        in_specs=[pl.BlockSpec((1, tq, LANES), lambda b, p, i: (b, i, p)),
                  pl.BlockSpec((1, S, LANES), lambda b, p, i: (b, 0, p)),
                  pl.BlockSpec((1, S, LANES), lambda b, p, i: (b, 0, p)),
                  pl.BlockSpec((2 * tk, tk), lambda b, p, i: (0, 0))],
        out_specs=pl.BlockSpec((1, tq, LANES), lambda b, p, i: (b, i, p)),
        out_shape=jax.ShapeDtypeStruct((B, T, WIDTH), BF16),
        scratch_shapes=[pltpu.VMEM((2, tq, LANES), F32), pltpu.VMEM((tq, LANES), F32),
                        pltpu.VMEM((2, tq, LANES), BF16), pltpu.VMEM((2, 2, 2, tq, tk), F32)],
        compiler_params=_params(("parallel", "parallel", "arbitrary")),
        name="sb_attn",
    )(q, k_all, v_all, sum_mat)


def _merge_kernel(x_ref, gm_ref, ya_ref, za_ref, yb_ref, zb_ref, yc_ref, zc_ref, g_ref,
                  wa_ref, wb_ref, wc_ref, wo_ref, o_ref):
    def branch(y_ref, z_ref, w_ref):
        return _bdot(y_ref[...].astype(F32) * _silu(z_ref[...].astype(F32)), w_ref[...])

    sg = _sigmoid(g_ref[...].astype(F32))
    merged = (sg[:, :D_MODEL] * branch(ya_ref, za_ref, wa_ref)
              + sg[:, D_MODEL:2 * D_MODEL] * branch(yb_ref, zb_ref, wb_ref)
              + sg[:, 2 * D_MODEL:] * branch(yc_ref, zc_ref, wc_ref))
    o_ref[...] = x_ref[...] + gm_ref[0] * _bdot(merged, wo_ref[...])


def merge_out(x2, gate_mod, ya, za, yb, zb, yc, zc, gates, wl, T, tm):
    M, D = x2.shape
    per = T // tm
    row = lambda wd: pl.BlockSpec((tm, wd), lambda i: (i, 0))
    const = lambda r, c: pl.BlockSpec((r, c), lambda i: (0, 0))
    return pl.pallas_call(
        _merge_kernel,
        grid=(M // tm,),
        in_specs=[row(D), pl.BlockSpec((1, 1, D), lambda i: (i // per, 0, 0))] + [row(WIDTH)] * 6 + [row(3 * D),
                  const(WIDTH, D), const(WIDTH, D), const(WIDTH, D), const(D, D)],
        out_specs=row(D),
        out_shape=jax.ShapeDtypeStruct((M, D), F32),
        compiler_params=_params(("parallel",)),
        name="merge_out",
    )(x2, gate_mod, ya, za, yb, zb, yc, zc, gates, wl["w_br_rwkv"], wl["w_br_mla"], wl["w_br_sb"], wl["w_out"])


def _rot_matrix(width, offset):
    half = ROPE // 2
    m = np.zeros((width, width), np.float32)
    for i in range(half):
        m[offset + half + i, offset + i] = -1.0
        m[offset + i, offset + half + i] = 1.0
    return jnp.asarray(m, BF16)


def _prep_layer(P, l):
    row = lambda a: a[l].reshape(1, -1)
    zeros_lora = jnp.zeros((LORA, WIDTH), F32)
    w_uq = P["mla_w_uq"][l].reshape(Q_LORA, HEADS, QK).transpose(1, 0, 2)
    w_ukv = P["mla_w_ukv"][l].reshape(KV_LORA, HEADS, 2 * HEAD_DIM)
    w_uk = jnp.pad(w_ukv[:, :, :NOPE].transpose(1, 0, 2), ((0, 0), (0, 0), (0, ROPE)))
    w_uv = w_ukv[:, :, NOPE:].reshape(KV_LORA, PAIRS, LANES).transpose(1, 0, 2)
    e96 = np.zeros((ROPE, QK), np.float32)
    e96[np.arange(ROPE), NOPE + np.arange(ROPE)] = 1.0
    half = ROPE // 2
    w_uq_rot = jnp.concatenate([jnp.zeros_like(w_uq[:, :, :NOPE]), -w_uq[:, :, NOPE + half:], w_uq[:, :, NOPE:NOPE + half]],
                               axis=-1)
    g_rope = P["mla_qn_rope"][l]
    qn_rot = jnp.concatenate([jnp.zeros((NOPE,), F32), g_rope[half:], g_rope[:half]]).reshape(1, QK)
    return dict(
        norm_g=row(P["norm_g"]),
        mu=row(P["rwkv_mu"]), w0=row(P["rwkv_w0"]), a0=row(P["rwkv_a0"]), k_k=row(P["rwkv_k_k"]),
        k_a=row(P["rwkv_k_a"]), r_k=row(P["rwkv_r_k"]), gn_g=row(P["rwkv_gn_g"]), gn_b=row(P["rwkv_gn_b"]),
        w_up=jnp.concatenate([P["rwkv_w_up"][l], zeros_lora], axis=0).astype(BF16),
        a_up=jnp.concatenate([zeros_lora, P["rwkv_a_up"][l]], axis=0).astype(BF16),
        q_norm=row(P["mla_q_norm"]), w_uq=w_uq.astype(BF16), w_uq_rot=w_uq_rot.astype(BF16), qn_rot=qn_rot,
        qn=jnp.concatenate([P["mla_qn_nope"][l], P["mla_qn_rope"][l]]).reshape(1, QK),
        kv_norm=row(P["mla_kv_norm"]), kn_rope=row(P["mla_kn_rope"]),
        kn_nope=jnp.pad(P["mla_kn_nope"][l], (0, ROPE)).reshape(1, QK),
        w_uk=w_uk.astype(BF16), w_uv=w_uv.astype(BF16),
        p32=_rot_matrix(ROPE, 0), e96=jnp.asarray(e96, BF16),
        w_br_rwkv=P["w_br_rwkv"][l].astype(BF16), w_br_mla=P["w_br_mla"][l].astype(BF16),
        w_br_sb=P["w_br_sb"][l].astype(BF16), w_out=P["w_out"][l].astype(BF16),
    )


def _rope_tables(past, T):
    half = ROPE // 2
    inv = ROPE_THETA ** (-jnp.arange(half, dtype=F32) / half)
    ang = (past + jnp.arange(T, dtype=jnp.int32)).astype(F32)[:, None] * inv
    c32 = jnp.tile(jnp.cos(ang), (1, 2))
    s32 = jnp.tile(jnp.sin(ang), (1, 2))
    c96 = jnp.concatenate([jnp.ones((T, NOPE), F32), c32], axis=1)
    s96 = jnp.concatenate([jnp.zeros((T, NOPE), F32), s32], axis=1)
    return c96, s96, c32, s32


def _pad_rows(a, rows):
    return a if a.shape[1] == rows else jnp.pad(a, ((0, 0), (0, rows - a.shape[1])) + ((0, 0),) * (a.ndim - 2))


def _trunk(x, mods, pasts, layers, w_in_all, cfg):
    B, T, D = x.shape
    past = cfg["past"]
    S = past + T
    s_pad = -(-S // cfg["kv_mult"]) * cfg["kv_mult"]
    tabs = _rope_tables(past, T)
    x2 = x.reshape(B * T, D)
    new = []
    for l, wl in enumerate(layers):
        shift, scale, gate = (mods[l][:, None, i * D:(i + 1) * D] for i in range(3))
        outs = in_proj(x2, scale, shift, wl["norm_g"], w_in_all, l, T, cfg["tm"])
        pr, zA, cq, ckv, kr, zB, sq, sk, sv, zC, gates, sk_b, sv_b = outs
        seq = lambda a: a.reshape(B, T, a.shape[-1])
        if pasts is None:
            s0 = jnp.zeros((B, PAIRS, LANES, LANES), F32)
            shift0 = jnp.zeros((B, 1, SHIFT_W), F32)
        else:
            s0, shift0 = _state_to_bd(pasts[l][0]), pasts[l][1]
        pr3 = seq(pr)
        yA, s_fin = rwkv_branch(pr3, shift0, s0, wl, cfg["rwkv_n"])
        q, ckv_n, k_rope = mla_q(seq(cq), seq(ckv), seq(kr), tabs, wl, cfg["tm_q"])
        sk3, sv3 = seq(sk), seq(sv)
        if pasts is None:
            ckv_all, kr_all, k_all, v_all = ckv_n, k_rope, seq(sk_b), seq(sv_b)
        else:
            _, _, ckv0, kr0, k0, v0 = pasts[l]
            ckv_all = jnp.concatenate([ckv0, ckv_n], axis=1)
            kr_all = jnp.concatenate([kr0, k_rope], axis=1)
            k_all = jnp.concatenate([k0.reshape(B, past, WIDTH).astype(BF16), seq(sk_b)], axis=1)
            v_all = jnp.concatenate([v0.reshape(B, past, WIDTH).astype(BF16), seq(sv_b)], axis=1)
        ckv_all, kr_all, k_all, v_all = (_pad_rows(a, s_pad) for a in (ckv_all, kr_all, k_all, v_all))
        kf, vf = mla_kv(ckv_all, kr_all, wl, cfg["ts"])
        yB = mla_attn(q, kf, vf, past, S, cfg["tq"], cfg["tk"], cfg["tkd"])
        yC = sb_attn(seq(sq), k_all, v_all, past, cfg["sb_tq"], cfg["sb_tk"])
        x2 = merge_out(x2, gate, yA.reshape(B * T, WIDTH), zA, yB.reshape(B * T, WIDTH), zB,
                       yC.reshape(B * T, WIDTH), zC, gates, wl, T, cfg["tm"])
        new.append((_bd_to_state(s_fin), pr3[:, -1:], ckv_n, k_rope,
                    sk3.reshape(B, T, HEADS, HEAD_DIM), sv3.reshape(B, T, HEADS, HEAD_DIM)))
    stacked = [jnp.stack([st[i] for st in new], axis=0) for i in range(6)]
    return x2.reshape(B, T, D), stacked


PROMPT_CFG = dict(past=0, tm=512, rwkv_n=128, tm_q=512, ts=512, tq=512, tk=2048, tkd=512, sb_tq=512, sb_tk=256,
                  kv_mult=1024)
SAMPLE_CFG = dict(past=2048, tm=16, rwkv_n=16, tm_q=16, ts=2560, tq=16, tk=2560, tkd=2560, sb_tq=16, sb_tk=512,
                  kv_mult=512)


def kernel(x_prompt, x_sample, state_rwkv_wkv, state_rwkv_shift, cache_mla_ckv, cache_mla_krope, cache_sb_k, cache_sb_v, c_prompt, c_sample, w_ada, b_ada, norm_g, w_in, rwkv_mu, rwkv_w0, rwkv_w_up, rwkv_a0, rwkv_a_up, rwkv_k_k, rwkv_k_a, rwkv_r_k, rwkv_gn_g, rwkv_gn_b, w_br_rwkv, mla_q_norm, mla_w_uq, mla_kv_norm, mla_w_ukv, mla_qn_nope, mla_qn_rope, mla_kn_nope, mla_kn_rope, w_br_mla, w_br_sb, w_out):
    P = dict(w_in=w_in, norm_g=norm_g, rwkv_mu=rwkv_mu, rwkv_w0=rwkv_w0, rwkv_w_up=rwkv_w_up, rwkv_a0=rwkv_a0,
             rwkv_a_up=rwkv_a_up, rwkv_k_k=rwkv_k_k, rwkv_k_a=rwkv_k_a, rwkv_r_k=rwkv_r_k, rwkv_gn_g=rwkv_gn_g,
             rwkv_gn_b=rwkv_gn_b, w_br_rwkv=w_br_rwkv, mla_q_norm=mla_q_norm, mla_w_uq=mla_w_uq,
             mla_kv_norm=mla_kv_norm, mla_w_ukv=mla_w_ukv, mla_qn_nope=mla_qn_nope, mla_qn_rope=mla_qn_rope,
             mla_kn_nope=mla_kn_nope, mla_kn_rope=mla_kn_rope, w_br_mla=w_br_mla, w_br_sb=w_br_sb, w_out=w_out)
    depth = w_in.shape[0]
    bp = x_prompt.shape[0]
    layers = [_prep_layer(P, l) for l in range(depth)]
    offs = np.cumsum([0] + [wd for _, wd in IN_GROUPS])
    src = {name: (offs[i], offs[i + 1]) for i, (name, _) in enumerate(IN_GROUPS)}
    w_in_all = jnp.concatenate([w_in[:, :, src[name][0]:src[name][1]] for name, _ in _in_weight_layout()[0]],
                               axis=-1).astype(BF16)
    mods = ada_mod(jnp.concatenate([c_prompt, c_sample], axis=0), w_ada, b_ada)
    y_p, st_p = _trunk(x_prompt, mods[:, :bp], None, layers, w_in_all, PROMPT_CFG)
    pasts = [(state_rwkv_wkv[l], state_rwkv_shift[l], cache_mla_ckv[l], cache_mla_krope[l],
              cache_sb_k[l], cache_sb_v[l]) for l in range(depth)]
    assert cache_mla_ckv.shape[2] == SAMPLE_CFG["past"]
    y_s, st_s = _trunk(x_sample, mods[:, bp:], pasts, layers, w_in_all, SAMPLE_CFG)
    return (y_p, y_s, *st_p, *st_s)
```
